```python
import jax, jax.numpy as jnp
from jax import lax
import numpy as np

D_MODEL = 1024
BATCH = 32
SEQ = 256
DEPTH = 2
DEC_BATCH = 8
DEC_SEQ = 1024
PAST_LEN = 512

GRID_W = 64
D_ATTN = D_MODEL // 2
HEAD_DIM = 64
N_HEADS_A = D_ATTN // HEAD_DIM
WIN_ROWS = 8
WIN_COLS = 16
D_POOL = D_MODEL // 4
POOL_WINDOWS = (2, 4, 8, 16)
POOL_GROUP_DIM = D_POOL // len(POOL_WINDOWS)
D_CONV = D_MODEL // 4
CONV_WIDTH = 31
D_MIX = D_ATTN + D_POOL + D_CONV
D_IN = 3 * D_ATTN + D_POOL + 2 * D_CONV
N_EXPERTS = 16
N_GROUPS = 4
EXPERTS_PER_GROUP = N_EXPERTS // N_GROUPS
TOP_K = 2
D_FF = 512
EPS = 1e-6

kernel_name = "hybrid_natten_pool_conformer_moe_diffusion_step"


def rms_norm(x, g):
    xf = x.astype(jnp.float32)
    y = xf * lax.rsqrt(jnp.mean(xf * xf, axis=-1, keepdims=True) + EPS)
    return (y * g.astype(jnp.float32)).astype(x.dtype)


def ada_modulation(cond, w, b):
    m = jax.nn.silu(cond) @ w + b
    return jnp.split(m, 6, axis=-1)


def input_projection(h, w):
    B, L, _ = h.shape
    z = h @ w
    q = z[..., 0:D_ATTN].reshape(B, L, N_HEADS_A, HEAD_DIM)
    k = z[..., D_ATTN:2 * D_ATTN].reshape(B, L, N_HEADS_A, HEAD_DIM)
    v = z[..., 2 * D_ATTN:3 * D_ATTN].reshape(B, L, N_HEADS_A, HEAD_DIM)
    u_pool = z[..., 3 * D_ATTN:3 * D_ATTN + D_POOL]
    u_conv = z[..., 3 * D_ATTN + D_POOL:]
    return q, k, v, u_pool, u_conv


def context_attention(q, k, v):
    B, L = q.shape[:2]
    s = jnp.einsum('blhd,bmhd->bhlm', q, k).astype(jnp.float32) * (HEAD_DIM ** -0.5)
    p = jax.nn.softmax(s, axis=-1).astype(v.dtype)
    return jnp.einsum('bhlm,bmhd->blhd', p, v).reshape(B, L, D_ATTN)


def neighbourhood_attention(q, k, v, k_ctx, v_ctx, rpb):
    B, N, H, Dh = q.shape
    rows = N // GRID_W
    kr = min(WIN_ROWS, rows)
    r = np.arange(rows)
    rs = np.clip(r - kr // 2, 0, rows - kr)
    row_idx = rs[:, None] + np.arange(kr)[None, :]
    col = np.arange(GRID_W)
    cs = np.clip(col - WIN_COLS // 2, 0, GRID_W - WIN_COLS)
    col_mask = (col[None, :] >= cs[:, None]) & (col[None, :] < cs[:, None] + WIN_COLS)
    dr = row_idx - r[:, None] + (WIN_ROWS - 1)
    dc = np.clip(col[None, :] - col[:, None], -(WIN_COLS - 1), WIN_COLS - 1) + (WIN_COLS - 1)
    bias = rpb[:, dr[:, None, :, None], dc[None, :, None, :]]

    qg = q.reshape(B, rows, GRID_W, H, Dh)
    kg = k.reshape(B, rows, GRID_W, H, Dh)[:, row_idx]
    vg = v.reshape(B, rows, GRID_W, H, Dh)[:, row_idx]
    scale = HEAD_DIM ** -0.5
    s_loc = jnp.einsum('brchd,brjkhd->bhrcjk', qg, kg).astype(jnp.float32) * scale + bias.astype(jnp.float32)
    s_loc = jnp.where(col_mask[:, None, :], s_loc, -jnp.inf)
    s_ctx = jnp.einsum('brchd,bmhd->bhrcm', qg, k_ctx).astype(jnp.float32) * scale
    n_loc = kr * GRID_W
    s = jnp.concatenate([s_loc.reshape(B, H, rows, GRID_W, n_loc), s_ctx], axis=-1)
    p = jax.nn.softmax(s, axis=-1).astype(v.dtype)
    p_loc = p[..., :n_loc].reshape(B, H, rows, GRID_W, kr, GRID_W)
    p_ctx = p[..., n_loc:]
    out = (jnp.einsum('bhrcjk,brjkhd->brchd', p_loc, vg)
           + jnp.einsum('bhrcm,bmhd->brchd', p_ctx, v_ctx))
    return out.reshape(B, N, H * Dh)


def multiscale_pool(u, pool_w, pool_scale):
    B, L, C = u.shape
    uf = u.astype(jnp.float32)
    csum = jnp.concatenate([jnp.zeros((B, 1, C), jnp.float32), jnp.cumsum(uf, axis=1)], axis=1)
    t = np.arange(L)
    outs = []
    for g, w in enumerate(POOL_WINDOWS):
        lo = np.clip(t - w // 2, 0, L - 1)
        hi = np.clip(t + w - w // 2 - 1, 0, L - 1)
        cnt = (hi - lo + 1).astype(np.float32)
        sl = slice(g * POOL_GROUP_DIM, (g + 1) * POOL_GROUP_DIM)
        cg = csum[:, :, sl]
        mean = (cg[:, hi + 1] - cg[:, lo]) / jnp.asarray(cnt)[None, :, None]
        diff = (mean - uf[:, :, sl]).astype(u.dtype)
        outs.append(diff @ pool_w[g])
    return jnp.concatenate(outs, axis=-1) * pool_scale


def conformer_conv(u, conv_w, conv_b, ln_g, ln_b, pw_w, pw_b):
    a, g = jnp.split(u, 2, axis=-1)
    y = a * jax.nn.sigmoid(g)
    y = lax.conv_general_dilated(
        y, conv_w[:, None, :].astype(y.dtype), window_strides=(1,),
        padding=[(CONV_WIDTH // 2, CONV_WIDTH // 2)],
        dimension_numbers=('NWC', 'WIO', 'NWC'), feature_group_count=D_CONV) + conv_b
    yf = y.astype(jnp.float32)
    mu = jnp.mean(yf, axis=-1, keepdims=True)
    var = jnp.mean((yf - mu) ** 2, axis=-1, keepdims=True)
    yn = (yf - mu) * lax.rsqrt(var + EPS) * ln_g.astype(jnp.float32) + ln_b.astype(jnp.float32)
    yn = jax.nn.silu(yn).astype(u.dtype)
    return yn @ pw_w + pw_b


def mix_and_project(a, u_pool, u_conv, pool_w, pool_scale, conv_w, conv_b, ln_g, ln_b, pw_w, pw_b, w_out):
    m = jnp.concatenate([
        a,
        multiscale_pool(u_pool, pool_w, pool_scale),
        conformer_conv(u_conv, conv_w, conv_b, ln_g, ln_b, pw_w, pw_b)], axis=-1)
    return m @ w_out


def grouped_moe(h, w_router, b_router, w_gate, w_up, w_down):
    B, L, D = h.shape
    t = h.reshape(B * L, D)
    aff = jax.nn.sigmoid((t @ w_router).astype(jnp.float32))
    sel = aff + b_router.astype(jnp.float32)
    gscore = lax.top_k(sel.reshape(-1, N_GROUPS, EXPERTS_PER_GROUP), 2)[0].sum(-1)
    gbest = jnp.argmax(gscore, axis=-1)
    in_group = (jnp.arange(N_EXPERTS) // EXPERTS_PER_GROUP)[None, :] == gbest[:, None]
    _, idx = lax.top_k(jnp.where(in_group, sel, -jnp.inf), TOP_K)
    wsel = jnp.take_along_axis(aff, idx, axis=-1)
    wsel = wsel / jnp.sum(wsel, axis=-1, keepdims=True)
    gate = jnp.sum(jax.nn.one_hot(idx, N_EXPERTS, dtype=jnp.float32) * wsel[..., None], axis=1)
    hid = jax.nn.silu(jnp.einsum('td,edf->tef', t, w_gate)) * jnp.einsum('td,edf->tef', t, w_up)
    y = jnp.einsum('tef,efd->td', hid * gate[:, :, None].astype(hid.dtype), w_down)
    return y.reshape(B, L, D)


def setup_inputs(seed: int = 0) -> dict:
    key = jax.random.key(seed)
    ks = jax.random.split(key, 32)

    def nrm(k, shape, s):
        return jax.random.normal(k, shape, jnp.float32) * s

    return {
        "x_prompt": nrm(ks[0], (BATCH, SEQ, D_MODEL), 1.0),
        "x_sample": nrm(ks[1], (DEC_BATCH, DEC_SEQ, D_MODEL), 1.0),
        "cache_k": nrm(ks[2], (DEC_BATCH, DEPTH, PAST_LEN, N_HEADS_A, HEAD_DIM), 1.0),
        "cache_v": nrm(ks[3], (DEC_BATCH, DEPTH, PAST_LEN, N_HEADS_A, HEAD_DIM), 1.0),
        "c": nrm(ks[4], (DEC_BATCH, D_MODEL), 1.0),
        "c_ctx": nrm(ks[5], (D_MODEL,), 1.0),
        "w_ada": nrm(ks[6], (DEPTH, D_MODEL, 6 * D_MODEL), 0.5 * D_MODEL ** -0.5),
        "b_ada": nrm(ks[7], (DEPTH, 6 * D_MODEL), 0.02),
        "norm1_g": 1.0 + nrm(ks[8], (DEPTH, D_MODEL), 0.1),
        "norm2_g": 1.0 + nrm(ks[9], (DEPTH, D_MODEL), 0.1),
        "w_in": nrm(ks[10], (DEPTH, D_MODEL, D_IN), D_MODEL ** -0.5),
        "w_out": nrm(ks[11], (DEPTH, D_MIX, D_MODEL), D_MIX ** -0.5),
        "rpb": nrm(ks[12], (DEPTH, N_HEADS_A, 2 * WIN_ROWS - 1, 2 * WIN_COLS - 1), 0.1),
        "pool_w": nrm(ks[13], (DEPTH, len(POOL_WINDOWS), POOL_GROUP_DIM, POOL_GROUP_DIM), POOL_GROUP_DIM ** -0.5),
        "pool_scale": 1.0 + nrm(ks[14], (DEPTH, D_POOL), 0.1),
        "conv_w": nrm(ks[15], (DEPTH, CONV_WIDTH, D_CONV), CONV_WIDTH ** -0.5),
        "conv_b": nrm(ks[16], (DEPTH, D_CONV), 0.02),
        "conv_ln_g": 1.0 + nrm(ks[17], (DEPTH, D_CONV), 0.1),
        "conv_ln_b": nrm(ks[18], (DEPTH, D_CONV), 0.02),
        "conv_pw_w": nrm(ks[19], (DEPTH, D_CONV, D_CONV), D_CONV ** -0.5),
        "conv_pw_b": nrm(ks[20], (DEPTH, D_CONV), 0.02),
        "w_router": nrm(ks[21], (D_MODEL, N_EXPERTS), D_MODEL ** -0.5),
        "b_router": nrm(ks[22], (N_EXPERTS,), 0.01),
        "moe_w_gate": nrm(ks[23], (DEPTH, N_EXPERTS, D_MODEL, D_FF), D_MODEL ** -0.5),
        "moe_w_up": nrm(ks[24], (DEPTH, N_EXPERTS, D_MODEL, D_FF), D_MODEL ** -0.5),
        "moe_w_down": nrm(ks[25], (DEPTH, N_EXPERTS, D_FF, D_MODEL), D_FF ** -0.5),
        "final_g": 1.0 + nrm(ks[26], (D_MODEL,), 0.1),
    }


def reference(x_prompt, x_sample, cache_k, cache_v, c, c_ctx, w_ada, b_ada, norm1_g, norm2_g,
              w_in, w_out, rpb, pool_w, pool_scale, conv_w, conv_b, conv_ln_g, conv_ln_b,
              conv_pw_w, conv_pw_b, w_router, b_router, moe_w_gate, moe_w_up, moe_w_down, final_g):
    xp = x_prompt
    xs = x_sample
    new_k, new_v = [], []
    for l in range(DEPTH):
        sh1, sc1, g1, sh2, sc2, g2 = ada_modulation(c_ctx, w_ada[l], b_ada[l])
        h = rms_norm(xp, norm1_g[l]) * (1.0 + sc1) + sh1
        q, k, v, u_pool, u_conv = input_projection(h, w_in[l])
        a = context_attention(q, k, v)
        xp = xp + g1 * mix_and_project(a, u_pool, u_conv, pool_w[l], pool_scale[l], conv_w[l], conv_b[l],
                                       conv_ln_g[l], conv_ln_b[l], conv_pw_w[l], conv_pw_b[l], w_out[l])
        h2 = rms_norm(xp, norm2_g[l]) * (1.0 + sc2) + sh2
        xp = xp + g2 * grouped_moe(h2, w_router, b_router, moe_w_gate[l], moe_w_up[l], moe_w_down[l])
        new_k.append(k)
        new_v.append(v)

        sh1, sc1, g1, sh2, sc2, g2 = [m[:, None, :] for m in ada_modulation(c, w_ada[l], b_ada[l])]
        h = rms_norm(xs, norm1_g[l]) * (1.0 + sc1) + sh1
        q, k, v, u_pool, u_conv = input_projection(h, w_in[l])
        a = neighbourhood_attention(q, k, v, cache_k[:, l], cache_v[:, l], rpb[l])
        xs = xs + g1 * mix_and_project(a, u_pool, u_conv, pool_w[l], pool_scale[l], conv_w[l], conv_b[l],
                                       conv_ln_g[l], conv_ln_b[l], conv_pw_w[l], conv_pw_b[l], w_out[l])
        h2 = rms_norm(xs, norm2_g[l]) * (1.0 + sc2) + sh2
        xs = xs + g2 * grouped_moe(h2, w_router, b_router, moe_w_gate[l], moe_w_up[l], moe_w_down[l])

    y_prompt = rms_norm(xp, final_g)
    y_sample = rms_norm(xs, final_g)
    new_cache_k = jnp.stack(new_k, axis=1)
    new_cache_v = jnp.stack(new_v, axis=1)
    return (y_prompt, y_sample, new_cache_k, new_cache_v)
```

```python
import functools

import jax
import jax.numpy as jnp
from jax import lax
from jax.experimental import pallas as pl
from jax.experimental.pallas import tpu as pltpu

D_MODEL = 1024
BATCH = 32
SEQ = 256
DEPTH = 2
DEC_BATCH = 8
DEC_SEQ = 1024
PAST_LEN = 512
GRID_W = 64
GRID_ROWS = DEC_SEQ // GRID_W
D_ATTN = 512
HEAD_DIM = 64
N_HEADS = 8
N_PAIRS = N_HEADS // 2
WIN_ROWS = 8
WIN_COLS = 16
D_POOL = 256
D_CONV = 256
CONV_WIDTH = 31
D_IN = 3 * D_ATTN + D_POOL + 2 * D_CONV
N_EXPERTS = 16
N_GROUPS = 4
GROUP_SIZE = N_EXPERTS // N_GROUPS
D_FF = 512
EPS = 1e-6
N_MOD_ROWS = 16
N_DR = 2 * WIN_ROWS - 1
N_DC = 2 * WIN_COLS - 1
N_BIAS_TILES = N_DR - 1
SEQ_PAD = 16
LANES = 128
MOE_TM = 512
VMEM_LIMIT = 60 * 1024 * 1024

F32 = jnp.float32
BF16 = jnp.bfloat16
HIGHEST = lax.Precision.HIGHEST
NT_DIMS = (((1,), (1,)), ((), ()))


def _rms(x, g):
    return x * lax.rsqrt(jnp.mean(x * x, axis=-1, keepdims=True) + EPS) * g


def _silu(x):
    return x * jax.nn.sigmoid(x)


def _ada_kernel(cond_ref, w_ref, b_ref, o_ref):
    o_ref[...] = jnp.dot(_silu(cond_ref[...]), w_ref[...], preferred_element_type=F32,
                         precision=HIGHEST) + b_ref[...]


def _ada_modulation(cond, w_ada, b_ada):
    n_col = 6 * D_MODEL // D_MODEL
    return pl.pallas_call(
        _ada_kernel,
        grid=(DEPTH, n_col),
        in_specs=[
            pl.BlockSpec((N_MOD_ROWS, D_MODEL), lambda l, n: (0, 0)),
            pl.BlockSpec((None, D_MODEL, D_MODEL), lambda l, n: (l, 0, n)),
            pl.BlockSpec((None, 1, D_MODEL), lambda l, n: (l, 0, n)),
        ],
        out_specs=pl.BlockSpec((None, N_MOD_ROWS, D_MODEL), lambda l, n: (l, 0, n)),
        out_shape=jax.ShapeDtypeStruct((DEPTH, N_MOD_ROWS, 6 * D_MODEL), F32),
        name="ada_modulation",
    )(cond, w_ada, b_ada.reshape(DEPTH, 1, 6 * D_MODEL))


def _rpb_kernel(rpb_ref, o_ref):
    base = (pl.program_id(0) * N_HEADS + pl.program_id(1)) * (N_DR * N_DC)
    cq = lax.broadcasted_iota(jnp.int32, (GRID_W, LANES), 0)
    lane = lax.broadcasted_iota(jnp.int32, (GRID_W, LANES), 1)
    ck = lane & (GRID_W - 1)
    right = lane >= GRID_W
    dcm = jnp.clip(ck - cq, -(WIN_COLS - 1), WIN_COLS - 1) + (WIN_COLS - 1)
    cs = jnp.clip(cq - WIN_COLS // 2, 0, GRID_W - WIN_COLS)
    valid = (ck >= cs) & (ck < cs + WIN_COLS)
    for d in range(N_BIAS_TILES):
        acc = jnp.zeros((GRID_W, LANES), F32)
        for dc in range(N_DC):
            v = jnp.where(right, rpb_ref[base + (d + 1) * N_DC + dc], rpb_ref[base + d * N_DC + dc])
            acc = jnp.where(dcm == dc, v, acc)
        o_ref[d] = jnp.where(valid, acc, -jnp.inf)


def _rpb_tiles(rpb):
    return pl.pallas_call(
        _rpb_kernel,
        grid=(DEPTH, N_HEADS),
        in_specs=[pl.BlockSpec(memory_space=pltpu.SMEM)],
        out_specs=pl.BlockSpec((None, None, N_BIAS_TILES, GRID_W, LANES), lambda l, h: (l, h, 0, 0, 0)),
        out_shape=jax.ShapeDtypeStruct((DEPTH, N_HEADS, N_BIAS_TILES, GRID_W, LANES), F32),
        name="rpb_tiles",
    )(rpb.reshape(-1))


def _shift_rows(a, s):
    return a if s == 0 else pltpu.roll(a, s % a.shape[0], 0)


def _pad_rows(a):
    return jnp.concatenate([a, jnp.zeros((SEQ_PAD, a.shape[1]), a.dtype)], axis=0)


def _softmax_parts(s):
    m = jnp.max(s, axis=-1, keepdims=True)
    e = jnp.exp(s - m)
    return e, jnp.sum(e, axis=-1, keepdims=True)


def _context_attention(qkv_scr, m_scr, n):
    left = lax.broadcasted_iota(jnp.int32, (n, LANES), 1) < HEAD_DIM
    for p in range(N_PAIRS):
        q = qkv_scr[p]
        k = qkv_scr[N_PAIRS + p]
        v = qkv_scr[2 * N_PAIRS + p]
        zero = jnp.zeros_like(q)
        outs = []
        for qh in (jnp.where(left, q, zero), jnp.where(left, zero, q)):
            s = lax.dot_general(qh, k, NT_DIMS, preferred_element_type=F32)
            e, l = _softmax_parts(s)
            outs.append(jnp.dot(e.astype(BF16), v, preferred_element_type=F32) / l)
        m_scr[:, p * LANES:(p + 1) * LANES] = jnp.where(left, outs[0], outs[1]).astype(BF16)


def _latent_attention(qkv_scr, kc_scr, vc_scr, bias_ref, a_scr):
    left = lax.broadcasted_iota(jnp.int32, (GRID_W, LANES), 1) < HEAD_DIM
    n_loc = WIN_ROWS * GRID_W

    def body(i, carry):
        p = i // GRID_ROWS
        r = i % GRID_ROWS
        rs = jnp.clip(r - WIN_ROWS // 2, 0, GRID_ROWS - WIN_ROWS)
        q = qkv_scr[p, pl.ds(pl.multiple_of(r * GRID_W, GRID_W), GRID_W), :]
        zero = jnp.zeros_like(q)
        qm = jnp.concatenate([jnp.where(left, q, zero), jnp.where(left, zero, q)], axis=0)
        k0 = pl.multiple_of(rs * GRID_W, GRID_W)
        kl = qkv_scr[N_PAIRS + p, pl.ds(k0, n_loc), :]
        vl = qkv_scr[2 * N_PAIRS + p, pl.ds(k0, n_loc), :]
        d0 = rs - r + (WIN_ROWS - 1)
        bias = jnp.concatenate([
            jnp.concatenate([bias_ref[2 * p + hh, d0 + 2 * j] for j in range(WIN_ROWS // 2)], axis=1)
            for hh in range(2)], axis=0)
        s_loc = lax.dot_general(qm, kl, NT_DIMS, preferred_element_type=F32) + bias
        s_ctx = lax.dot_general(qm, kc_scr[p], NT_DIMS, preferred_element_type=F32)
        m = jnp.maximum(jnp.max(s_loc, axis=-1, keepdims=True), jnp.max(s_ctx, axis=-1, keepdims=True))
        e_loc = jnp.exp(s_loc - m)
        e_ctx = jnp.exp(s_ctx - m)
        l = jnp.sum(e_loc, axis=-1, keepdims=True) + jnp.sum(e_ctx, axis=-1, keepdims=True)
        o = (jnp.dot(e_loc.astype(BF16), vl, preferred_element_type=F32)
             + jnp.dot(e_ctx.astype(BF16), vc_scr[p], preferred_element_type=F32)) / l
        out = jnp.where(left, o[:GRID_W], o[GRID_W:])
        a_scr[p, pl.ds(pl.multiple_of(r * GRID_W, GRID_W), GRID_W), :] = out.astype(BF16)
        return carry

    lax.fori_loop(0, N_PAIRS * GRID_ROWS, body, 0)


def _router_gates(h2, wrt_ref, br_ref, n):
    logits = lax.dot_general(wrt_ref[...], h2, NT_DIMS, preferred_element_type=F32, precision=HIGHEST)
    aff = jax.nn.sigmoid(logits)
    sel = aff + br_ref[...]
    rows = [sel[e:e + 1, :] for e in range(N_EXPERTS)]
    arow = [aff[e:e + 1, :] for e in range(N_EXPERTS)]
    gscore = []
    for g in range(N_GROUPS):
        r4 = rows[g * GROUP_SIZE:(g + 1) * GROUP_SIZE]
        best = None
        for i in range(GROUP_SIZE):
            for j in range(i + 1, GROUP_SIZE):
                s = r4[i] + r4[j]
                best = s if best is None else jnp.maximum(best, s)
        gscore.append(best)
    gbest = jnp.zeros((1, n), jnp.int32)
    gval = gscore[0]
    for g in range(1, N_GROUPS):
        upd = gscore[g] > gval
        gbest = jnp.where(upd, g, gbest)
        gval = jnp.where(upd, gscore[g], gval)
    picked = []
    for e in range(N_EXPERTS):
        g = e // GROUP_SIZE
        rank = jnp.zeros((1, n), jnp.int32)
        for j in range(g * GROUP_SIZE, (g + 1) * GROUP_SIZE):
            if j == e:
                continue
            ahead = (rows[j] > rows[e]) | (rows[j] == rows[e]) if j < e else rows[j] > rows[e]
            rank = rank + ahead.astype(jnp.int32)
        picked.append((rank < 2) & (gbest == g))
    denom = jnp.zeros((1, n), F32)
    for e in range(N_EXPERTS):
        denom = denom + jnp.where(picked[e], arow[e], 0.0)
    gate_rows = [jnp.where(picked[e], arow[e] / denom, 0.0) for e in range(N_EXPERTS)]
    gate_rows.append(jnp.zeros((LANES - N_EXPERTS, n), F32))
    return jnp.concatenate(gate_rows, axis=0).T


def _mixer_kernel(*refs, latent, n):
    it = iter(refs)
    x_ref, mod_ref = next(it), next(it)
    if latent:
        ck_ref, cv_ref, bias_ref = next(it), next(it), next(it)
    (n1g_ref, n2g_ref, w_in_ref, w_out_ref, poolw_ref, pscale_ref, convw_ref, convb_ref,
     lng_ref, lnb_ref, pww_ref, pwb_ref, wrt_ref, br_ref) = [next(it) for _ in range(14)]
    x1_ref, h2_ref, gates_ref = next(it), next(it), next(it)
    if not latent:
        knew_ref, vnew_ref = next(it), next(it)
    qkv_scr, u_scr, m_scr = next(it), next(it), next(it)
    if latent:
        kc_scr, vc_scr, a_scr = next(it), next(it), next(it)

    x = x_ref[...]
    sh1, sc1, g1 = mod_ref[0:1, :], mod_ref[1:2, :], mod_ref[2:3, :]
    sh2, sc2 = mod_ref[3:4, :], mod_ref[4:5, :]

    hb = (_rms(x, n1g_ref[...]) * (1.0 + sc1) + sh1).astype(BF16)
    cw = 2 * LANES
    for j in range(D_IN // cw):
        zj = jnp.dot(hb, w_in_ref[:, j * cw:(j + 1) * cw], preferred_element_type=F32)
        if j < 6:
            part = j // 2
            zs = zj * (HEAD_DIM ** -0.5) if part == 0 else zj
            c = part * N_PAIRS + 2 * (j % 2)
            qkv_scr[c] = zs[:, :LANES].astype(BF16)
            qkv_scr[c + 1] = zs[:, LANES:].astype(BF16)
            if not latent and part == 1:
                knew_ref[:, (j % 2) * cw:(j % 2 + 1) * cw] = zj
            if not latent and part == 2:
                vnew_ref[:, (j % 2) * cw:(j % 2 + 1) * cw] = zj
        else:
            u_scr[j - 6] = zj

    if latent:
        for p in range(N_PAIRS):
            kc_scr[p] = ck_ref[:, p * LANES:(p + 1) * LANES].astype(BF16)
            vc_scr[p] = cv_ref[:, p * LANES:(p + 1) * LANES].astype(BF16)
        _latent_attention(qkv_scr, kc_scr, vc_scr, bias_ref, a_scr)
        for p in range(N_PAIRS):
            m_scr[:, p * LANES:(p + 1) * LANES] = a_scr[p]
    else:
        _context_attention(qkv_scr, m_scr, n)

    row = lax.broadcasted_iota(jnp.int32, (n, D_POOL), 0)
    grp = lax.broadcasted_iota(jnp.int32, (n, D_POOL), 1) // (D_POOL // 4)
    up = u_scr[0]
    t2 = _pad_rows(up)
    t2 = t2 + _shift_rows(t2, 1)
    t4 = t2 + _shift_rows(t2, 2)
    t8 = t4 + _shift_rows(t4, 4)
    t16 = t8 + _shift_rows(t8, 8)
    s2, s4, s8, s16 = t2[:n], _shift_rows(t4, -1)[:n], _shift_rows(t8, -3)[:n], _shift_rows(t16, -7)[:n]
    wsum = jnp.where(grp == 0, s2, jnp.where(grp == 1, s4, jnp.where(grp == 2, s8, s16)))
    half = jnp.left_shift(1, grp)
    lo = jnp.maximum(row - half, 0)
    hi = jnp.minimum(row + half - 1, n - 1)
    diff = wsum / (hi - lo + 1).astype(F32) - up
    pooled = jnp.dot(diff.astype(BF16), poolw_ref[...], preferred_element_type=F32) * pscale_ref[...]
    m_scr[:, D_ATTN:D_ATTN + D_POOL] = pooled.astype(BF16)

    y = _pad_rows(u_scr[1] * jax.nn.sigmoid(u_scr[2]))
    acc = jnp.zeros(y.shape, F32) + convb_ref[...]
    for k in range(CONV_WIDTH):
        acc = acc + convw_ref[k:k + 1, :] * _shift_rows(y, CONV_WIDTH // 2 - k)
    acc = acc[:n]
    mu = jnp.mean(acc, axis=-1, keepdims=True)
    cen = acc - mu
    var = jnp.mean(cen * cen, axis=-1, keepdims=True)
    yn = _silu(cen * lax.rsqrt(var + EPS) * lng_ref[...] + lnb_ref[...])
    conv = jnp.dot(yn.astype(BF16), pww_ref[...], preferred_element_type=F32) + pwb_ref[...]
    m_scr[:, D_ATTN + D_POOL:] = conv.astype(BF16)

    x1 = x + g1 * jnp.dot(m_scr[...], w_out_ref[...], preferred_element_type=F32)
    x1_ref[...] = x1
    h2 = _rms(x1, n2g_ref[...]) * (1.0 + sc2) + sh2
    h2_ref[...] = h2.astype(BF16)
    gates_ref[...] = _router_gates(h2, wrt_ref, br_ref, n)


def _const_spec(shape):
    nd = len(shape)
    return pl.BlockSpec(shape, lambda b: (0,) * nd, pipeline_mode=pl.Buffered(1))


def _mixer(x, mod_l, weights, *, latent, cache=None, bias=None):
    n = DEC_SEQ if latent else SEQ
    nb = x.shape[0] // n
    row0 = 1 if latent else 0
    mod_spec = pl.BlockSpec((None, 6, D_MODEL), (lambda b: (b + 1, 0, 0)) if latent else (lambda b: (0, 0, 0)))
    del row0
    in_specs = [pl.BlockSpec((n, D_MODEL), lambda b: (b, 0)), mod_spec]
    args = [x, mod_l]
    if latent:
        in_specs += [pl.BlockSpec((None, PAST_LEN, D_ATTN), lambda b: (b, 0, 0)),
                     pl.BlockSpec((None, PAST_LEN, D_ATTN), lambda b: (b, 0, 0)),
                     _const_spec(bias.shape)]
        args += [cache[0], cache[1], bias]
    in_specs += [_const_spec(w.shape) for w in weights]
    args += list(weights)
    tok = x.shape[0]
    out_shape = [jax.ShapeDtypeStruct((tok, D_MODEL), F32),
                 jax.ShapeDtypeStruct((tok, D_MODEL), BF16),
                 jax.ShapeDtypeStruct((tok, LANES), F32)]
    out_specs = [pl.BlockSpec((n, D_MODEL), lambda b: (b, 0)),
                 pl.BlockSpec((n, D_MODEL), lambda b: (b, 0)),
                 pl.BlockSpec((n, LANES), lambda b: (b, 0))]
    if not latent:
        out_shape += [jax.ShapeDtypeStruct((nb, n, D_ATTN), F32)] * 2
        out_specs += [pl.BlockSpec((None, n, D_ATTN), lambda b: (b, 0, 0))] * 2
    scratch = [pltpu.VMEM((3 * N_PAIRS, n, LANES), BF16),
               pltpu.VMEM((3, n, D_POOL), F32),
               pltpu.VMEM((n, D_MODEL), BF16)]
    if latent:
        scratch += [pltpu.VMEM((N_PAIRS, PAST_LEN, LANES), BF16),
                    pltpu.VMEM((N_PAIRS, PAST_LEN, LANES), BF16),
                    pltpu.VMEM((N_PAIRS, n, LANES), BF16)]
    return pl.pallas_call(
        functools.partial(_mixer_kernel, latent=latent, n=n),
        grid=(nb,),
        in_specs=in_specs,
        out_specs=out_specs,
        out_shape=out_shape,
        scratch_shapes=scratch,
        compiler_params=pltpu.CompilerParams(dimension_semantics=("arbitrary",),
                                             vmem_limit_bytes=VMEM_LIMIT),
        name="mixer_latent" if latent else "mixer_context",
    )(*args)


def _moe_kernel(h2_ref, gates_ref, x1_ref, mod_ref, wg_ref, wu_ref, wd_ref, fg_ref, o_ref, acc_ref, *, final):
    e = pl.program_id(1)

    @pl.when(e == 0)
    def _():
        acc_ref[...] = jnp.zeros_like(acc_ref)

    h2 = h2_ref[...]
    a = jnp.dot(h2, wg_ref[...], preferred_element_type=F32)
    b = jnp.dot(h2, wu_ref[...], preferred_element_type=F32)
    lane = lax.broadcasted_iota(jnp.int32, gates_ref.shape, 1)
    gcol = jnp.sum(jnp.where(lane == e, gates_ref[...], 0.0), axis=1, keepdims=True)
    hid = _silu(a) * b * gcol
    acc_ref[...] += jnp.dot(hid.astype(BF16), wd_ref[...], preferred_element_type=F32)

    @pl.when(e == N_EXPERTS - 1)
    def _():
        x2 = x1_ref[...] + mod_ref[5:6, :] * acc_ref[...]
        o_ref[...] = _rms(x2, fg_ref[...]) if final else x2


def _moe(h2, gates, x1, mod_l, wg, wu, wd, final_g, *, latent, final):
    tok = h2.shape[0]
    tiles_per_seq = DEC_SEQ // MOE_TM
    mod_map = (lambda i, e: (i // tiles_per_seq + 1, 0, 0)) if latent else (lambda i, e: (0, 0, 0))
    return pl.pallas_call(
        functools.partial(_moe_kernel, final=final),
        grid=(tok // MOE_TM, N_EXPERTS),
        in_specs=[
            pl.BlockSpec((MOE_TM, D_MODEL), lambda i, e: (i, 0)),
            pl.BlockSpec((MOE_TM, LANES), lambda i, e: (i, 0)),
            pl.BlockSpec((MOE_TM, D_MODEL), lambda i, e: (i, 0)),
            pl.BlockSpec((None, 6, D_MODEL), mod_map),
            pl.BlockSpec((None, D_MODEL, D_FF), lambda i, e: (e, 0, 0)),
            pl.BlockSpec((None, D_MODEL, D_FF), lambda i, e: (e, 0, 0)),
            pl.BlockSpec((None, D_FF, D_MODEL), lambda i, e: (e, 0, 0)),
            pl.BlockSpec((1, D_MODEL), lambda i, e: (0, 0)),
        ],
        out_specs=pl.BlockSpec((MOE_TM, D_MODEL), lambda i, e: (i, 0)),
        out_shape=jax.ShapeDtypeStruct((tok, D_MODEL), F32),
        scratch_shapes=[pltpu.VMEM((MOE_TM, D_MODEL), F32)],
        compiler_params=pltpu.CompilerParams(dimension_semantics=("arbitrary", "arbitrary"),
                                             vmem_limit_bytes=VMEM_LIMIT),
        name="moe",
    )(h2, gates, x1, mod_l, wg, wu, wd, final_g)


def kernel(x_prompt, x_sample, cache_k, cache_v, c, c_ctx, w_ada, b_ada, norm1_g, norm2_g, w_in, w_out, rpb, pool_w, pool_scale, conv_w, conv_b, conv_ln_g, conv_ln_b, conv_pw_w, conv_pw_b, w_router, b_router, moe_w_gate, moe_w_up, moe_w_down, final_g):
    xp = x_prompt.reshape(BATCH * SEQ, D_MODEL)
    xs = x_sample.reshape(DEC_BATCH * DEC_SEQ, D_MODEL)
    cond = jnp.concatenate([c_ctx[None, :], c, jnp.zeros((N_MOD_ROWS - 1 - DEC_BATCH, D_MODEL), F32)], axis=0)
    mod = _ada_modulation(cond, w_ada, b_ada).reshape(DEPTH, N_MOD_ROWS, 6, D_MODEL)
    bias = _rpb_tiles(rpb)
    ck = cache_k.reshape(DEC_BATCH, DEPTH, PAST_LEN, D_ATTN)
    cv = cache_v.reshape(DEC_BATCH, DEPTH, PAST_LEN, D_ATTN)
    wrt = w_router.T
    br = b_router.reshape(N_EXPERTS, 1)
    fg = final_g.reshape(1, D_MODEL)
    new_k, new_v = [], []
    for l in range(DEPTH):
        pool_bd = jnp.zeros((D_POOL, D_POOL), F32)
        gd = D_POOL // 4
        for g in range(4):
            pool_bd = pool_bd.at[g * gd:(g + 1) * gd, g * gd:(g + 1) * gd].set(pool_w[l, g])
        weights = [norm1_g[l].reshape(1, -1), norm2_g[l].reshape(1, -1),
                   w_in[l].astype(BF16), w_out[l].astype(BF16),
                   pool_bd.astype(BF16), pool_scale[l].reshape(1, -1),
                   conv_w[l], conv_b[l].reshape(1, -1),
                   conv_ln_g[l].reshape(1, -1), conv_ln_b[l].reshape(1, -1),
                   conv_pw_w[l].astype(BF16), conv_pw_b[l].reshape(1, -1), wrt, br]
        wg, wu, wd = moe_w_gate[l].astype(BF16), moe_w_up[l].astype(BF16), moe_w_down[l].astype(BF16)
        final = l == DEPTH - 1

        x1, h2, gates, k_l, v_l = _mixer(xp, mod[l], weights, latent=False)
        xp = _moe(h2, gates, x1, mod[l], wg, wu, wd, fg, latent=False, final=final)
        new_k.append(k_l)
        new_v.append(v_l)

        x1, h2, gates = _mixer(xs, mod[l], weights, latent=True, cache=(ck[:, l], cv[:, l]), bias=bias[l])
        xs = _moe(h2, gates, x1, mod[l], wg, wu, wd, fg, latent=True, final=final)

    y_prompt = xp.reshape(BATCH, SEQ, D_MODEL)
    y_sample = xs.reshape(DEC_BATCH, DEC_SEQ, D_MODEL)
    new_cache_k = jnp.stack(new_k, axis=1).reshape(BATCH, DEPTH, SEQ, N_HEADS, HEAD_DIM)
    new_cache_v = jnp.stack(new_v, axis=1).reshape(BATCH, DEPTH, SEQ, N_HEADS, HEAD_DIM)
    return (y_prompt, y_sample, new_cache_k, new_cache_v)
```

```python
import functools

import numpy as np
import jax
import jax.numpy as jnp
from jax import lax
from jax.experimental import pallas as pl
from jax.experimental.pallas import tpu as pltpu

D_MODEL = 1024
BATCH = 32
SEQ = 256
DEPTH = 2
DEC_BATCH = 8
DEC_SEQ = 1024
PAST_LEN = 512
GRID_W = 64
GRID_ROWS = DEC_SEQ // GRID_W
D_ATTN = 512
HEAD_DIM = 64
N_HEADS = 8
N_PAIRS = N_HEADS // 2
WIN_ROWS = 8
WIN_COLS = 16
D_POOL = 256
D_CONV = 256
CONV_WIDTH = 31
D_IN = 3 * D_ATTN + D_POOL + 2 * D_CONV
N_EXPERTS = 16
N_GROUPS = 4
GROUP_SIZE = N_EXPERTS // N_GROUPS
D_FF = 512
EPS = 1e-6
N_MOD_ROWS = 16
N_DR = 2 * WIN_ROWS - 1
N_DC = 2 * WIN_COLS - 1
N_BIAS_TILES = N_DR - 1
SEQ_PAD = 16
LANES = 128
ROW_SLABS = D_MODEL // LANES
N_TOKENS = BATCH * SEQ + DEC_BATCH * DEC_SEQ
PAIRS_PER_GROUP = GROUP_SIZE * (GROUP_SIZE - 1) // 2
N_BUCKETS = N_GROUPS * PAIRS_PER_GROUP
ROUTE_TM = 512
CTX_ROUTE_BLOCKS = BATCH * SEQ // ROUTE_TM
MOE_TM = 256
MOE_TILES = N_TOKENS // MOE_TM + N_BUCKETS
DMA_UNROLL = 8
VMEM_LIMIT = 60 * 1024 * 1024

F32 = jnp.float32
BF16 = jnp.bfloat16
HIGHEST = lax.Precision.HIGHEST
NT_DIMS = (((1,), (1,)), ((), ()))


def _rms(x, g):
    return x * lax.rsqrt(jnp.mean(x * x, axis=-1, keepdims=True) + EPS) * g


def _silu(x):
    return x * jax.nn.sigmoid(x)


def _ada_kernel(cond_ref, w_ref, b_ref, o_ref):
    o_ref[...] = jnp.dot(_silu(cond_ref[...]), w_ref[...], preferred_element_type=F32,
                         precision=HIGHEST) + b_ref[...]


def _ada_modulation(cond, w_ada, b_ada):
    n_col = 6 * D_MODEL // D_MODEL
    return pl.pallas_call(
        _ada_kernel,
        grid=(DEPTH, n_col),
        in_specs=[
            pl.BlockSpec((N_MOD_ROWS, D_MODEL), lambda l, n: (0, 0)),
            pl.BlockSpec((None, D_MODEL, D_MODEL), lambda l, n: (l, 0, n)),
            pl.BlockSpec((None, 1, D_MODEL), lambda l, n: (l, 0, n)),
        ],
        out_specs=pl.BlockSpec((None, N_MOD_ROWS, D_MODEL), lambda l, n: (l, 0, n)),
        out_shape=jax.ShapeDtypeStruct((DEPTH, N_MOD_ROWS, 6 * D_MODEL), F32),
        name="ada_modulation",
    )(cond, w_ada, b_ada.reshape(DEPTH, 1, 6 * D_MODEL))


def _rpb_kernel(rpb_ref, o_ref):
    base = (pl.program_id(0) * N_HEADS + pl.program_id(1)) * (N_DR * N_DC)
    cq = lax.broadcasted_iota(jnp.int32, (GRID_W, LANES), 0)
    lane = lax.broadcasted_iota(jnp.int32, (GRID_W, LANES), 1)
    ck = lane & (GRID_W - 1)
    right = lane >= GRID_W
    dcm = jnp.clip(ck - cq, -(WIN_COLS - 1), WIN_COLS - 1) + (WIN_COLS - 1)
    cs = jnp.clip(cq - WIN_COLS // 2, 0, GRID_W - WIN_COLS)
    valid = (ck >= cs) & (ck < cs + WIN_COLS)
    for d in range(N_BIAS_TILES):
        acc = jnp.zeros((GRID_W, LANES), F32)
        for dc in range(N_DC):
            v = jnp.where(right, rpb_ref[base + (d + 1) * N_DC + dc], rpb_ref[base + d * N_DC + dc])
            acc = jnp.where(dcm == dc, v, acc)
        o_ref[d] = jnp.where(valid, acc, -jnp.inf)


def _rpb_tiles(rpb):
    return pl.pallas_call(
        _rpb_kernel,
        grid=(DEPTH, N_HEADS),
        in_specs=[pl.BlockSpec(memory_space=pltpu.SMEM)],
        out_specs=pl.BlockSpec((None, None, N_BIAS_TILES, GRID_W, LANES), lambda l, h: (l, h, 0, 0, 0)),
        out_shape=jax.ShapeDtypeStruct((DEPTH, N_HEADS, N_BIAS_TILES, GRID_W, LANES), F32),
        name="rpb_tiles",
    )(rpb.reshape(-1))


def _shift_rows(a, s):
    return a if s == 0 else pltpu.roll(a, s % a.shape[0], 0)


def _pad_rows(a):
    return jnp.concatenate([a, jnp.zeros((SEQ_PAD, a.shape[1]), a.dtype)], axis=0)


def _softmax_parts(s):
    m = jnp.max(s, axis=-1, keepdims=True)
    e = jnp.exp(s - m)
    return e, jnp.sum(e, axis=-1, keepdims=True)


def _context_attention(qkv_scr, m_scr, n):
    left = lax.broadcasted_iota(jnp.int32, (n, LANES), 1) < HEAD_DIM
    for p in range(N_PAIRS):
        q = qkv_scr[p]
        k = qkv_scr[N_PAIRS + p]
        v = qkv_scr[2 * N_PAIRS + p]
        zero = jnp.zeros_like(q)
        outs = []
        for qh in (jnp.where(left, q, zero), jnp.where(left, zero, q)):
            s = lax.dot_general(qh, k, NT_DIMS, preferred_element_type=F32)
            e, l = _softmax_parts(s)
            outs.append(jnp.dot(e.astype(BF16), v, preferred_element_type=F32) / l)
        m_scr[p] = jnp.where(left, outs[0], outs[1]).astype(BF16)


def _latent_attention(qkv_scr, kc_scr, vc_scr, bias_ref, m_scr):
    left = lax.broadcasted_iota(jnp.int32, (GRID_W, LANES), 1) < HEAD_DIM
    n_loc = WIN_ROWS * GRID_W

    def body(i, carry):
        p = i // GRID_ROWS
        r = i % GRID_ROWS
        rs = jnp.clip(r - WIN_ROWS // 2, 0, GRID_ROWS - WIN_ROWS)
        q = qkv_scr[p, pl.ds(pl.multiple_of(r * GRID_W, GRID_W), GRID_W), :]
        zero = jnp.zeros_like(q)
        qm = jnp.concatenate([jnp.where(left, q, zero), jnp.where(left, zero, q)], axis=0)
        k0 = pl.multiple_of(rs * GRID_W, GRID_W)
        kl = qkv_scr[N_PAIRS + p, pl.ds(k0, n_loc), :]
        vl = qkv_scr[2 * N_PAIRS + p, pl.ds(k0, n_loc), :]
        d0 = rs - r + (WIN_ROWS - 1)
        bias = jnp.concatenate([
            jnp.concatenate([bias_ref[2 * p + hh, d0 + 2 * j] for j in range(WIN_ROWS // 2)], axis=1)
            for hh in range(2)], axis=0)
        s_loc = lax.dot_general(qm, kl, NT_DIMS, preferred_element_type=F32) + bias
        s_ctx = lax.dot_general(qm, kc_scr[p], NT_DIMS, preferred_element_type=F32)
        m = jnp.maximum(jnp.max(s_loc, axis=-1, keepdims=True), jnp.max(s_ctx, axis=-1, keepdims=True))
        e_loc = jnp.exp(s_loc - m)
        e_ctx = jnp.exp(s_ctx - m)
        l = jnp.sum(e_loc, axis=-1, keepdims=True) + jnp.sum(e_ctx, axis=-1, keepdims=True)
        o = (jnp.dot(e_loc.astype(BF16), vl, preferred_element_type=F32)
             + jnp.dot(e_ctx.astype(BF16), vc_scr[p], preferred_element_type=F32)) / l
        out = jnp.where(left, o[:GRID_W], o[GRID_W:])
        m_scr[p, pl.ds(pl.multiple_of(r * GRID_W, GRID_W), GRID_W), :] = out.astype(BF16)
        return carry

    lax.fori_loop(0, N_PAIRS * GRID_ROWS, body, 0)


def _router_gates(h2, wrt_ref, br_ref, n):
    logits = lax.dot_general(wrt_ref[...], h2, NT_DIMS, preferred_element_type=F32, precision=HIGHEST)
    aff = jax.nn.sigmoid(logits)
    sel = aff + br_ref[...]
    rows = [sel[e:e + 1, :] for e in range(N_EXPERTS)]
    arow = [aff[e:e + 1, :] for e in range(N_EXPERTS)]
    gscore = []
    for g in range(N_GROUPS):
        r4 = rows[g * GROUP_SIZE:(g + 1) * GROUP_SIZE]
        best = None
        for i in range(GROUP_SIZE):
            for j in range(i + 1, GROUP_SIZE):
                s = r4[i] + r4[j]
                best = s if best is None else jnp.maximum(best, s)
        gscore.append(best)
    gbest = jnp.zeros((1, n), jnp.int32)
    gval = gscore[0]
    for g in range(1, N_GROUPS):
        upd = gscore[g] > gval
        gbest = jnp.where(upd, g, gbest)
        gval = jnp.where(upd, gscore[g], gval)
    picked = []
    for e in range(N_EXPERTS):
        g = e // GROUP_SIZE
        rank = jnp.zeros((1, n), jnp.int32)
        for j in range(g * GROUP_SIZE, (g + 1) * GROUP_SIZE):
            if j == e:
                continue
            ahead = (rows[j] > rows[e]) | (rows[j] == rows[e]) if j < e else rows[j] > rows[e]
            rank = rank + ahead.astype(jnp.int32)
        picked.append((rank < 2) & (gbest == g))
    denom = jnp.zeros((1, n), F32)
    for e in range(N_EXPERTS):
        denom = denom + jnp.where(picked[e], arow[e], 0.0)
    gate_rows = [jnp.where(picked[e], arow[e] / denom, 0.0) for e in range(N_EXPERTS)]
    e_lo = jnp.zeros((1, n), F32)
    e_hi = jnp.zeros((1, n), F32)
    for e in range(N_EXPERTS):
        e_hi = jnp.where(picked[e], float(e), e_hi)
        e_lo = jnp.where(picked[N_EXPERTS - 1 - e], float(N_EXPERTS - 1 - e), e_lo)
    gate_rows += [e_lo, e_hi, jnp.zeros((LANES - N_EXPERTS - 2, n), F32)]
    return jnp.concatenate(gate_rows, axis=0).T


def _rows_to_2d(ref, lead=()):
    return jnp.concatenate([ref[lead + (slice(None), j, slice(None))] for j in range(ROW_SLABS)], axis=1)


def _mixer_kernel(*refs, latent, has_prev, n):
    it = iter(refs)
    x_ref, mod_ref = next(it), next(it)
    if has_prev:
        y_ref, modp_ref = next(it), next(it)
    if latent:
        ck_ref, cv_ref, bias_ref = next(it), next(it), next(it)
    (n1g_ref, w_in_ref, w_out_ref, poolw_ref, pscale_ref, convw_ref, convb_ref,
     lng_ref, lnb_ref, pww_ref, pwb_ref) = [next(it) for _ in range(11)]
    x1_ref = next(it)
    if not latent:
        knew_ref, vnew_ref = next(it), next(it)
    qkv_scr, u_scr, m_scr = next(it), next(it), next(it)
    if latent:
        kc_scr, vc_scr = next(it), next(it)

    x = x_ref[...]
    if has_prev:
        x = x + modp_ref[5:6, :] * _rows_to_2d(y_ref)
    sh1, sc1, g1 = mod_ref[0:1, :], mod_ref[1:2, :], mod_ref[2:3, :]

    hb = (_rms(x, n1g_ref[...]) * (1.0 + sc1) + sh1).astype(BF16)
    cw = 2 * LANES
    for j in range(D_IN // cw):
        zj = jnp.dot(hb, w_in_ref[:, j * cw:(j + 1) * cw], preferred_element_type=F32)
        if j < 6:
            part = j // 2
            zs = zj * (HEAD_DIM ** -0.5) if part == 0 else zj
            c = part * N_PAIRS + 2 * (j % 2)
            qkv_scr[c] = zs[:, :LANES].astype(BF16)
            qkv_scr[c + 1] = zs[:, LANES:].astype(BF16)
            if not latent and part == 1:
                knew_ref[:, (j % 2) * cw:(j % 2 + 1) * cw] = zj
            if not latent and part == 2:
                vnew_ref[:, (j % 2) * cw:(j % 2 + 1) * cw] = zj
        else:
            u_scr[j - 6] = zj

    if latent:
        for p in range(N_PAIRS):
            kc_scr[p] = ck_ref[:, p * LANES:(p + 1) * LANES].astype(BF16)
            vc_scr[p] = cv_ref[:, p * LANES:(p + 1) * LANES].astype(BF16)
        _latent_attention(qkv_scr, kc_scr, vc_scr, bias_ref, m_scr)
    else:
        _context_attention(qkv_scr, m_scr, n)

    row = lax.broadcasted_iota(jnp.int32, (n, D_POOL), 0)
    grp = lax.broadcasted_iota(jnp.int32, (n, D_POOL), 1) // (D_POOL // 4)
    up = u_scr[0]
    t2 = _pad_rows(up)
    t2 = t2 + _shift_rows(t2, 1)
    t4 = t2 + _shift_rows(t2, 2)
    t8 = t4 + _shift_rows(t4, 4)
    t16 = t8 + _shift_rows(t8, 8)
    s2, s4, s8, s16 = t2[:n], _shift_rows(t4, -1)[:n], _shift_rows(t8, -3)[:n], _shift_rows(t16, -7)[:n]
    wsum = jnp.where(grp == 0, s2, jnp.where(grp == 1, s4, jnp.where(grp == 2, s8, s16)))
    half = jnp.left_shift(1, grp)
    lo = jnp.maximum(row - half, 0)
    hi = jnp.minimum(row + half - 1, n - 1)
    diff = wsum / (hi - lo + 1).astype(F32) - up
    pooled = jnp.dot(diff.astype(BF16), poolw_ref[...], preferred_element_type=F32) * pscale_ref[...]
    m_scr[N_PAIRS] = pooled[:, :LANES].astype(BF16)
    m_scr[N_PAIRS + 1] = pooled[:, LANES:].astype(BF16)

    y = _pad_rows(u_scr[1] * jax.nn.sigmoid(u_scr[2]))
    acc = jnp.zeros(y.shape, F32) + convb_ref[...]
    for k in range(CONV_WIDTH):
        acc = acc + convw_ref[k:k + 1, :] * _shift_rows(y, CONV_WIDTH // 2 - k)
    acc = acc[:n]
    mu = jnp.mean(acc, axis=-1, keepdims=True)
    cen = acc - mu
    var = jnp.mean(cen * cen, axis=-1, keepdims=True)
    yn = _silu(cen * lax.rsqrt(var + EPS) * lng_ref[...] + lnb_ref[...])
    conv = jnp.dot(yn.astype(BF16), pww_ref[...], preferred_element_type=F32) + pwb_ref[...]
    m_scr[N_PAIRS + 2] = conv[:, :LANES].astype(BF16)
    m_scr[N_PAIRS + 3] = conv[:, LANES:].astype(BF16)

    mixed = jnp.concatenate([m_scr[j] for j in range(ROW_SLABS)], axis=1)
    x1_ref[...] = x + g1 * jnp.dot(mixed, w_out_ref[...], preferred_element_type=F32)


def _route_kernel(xp_ref, xs_ref, mod_ref, n2g_ref, wrt_ref, br_ref, hg_ref):
    x1 = lax.cond(pl.program_id(0) < CTX_ROUTE_BLOCKS, lambda: xp_ref[...], lambda: xs_ref[...])
    h2 = _rms(x1, n2g_ref[...]) * (1.0 + mod_ref[4:5, :]) + mod_ref[3:4, :]
    for j in range(ROW_SLABS):
        hg_ref[:, j, :] = h2[:, j * LANES:(j + 1) * LANES]
    hg_ref[:, ROW_SLABS, :] = _router_gates(h2, wrt_ref, br_ref, ROUTE_TM)


def _route(x1p, x1s, mod_l, n2g, wrt, br):
    per_seq = DEC_SEQ // ROUTE_TM
    last_ctx = CTX_ROUTE_BLOCKS - 1
    return pl.pallas_call(
        _route_kernel,
        grid=(N_TOKENS // ROUTE_TM,),
        in_specs=[
            pl.BlockSpec((ROUTE_TM, D_MODEL), lambda i: (jnp.minimum(i, last_ctx), 0)),
            pl.BlockSpec((ROUTE_TM, D_MODEL), lambda i: (jnp.maximum(i - CTX_ROUTE_BLOCKS, 0), 0)),
            pl.BlockSpec((None, 6, D_MODEL),
                         lambda i: (jnp.where(i < CTX_ROUTE_BLOCKS, 0, (i - CTX_ROUTE_BLOCKS) // per_seq + 1), 0, 0)),
            pl.BlockSpec((1, D_MODEL), lambda i: (0, 0)),
            pl.BlockSpec((N_EXPERTS, D_MODEL), lambda i: (0, 0)),
            pl.BlockSpec((N_EXPERTS, 1), lambda i: (0, 0)),
        ],
        out_specs=pl.BlockSpec((ROUTE_TM, ROW_SLABS + 1, LANES), lambda i: (i, 0, 0)),
        out_shape=jax.ShapeDtypeStruct((N_TOKENS, ROW_SLABS + 1, LANES), F32),
        name="norm_route",
    )(x1p, x1s, mod_l, n2g, wrt, br)


def _const_spec(shape):
    nd = len(shape)
    return pl.BlockSpec(shape, lambda b: (0,) * nd, pipeline_mode=pl.Buffered(1))


def _stream_specs(latent):
    n = DEC_SEQ if latent else SEQ
    first = BATCH * SEQ // n if latent else 0
    own = pl.BlockSpec((n, D_MODEL), lambda b: (b, 0))
    rows = pl.BlockSpec((n, ROW_SLABS, LANES), lambda b: (first + b, 0, 0))
    mod = pl.BlockSpec((None, 6, D_MODEL), (lambda b: (b + 1, 0, 0)) if latent else (lambda b: (0, 0, 0)))
    return n, own, rows, mod


def _mixer(x, mod_l, weights, *, latent, prev=None, cache=None, bias=None, layer=0):
    n, own_spec, rows_spec, mod_spec = _stream_specs(latent)
    tok = x.shape[0]
    nb = tok // n
    in_specs = [own_spec, mod_spec]
    args = [x, mod_l]
    if prev is not None:
        in_specs += [rows_spec, mod_spec]
        args += list(prev)
    if latent:
        cache_spec = pl.BlockSpec((None, None, PAST_LEN, D_ATTN), lambda b: (b, layer, 0, 0))
        in_specs += [cache_spec, cache_spec, _const_spec(bias.shape)]
        args += [cache[0], cache[1], bias]
    in_specs += [_const_spec(w.shape) for w in weights]
    args += list(weights)
    out_shape = [jax.ShapeDtypeStruct((tok, D_MODEL), F32)]
    out_specs = [own_spec]
    if not latent:
        out_shape += [jax.ShapeDtypeStruct((nb, n, D_ATTN), F32)] * 2
        out_specs += [pl.BlockSpec((None, n, D_ATTN), lambda b: (b, 0, 0))] * 2
    scratch = [pltpu.VMEM((3 * N_PAIRS, n, LANES), BF16),
               pltpu.VMEM((3, n, D_POOL), F32),
               pltpu.VMEM((ROW_SLABS, n, LANES), BF16)]
    if latent:
        scratch += [pltpu.VMEM((N_PAIRS, PAST_LEN, LANES), BF16),
                    pltpu.VMEM((N_PAIRS, PAST_LEN, LANES), BF16)]
    return pl.pallas_call(
        functools.partial(_mixer_kernel, latent=latent, has_prev=prev is not None, n=n),
        grid=(nb,),
        in_specs=in_specs,
        out_specs=out_specs,
        out_shape=out_shape,
        scratch_shapes=scratch,
        compiler_params=pltpu.CompilerParams(dimension_semantics=("arbitrary",),
                                             vmem_limit_bytes=VMEM_LIMIT),
        name="mixer_latent" if latent else "mixer_context",
    )(*args)


def _bucket_tables():
    lo, hi = [], []
    for g in range(N_GROUPS):
        for a in range(GROUP_SIZE):
            for b in range(a + 1, GROUP_SIZE):
                lo.append(g * GROUP_SIZE + a)
                hi.append(g * GROUP_SIZE + b)
    return np.asarray(lo, np.int32), np.asarray(hi, np.int32)


def _route_plan(e_lo, e_hi):
    a = e_lo % GROUP_SIZE
    b = e_hi % GROUP_SIZE
    bucket = (e_lo // GROUP_SIZE) * PAIRS_PER_GROUP + (a * (2 * GROUP_SIZE - 1 - a)) // 2 + (b - a - 1)
    onehot = (bucket[:, None] == jnp.arange(N_BUCKETS, dtype=jnp.int32)[None, :]).astype(jnp.int32)
    csum = jnp.cumsum(onehot, axis=0)
    count = csum[-1]
    rank = jnp.sum(csum * onehot, axis=1) - 1
    tiles_b = (count + MOE_TM - 1) // MOE_TM
    tile_end = jnp.cumsum(tiles_b)
    tile_start = tile_end - tiles_b
    pos = jnp.sum(onehot * tile_start[None, :], axis=1) * MOE_TM + rank
    row_tok = jnp.zeros((MOE_TILES * MOE_TM,), jnp.int32).at[pos].set(
        jnp.arange(N_TOKENS, dtype=jnp.int32), unique_indices=True)
    n_used = tile_end[-1]
    tile = jnp.arange(MOE_TILES, dtype=jnp.int32)
    tile_b = jnp.sum((tile[:, None] >= tile_end[None, :]).astype(jnp.int32), axis=1)
    last_b = jnp.sum((n_used - 1 >= tile_end).astype(jnp.int32))
    tile_b = jnp.minimum(tile_b, last_b)
    n_valid = jnp.clip(count[tile_b] - (tile - tile_start[tile_b]) * MOE_TM, 0, MOE_TM)
    n_valid = jnp.where(tile < n_used, n_valid, 0)
    lo_tab, hi_tab = _bucket_tables()
    return (jnp.asarray(lo_tab)[tile_b], jnp.asarray(hi_tab)[tile_b], n_valid.astype(jnp.int32),
            n_used.reshape(1).astype(jnp.int32), row_tok)


def _moe_kernel(e0_ref, e1_ref, nv_ref, nu_ref, tok_ref, hg_hbm, wg0_ref, wu0_ref, wd0_ref,
                wg1_ref, wu1_ref, wd1_ref, y_hbm, buf, stg, gsem, ssem):
    i = pl.program_id(0)
    n_used = nu_ref[0]
    slot = i % 2

    def start_gather(tile, s):
        base = tile * MOE_TM

        def body(c, carry):
            for u in range(DMA_UNROLL):
                r = c * DMA_UNROLL + u
                pltpu.make_async_copy(hg_hbm.at[tok_ref[base + r]], buf.at[s, r], gsem.at[s]).start()
            return carry

        lax.fori_loop(0, MOE_TM // DMA_UNROLL, body, 0)

    def wait_gather(s):
        pltpu.make_async_copy(hg_hbm.at[pl.ds(0, MOE_TM)], buf.at[s], gsem.at[s]).wait()

    def wait_scatter(s, rows):
        pltpu.make_async_copy(stg.at[s, pl.ds(0, rows)], y_hbm.at[pl.ds(0, rows)], ssem.at[s]).wait()

    @pl.when(i == 0)
    def _():
        start_gather(0, 0)

    @pl.when(i < n_used)
    def _():
        wait_gather(slot)

        @pl.when(i + 1 < n_used)
        def _():
            start_gather(i + 1, 1 - slot)

        @pl.when(i >= 2)
        def _():
            wait_scatter(slot, nv_ref[jnp.maximum(i - 2, 0)])

        xb = _rows_to_2d(buf, (slot,)).astype(BF16)
        gates = buf[slot, :, ROW_SLABS, :]
        lane = lax.broadcasted_iota(jnp.int32, (MOE_TM, LANES), 1)
        acc = None
        for e_ref, wg_ref, wu_ref, wd_ref in ((e0_ref, wg0_ref, wu0_ref, wd0_ref),
                                              (e1_ref, wg1_ref, wu1_ref, wd1_ref)):
            gcol = jnp.sum(jnp.where(lane == e_ref[i], gates, 0.0), axis=1, keepdims=True)
            a = jnp.dot(xb, wg_ref[...], preferred_element_type=F32)
            b = jnp.dot(xb, wu_ref[...], preferred_element_type=F32)
            hid = (_silu(a) * b * gcol).astype(BF16)
            o = jnp.dot(hid, wd_ref[...], preferred_element_type=F32)
            acc = o if acc is None else acc + o
        for j in range(ROW_SLABS):
            stg[slot, :, j, :] = acc[:, j * LANES:(j + 1) * LANES]

        base = i * MOE_TM

        def scatter_row(r, carry):
            pltpu.make_async_copy(stg.at[slot, r], y_hbm.at[tok_ref[base + r]], ssem.at[slot]).start()
            return carry

        lax.fori_loop(0, nv_ref[i], scatter_row, 0)

        @pl.when(i == n_used - 1)
        def _():
            wait_scatter(slot, nv_ref[i])

            @pl.when(i >= 1)
            def _():
                wait_scatter(1 - slot, nv_ref[jnp.maximum(i - 1, 0)])


def _moe(hg, plan, wg, wu, wd, layer):
    def w_spec(which, shape):
        return pl.BlockSpec((None, None) + shape,
                            lambda i, e0, e1, nv, nu, tok: (layer, (e0, e1)[which][i], 0, 0))

    up_shape, down_shape = (D_MODEL, D_FF), (D_FF, D_MODEL)
    grid_spec = pltpu.PrefetchScalarGridSpec(
        num_scalar_prefetch=5,
        grid=(MOE_TILES,),
        in_specs=[pl.BlockSpec(memory_space=pl.ANY),
                  w_spec(0, up_shape), w_spec(0, up_shape), w_spec(0, down_shape),
                  w_spec(1, up_shape), w_spec(1, up_shape), w_spec(1, down_shape)],
        out_specs=pl.BlockSpec(memory_space=pl.ANY),
        scratch_shapes=[pltpu.VMEM((2, MOE_TM, ROW_SLABS + 1, LANES), F32),
                        pltpu.VMEM((2, MOE_TM, ROW_SLABS, LANES), F32),
                        pltpu.SemaphoreType.DMA((2,)),
                        pltpu.SemaphoreType.DMA((2,))],
    )
    return pl.pallas_call(
        _moe_kernel,
        grid_spec=grid_spec,
        out_shape=jax.ShapeDtypeStruct((N_TOKENS, ROW_SLABS, LANES), F32),
        compiler_params=pltpu.CompilerParams(dimension_semantics=("arbitrary",),
                                             vmem_limit_bytes=VMEM_LIMIT),
        name="moe_routed",
    )(*plan, hg, wg, wu, wd, wg, wu, wd)


def _final_kernel(x1_ref, y_ref, mod_ref, fg_ref, o_ref):
    x2 = x1_ref[...] + mod_ref[5:6, :] * _rows_to_2d(y_ref)
    o_ref[...] = _rms(x2, fg_ref[...])


def _final(x1, y, mod_l, fg, *, latent):
    n, own_spec, rows_spec, mod_spec = _stream_specs(latent)
    return pl.pallas_call(
        _final_kernel,
        grid=(x1.shape[0] // n,),
        in_specs=[own_spec, rows_spec, mod_spec, pl.BlockSpec((1, D_MODEL), lambda b: (0, 0))],
        out_specs=own_spec,
        out_shape=jax.ShapeDtypeStruct(x1.shape, F32),
        name="final_norm",
    )(x1, y, mod_l, fg)


def kernel(x_prompt, x_sample, cache_k, cache_v, c, c_ctx, w_ada, b_ada, norm1_g, norm2_g, w_in, w_out, rpb, pool_w, pool_scale, conv_w, conv_b, conv_ln_g, conv_ln_b, conv_pw_w, conv_pw_b, w_router, b_router, moe_w_gate, moe_w_up, moe_w_down, final_g):
    xp = x_prompt.reshape(BATCH * SEQ, D_MODEL)
    xs = x_sample.reshape(DEC_BATCH * DEC_SEQ, D_MODEL)
    cond = jnp.concatenate([c_ctx[None, :], c, jnp.zeros((N_MOD_ROWS - 1 - DEC_BATCH, D_MODEL), F32)], axis=0)
    mod = _ada_modulation(cond, w_ada, b_ada).reshape(DEPTH, N_MOD_ROWS, 6, D_MODEL)
    bias = _rpb_tiles(rpb)
    ck = cache_k.reshape(DEC_BATCH, DEPTH, PAST_LEN, D_ATTN)
    cv = cache_v.reshape(DEC_BATCH, DEPTH, PAST_LEN, D_ATTN)
    wrt = w_router.T
    br = b_router.reshape(N_EXPERTS, 1)
    fg = final_g.reshape(1, D_MODEL)
    wg, wu, wd = moe_w_gate.astype(BF16), moe_w_up.astype(BF16), moe_w_down.astype(BF16)
    new_k, new_v = [], []
    prev = None
    for l in range(DEPTH):
        pool_bd = jnp.zeros((D_POOL, D_POOL), F32)
        gd = D_POOL // 4
        for g in range(4):
            pool_bd = pool_bd.at[g * gd:(g + 1) * gd, g * gd:(g + 1) * gd].set(pool_w[l, g])
        weights = [norm1_g[l].reshape(1, -1),
                   w_in[l].astype(BF16), w_out[l].astype(BF16),
                   pool_bd.astype(BF16), pool_scale[l].reshape(1, -1),
                   conv_w[l], conv_b[l].reshape(1, -1),
                   conv_ln_g[l].reshape(1, -1), conv_ln_b[l].reshape(1, -1),
                   conv_pw_w[l].astype(BF16), conv_pw_b[l].reshape(1, -1)]

        xp, k_l, v_l = _mixer(xp, mod[l], weights, latent=False, prev=prev)
        xs, = _mixer(xs, mod[l], weights, latent=True, prev=prev, cache=(ck, cv), bias=bias[l], layer=l)
        new_k.append(k_l)
        new_v.append(v_l)

        hg = _route(xp, xs, mod[l], norm2_g[l].reshape(1, -1), wrt, br)
        ids = hg[:, ROW_SLABS, N_EXPERTS:N_EXPERTS + 2].astype(jnp.int32)
        y = _moe(hg, _route_plan(ids[:, 0], ids[:, 1]), wg, wu, wd, l)
        prev = (y, mod[l])

    y_prompt = _final(xp, y, mod[DEPTH - 1], fg, latent=False).reshape(BATCH, SEQ, D_MODEL)
    y_sample = _final(xs, y, mod[DEPTH - 1], fg, latent=True).reshape(DEC_BATCH, DEC_SEQ, D_MODEL)
    new_cache_k = jnp.stack(new_k, axis=1).reshape(BATCH, DEPTH, SEQ, N_HEADS, HEAD_DIM)
    new_cache_v = jnp.stack(new_v, axis=1).reshape(BATCH, DEPTH, SEQ, N_HEADS, HEAD_DIM)
    return (y_prompt, y_sample, new_cache_k, new_cache_v)
```

```python
import functools

import numpy as np
import jax
import jax.numpy as jnp
from jax import lax
from jax.experimental import pallas as pl
from jax.experimental.pallas import tpu as pltpu

D_MODEL = 1024
BATCH = 32
SEQ = 256
DEPTH = 2
DEC_BATCH = 8
DEC_SEQ = 1024
PAST_LEN = 512
GRID_W = 64
GRID_ROWS = DEC_SEQ // GRID_W
D_ATTN = 512
HEAD_DIM = 64
N_HEADS = 8
N_PAIRS = N_HEADS // 2
WIN_ROWS = 8
WIN_COLS = 16
D_POOL = 256
D_CONV = 256
CONV_WIDTH = 31
D_IN = 3 * D_ATTN + D_POOL + 2 * D_CONV
N_EXPERTS = 16
N_GROUPS = 4
GROUP_SIZE = N_EXPERTS // N_GROUPS
D_FF = 512
EPS = 1e-6
N_MOD_ROWS = 16
N_DR = 2 * WIN_ROWS - 1
N_DC = 2 * WIN_COLS - 1
N_BIAS_TILES = N_DR - 1
SEQ_PAD = 16
LANES = 128
ROW_SLABS = D_MODEL // LANES
N_TOKENS = BATCH * SEQ + DEC_BATCH * DEC_SEQ
PAIRS_PER_GROUP = GROUP_SIZE * (GROUP_SIZE - 1) // 2
N_BUCKETS = N_GROUPS * PAIRS_PER_GROUP
BUCKET_ROWS = 32
META_ROWS = 8
ROUTE_TM = 512
CTX_ROUTE_BLOCKS = BATCH * SEQ // ROUTE_TM
MOE_TM = 256
MOE_TILES = N_TOKENS // MOE_TM + N_BUCKETS
DMA_UNROLL = 8
VMEM_LIMIT = 60 * 1024 * 1024

F32 = jnp.float32
BF16 = jnp.bfloat16
HIGHEST = lax.Precision.HIGHEST
NT_DIMS = (((1,), (1,)), ((), ()))


def _rms(x, g):
    return x * lax.rsqrt(jnp.mean(x * x, axis=-1, keepdims=True) + EPS) * g


def _silu(x):
    return x * jax.nn.sigmoid(x)


def _ada_kernel(cond_ref, w_ref, b_ref, o_ref):
    o_ref[...] = jnp.dot(_silu(cond_ref[...]), w_ref[...], preferred_element_type=F32,
                         precision=HIGHEST) + b_ref[...]


def _ada_modulation(cond, w_ada, b_ada):
    n_col = 6 * D_MODEL // D_MODEL
    return pl.pallas_call(
        _ada_kernel,
        grid=(DEPTH, n_col),
        in_specs=[
            pl.BlockSpec((N_MOD_ROWS, D_MODEL), lambda l, n: (0, 0)),
            pl.BlockSpec((None, D_MODEL, D_MODEL), lambda l, n: (l, 0, n)),
            pl.BlockSpec((None, 1, D_MODEL), lambda l, n: (l, 0, n)),
        ],
        out_specs=pl.BlockSpec((None, N_MOD_ROWS, D_MODEL), lambda l, n: (l, 0, n)),
        out_shape=jax.ShapeDtypeStruct((DEPTH, N_MOD_ROWS, 6 * D_MODEL), F32),
        name="ada_modulation",
    )(cond, w_ada, b_ada.reshape(DEPTH, 1, 6 * D_MODEL))


def _rpb_kernel(rpb_ref, o_ref):
    base = (pl.program_id(0) * N_HEADS + pl.program_id(1)) * (N_DR * N_DC)
    cq = lax.broadcasted_iota(jnp.int32, (GRID_W, LANES), 0)
    lane = lax.broadcasted_iota(jnp.int32, (GRID_W, LANES), 1)
    ck = lane & (GRID_W - 1)
    right = lane >= GRID_W
    dcm = jnp.clip(ck - cq, -(WIN_COLS - 1), WIN_COLS - 1) + (WIN_COLS - 1)
    cs = jnp.clip(cq - WIN_COLS // 2, 0, GRID_W - WIN_COLS)
    valid = (ck >= cs) & (ck < cs + WIN_COLS)
    for d in range(N_BIAS_TILES):
        acc = jnp.zeros((GRID_W, LANES), F32)
        for dc in range(N_DC):
            v = jnp.where(right, rpb_ref[base + (d + 1) * N_DC + dc], rpb_ref[base + d * N_DC + dc])
            acc = jnp.where(dcm == dc, v, acc)
        o_ref[d] = jnp.where(valid, acc, -jnp.inf)


def _rpb_tiles(rpb):
    return pl.pallas_call(
        _rpb_kernel,
        grid=(DEPTH, N_HEADS),
        in_specs=[pl.BlockSpec(memory_space=pltpu.SMEM)],
        out_specs=pl.BlockSpec((None, None, N_BIAS_TILES, GRID_W, LANES), lambda l, h: (l, h, 0, 0, 0)),
        out_shape=jax.ShapeDtypeStruct((DEPTH, N_HEADS, N_BIAS_TILES, GRID_W, LANES), F32),
        name="rpb_tiles",
    )(rpb.reshape(-1))


def _shift_rows(a, s):
    return a if s == 0 else pltpu.roll(a, s % a.shape[0], 0)


def _pad_rows(a):
    return jnp.concatenate([a, jnp.zeros((SEQ_PAD, a.shape[1]), a.dtype)], axis=0)


def _softmax_parts(s):
    m = jnp.max(s, axis=-1, keepdims=True)
    e = jnp.exp(s - m)
    return e, jnp.sum(e, axis=-1, keepdims=True)


def _context_attention(qkv_scr, m_scr, n):
    left = lax.broadcasted_iota(jnp.int32, (n, LANES), 1) < HEAD_DIM
    for p in range(N_PAIRS):
        q = qkv_scr[p]
        k = qkv_scr[N_PAIRS + p]
        v = qkv_scr[2 * N_PAIRS + p]
        zero = jnp.zeros_like(q)
        outs = []
        for qh in (jnp.where(left, q, zero), jnp.where(left, zero, q)):
            s = lax.dot_general(qh, k, NT_DIMS, preferred_element_type=F32)
            e, l = _softmax_parts(s)
            outs.append(jnp.dot(e.astype(BF16), v, preferred_element_type=F32) / l)
        m_scr[p] = jnp.where(left, outs[0], outs[1]).astype(BF16)


def _latent_attention(qkv_scr, kc_scr, vc_scr, bias_ref, m_scr):
    left = lax.broadcasted_iota(jnp.int32, (GRID_W, LANES), 1) < HEAD_DIM
    n_loc = WIN_ROWS * GRID_W

    def body(i, carry):
        p = i // GRID_ROWS
        r = i % GRID_ROWS
        rs = jnp.clip(r - WIN_ROWS // 2, 0, GRID_ROWS - WIN_ROWS)
        q = qkv_scr[p, pl.ds(pl.multiple_of(r * GRID_W, GRID_W), GRID_W), :]
        zero = jnp.zeros_like(q)
        qm = jnp.concatenate([jnp.where(left, q, zero), jnp.where(left, zero, q)], axis=0)
        k0 = pl.multiple_of(rs * GRID_W, GRID_W)
        kl = qkv_scr[N_PAIRS + p, pl.ds(k0, n_loc), :]
        vl = qkv_scr[2 * N_PAIRS + p, pl.ds(k0, n_loc), :]
        d0 = rs - r + (WIN_ROWS - 1)
        bias = jnp.concatenate([
            jnp.concatenate([bias_ref[2 * p + hh, d0 + 2 * j] for j in range(WIN_ROWS // 2)], axis=1)
            for hh in range(2)], axis=0)
        s_loc = lax.dot_general(qm, kl, NT_DIMS, preferred_element_type=F32) + bias
        s_ctx = lax.dot_general(qm, kc_scr[p], NT_DIMS, preferred_element_type=F32)
        m = jnp.maximum(jnp.max(s_loc, axis=-1, keepdims=True), jnp.max(s_ctx, axis=-1, keepdims=True))
        e_loc = jnp.exp(s_loc - m)
        e_ctx = jnp.exp(s_ctx - m)
        l = jnp.sum(e_loc, axis=-1, keepdims=True) + jnp.sum(e_ctx, axis=-1, keepdims=True)
        o = (jnp.dot(e_loc.astype(BF16), vl, preferred_element_type=F32)
             + jnp.dot(e_ctx.astype(BF16), vc_scr[p], preferred_element_type=F32)) / l
        out = jnp.where(left, o[:GRID_W], o[GRID_W:])
        m_scr[p, pl.ds(pl.multiple_of(r * GRID_W, GRID_W), GRID_W), :] = out.astype(BF16)
        return carry

    lax.fori_loop(0, N_PAIRS * GRID_ROWS, body, 0)


def _router_bucket(h2, wr_hi_ref, wr_lo_ref, br_ref, n):
    h_hi = h2.astype(BF16)
    h_lo = (h2 - h_hi.astype(F32)).astype(BF16)
    logits = (jnp.dot(h_hi, wr_hi_ref[...], preferred_element_type=F32)
              + (jnp.dot(h_lo, wr_hi_ref[...], preferred_element_type=F32)
                 + jnp.dot(h_hi, wr_lo_ref[...], preferred_element_type=F32)))
    sel = jax.nn.sigmoid(logits.T[:N_EXPERTS]) + br_ref[...]
    rows = [sel[e:e + 1, :] for e in range(N_EXPERTS)]
    gscore = []
    for g in range(N_GROUPS):
        r4 = rows[g * GROUP_SIZE:(g + 1) * GROUP_SIZE]
        best = None
        for i in range(GROUP_SIZE):
            for j in range(i + 1, GROUP_SIZE):
                s = r4[i] + r4[j]
                best = s if best is None else jnp.maximum(best, s)
        gscore.append(best)
    gbest = jnp.zeros((1, n), jnp.int32)
    gval = gscore[0]
    for g in range(1, N_GROUPS):
        upd = gscore[g] > gval
        gbest = jnp.where(upd, g, gbest)
        gval = jnp.where(upd, gscore[g], gval)
    picked = []
    for e in range(N_EXPERTS):
        g = e // GROUP_SIZE
        rank = jnp.zeros((1, n), jnp.int32)
        for j in range(g * GROUP_SIZE, (g + 1) * GROUP_SIZE):
            if j == e:
                continue
            ahead = (rows[j] > rows[e]) | (rows[j] == rows[e]) if j < e else rows[j] > rows[e]
            rank = rank + ahead.astype(jnp.int32)
        picked.append((rank < 2) & (gbest == g))
    e_lo = jnp.zeros((1, n), jnp.int32)
    e_hi = jnp.zeros((1, n), jnp.int32)
    for e in range(N_EXPERTS):
        e_hi = jnp.where(picked[e], e, e_hi)
        e_lo = jnp.where(picked[N_EXPERTS - 1 - e], N_EXPERTS - 1 - e, e_lo)
    a = e_lo - GROUP_SIZE * gbest
    b = e_hi - GROUP_SIZE * gbest
    return gbest * PAIRS_PER_GROUP + jnp.right_shift(a * (2 * GROUP_SIZE - 1 - a), 1) + (b - a - 1)


def _slab(j, n):
    return pl.ds(j, n, stride=ROW_SLABS)


def _rows_to_2d(ref, n):
    return jnp.concatenate([ref[_slab(j, n), :] for j in range(ROW_SLABS)], axis=1)


def _store_slabs(ref, val, n):
    for j in range(ROW_SLABS):
        ref[_slab(j, n), :] = val[:, j * LANES:(j + 1) * LANES]


def _mixer_kernel(*refs, latent, has_prev, n):
    it = iter(refs)
    x_ref, mod_ref = next(it), next(it)
    if has_prev:
        y_ref, modp_ref = next(it), next(it)
    if latent:
        ck_ref, cv_ref, bias_ref = next(it), next(it), next(it)
    (n1g_ref, w_in_ref, w_out_ref, poolw_ref, pscale_ref, convw_ref, convb_ref,
     lng_ref, lnb_ref, pww_ref, pwb_ref) = [next(it) for _ in range(11)]
    x1_ref = next(it)
    if not latent:
        knew_ref, vnew_ref = next(it), next(it)
    qkv_scr, u_scr, m_scr = next(it), next(it), next(it)
    if latent:
        kc_scr, vc_scr = next(it), next(it)

    x = x_ref[...]
    if has_prev:
        x = x + modp_ref[5:6, :] * _rows_to_2d(y_ref, n)
    sh1, sc1, g1 = mod_ref[0:1, :], mod_ref[1:2, :], mod_ref[2:3, :]

    hb = (_rms(x, n1g_ref[...]) * (1.0 + sc1) + sh1).astype(BF16)
    cw = 2 * LANES
    for j in range(D_IN // cw):
        zj = jnp.dot(hb, w_in_ref[:, j * cw:(j + 1) * cw], preferred_element_type=F32)
        if j < 6:
            part = j // 2
            zs = zj * (HEAD_DIM ** -0.5) if part == 0 else zj
            c = part * N_PAIRS + 2 * (j % 2)
            qkv_scr[c] = zs[:, :LANES].astype(BF16)
            qkv_scr[c + 1] = zs[:, LANES:].astype(BF16)
            if not latent and part == 1:
                knew_ref[:, (j % 2) * cw:(j % 2 + 1) * cw] = zj
            if not latent and part == 2:
                vnew_ref[:, (j % 2) * cw:(j % 2 + 1) * cw] = zj
        else:
            u_scr[j - 6] = zj

    if latent:
        for p in range(N_PAIRS):
            kc_scr[p] = ck_ref[:, p * LANES:(p + 1) * LANES].astype(BF16)
            vc_scr[p] = cv_ref[:, p * LANES:(p + 1) * LANES].astype(BF16)
        _latent_attention(qkv_scr, kc_scr, vc_scr, bias_ref, m_scr)
    else:
        _context_attention(qkv_scr, m_scr, n)

    row = lax.broadcasted_iota(jnp.int32, (n, D_POOL), 0)
    grp = lax.broadcasted_iota(jnp.int32, (n, D_POOL), 1) // (D_POOL // 4)
    up = u_scr[0]
    t2 = _pad_rows(up)
    t2 = t2 + _shift_rows(t2, 1)
    t4 = t2 + _shift_rows(t2, 2)
    t8 = t4 + _shift_rows(t4, 4)
    t16 = t8 + _shift_rows(t8, 8)
    s2, s4, s8, s16 = t2[:n], _shift_rows(t4, -1)[:n], _shift_rows(t8, -3)[:n], _shift_rows(t16, -7)[:n]
    wsum = jnp.where(grp == 0, s2, jnp.where(grp == 1, s4, jnp.where(grp == 2, s8, s16)))
    half = jnp.left_shift(1, grp)
    lo = jnp.maximum(row - half, 0)
    hi = jnp.minimum(row + half - 1, n - 1)
    diff = wsum / (hi - lo + 1).astype(F32) - up
    pooled = jnp.dot(diff.astype(BF16), poolw_ref[...], preferred_element_type=F32) * pscale_ref[...]
    m_scr[N_PAIRS] = pooled[:, :LANES].astype(BF16)
    m_scr[N_PAIRS + 1] = pooled[:, LANES:].astype(BF16)

    y = _pad_rows(u_scr[1] * jax.nn.sigmoid(u_scr[2]))
    acc = jnp.zeros(y.shape, F32) + convb_ref[...]
    for k in range(CONV_WIDTH):
        acc = acc + convw_ref[k:k + 1, :] * _shift_rows(y, CONV_WIDTH // 2 - k)
    acc = acc[:n]
    mu = jnp.mean(acc, axis=-1, keepdims=True)
    cen = acc - mu
    var = jnp.mean(cen * cen, axis=-1, keepdims=True)
    yn = _silu(cen * lax.rsqrt(var + EPS) * lng_ref[...] + lnb_ref[...])
    conv = jnp.dot(yn.astype(BF16), pww_ref[...], preferred_element_type=F32) + pwb_ref[...]
    m_scr[N_PAIRS + 2] = conv[:, :LANES].astype(BF16)
    m_scr[N_PAIRS + 3] = conv[:, LANES:].astype(BF16)

    mixed = jnp.concatenate([m_scr[j] for j in range(ROW_SLABS)], axis=1)
    x1_ref[...] = x + g1 * jnp.dot(mixed, w_out_ref[...], preferred_element_type=F32)


def _route_kernel(xp_ref, xs_ref, mod_ref, n2g_ref, wr_hi_ref, wr_lo_ref, br_ref,
                  h_ref, meta_ref, count_ref, run_scr):
    i = pl.program_id(0)
    n = ROUTE_TM

    @pl.when(i == 0)
    def _():
        run_scr[...] = jnp.zeros_like(run_scr)

    x1 = lax.cond(i < CTX_ROUTE_BLOCKS, lambda: xp_ref[...], lambda: xs_ref[...])
    h2 = _rms(x1, n2g_ref[...]) * (1.0 + mod_ref[4:5, :]) + mod_ref[3:4, :]
    _store_slabs(h_ref, h2, n)

    bucket = _router_bucket(h2, wr_hi_ref, wr_lo_ref, br_ref, n)
    onehot = lax.broadcasted_iota(jnp.int32, (BUCKET_ROWS, n), 0) == bucket
    upper = (lax.broadcasted_iota(jnp.int32, (n, n), 0) <= lax.broadcasted_iota(jnp.int32, (n, n), 1))
    incl = jnp.dot(onehot.astype(BF16), upper.astype(BF16), preferred_element_type=F32)
    run = run_scr[...]
    rank = jnp.sum(jnp.where(onehot, incl + run, 0.0), axis=0, keepdims=True) - 1.0
    run = run + jnp.sum(onehot.astype(F32), axis=1, keepdims=True)
    run_scr[...] = run
    meta_ref[...] = jnp.concatenate(
        [bucket, rank.astype(jnp.int32), jnp.zeros((META_ROWS - 2, n), jnp.int32)], axis=0)
    count_ref[...] = jnp.broadcast_to(run, count_ref.shape).astype(jnp.int32)


def _route(x1p, x1s, mod_l, n2g, wr_hi, wr_lo, br):
    per_seq = DEC_SEQ // ROUTE_TM
    last_ctx = CTX_ROUTE_BLOCKS - 1
    return pl.pallas_call(
        _route_kernel,
        grid=(N_TOKENS // ROUTE_TM,),
        in_specs=[
            pl.BlockSpec((ROUTE_TM, D_MODEL), lambda i: (jnp.minimum(i, last_ctx), 0)),
            pl.BlockSpec((ROUTE_TM, D_MODEL), lambda i: (jnp.maximum(i - CTX_ROUTE_BLOCKS, 0), 0)),
            pl.BlockSpec((None, 6, D_MODEL),
                         lambda i: (jnp.where(i < CTX_ROUTE_BLOCKS, 0, (i - CTX_ROUTE_BLOCKS) // per_seq + 1), 0, 0)),
            pl.BlockSpec((1, D_MODEL), lambda i: (0, 0)),
            pl.BlockSpec((D_MODEL, LANES), lambda i: (0, 0)),
            pl.BlockSpec((D_MODEL, LANES), lambda i: (0, 0)),
            pl.BlockSpec((N_EXPERTS, 1), lambda i: (0, 0)),
        ],
        out_specs=[pl.BlockSpec((ROUTE_TM * ROW_SLABS, LANES), lambda i: (i, 0)),
                   pl.BlockSpec((META_ROWS, ROUTE_TM), lambda i: (0, i)),
                   pl.BlockSpec((BUCKET_ROWS, LANES), lambda i: (0, 0))],
        out_shape=[jax.ShapeDtypeStruct((N_TOKENS * ROW_SLABS, LANES), F32),
                   jax.ShapeDtypeStruct((META_ROWS, N_TOKENS), jnp.int32),
                   jax.ShapeDtypeStruct((BUCKET_ROWS, LANES), jnp.int32)],
        scratch_shapes=[pltpu.VMEM((BUCKET_ROWS, 1), F32)],
        compiler_params=pltpu.CompilerParams(dimension_semantics=("arbitrary",)),
        name="norm_route",
    )(x1p, x1s, mod_l, n2g, wr_hi, wr_lo, br)


def _const_spec(shape):
    nd = len(shape)
    return pl.BlockSpec(shape, lambda b: (0,) * nd, pipeline_mode=pl.Buffered(1))


def _stream_specs(latent):
    n = DEC_SEQ if latent else SEQ
    first = BATCH * SEQ // n if latent else 0
    own = pl.BlockSpec((n, D_MODEL), lambda b: (b, 0))
    rows = pl.BlockSpec((n * ROW_SLABS, LANES), lambda b: (first + b, 0))
    mod = pl.BlockSpec((None, 6, D_MODEL), (lambda b: (b + 1, 0, 0)) if latent else (lambda b: (0, 0, 0)))
    return n, own, rows, mod


def _mixer(x, mod_l, weights, *, latent, prev=None, cache=None, bias=None, layer=0):
    n, own_spec, rows_spec, mod_spec = _stream_specs(latent)
    tok = x.shape[0]
    nb = tok // n
    in_specs = [own_spec, mod_spec]
    args = [x, mod_l]
    if prev is not None:
        in_specs += [rows_spec, mod_spec]
        args += list(prev)
    if latent:
        cache_spec = pl.BlockSpec((None, None, PAST_LEN, D_ATTN), lambda b: (b, layer, 0, 0))
        in_specs += [cache_spec, cache_spec, _const_spec(bias.shape)]
        args += [cache[0], cache[1], bias]
    in_specs += [_const_spec(w.shape) for w in weights]
    args += list(weights)
    out_shape = [jax.ShapeDtypeStruct((tok, D_MODEL), F32)]
    out_specs = [own_spec]
    if not latent:
        out_shape += [jax.ShapeDtypeStruct((nb, n, D_ATTN), F32)] * 2
        out_specs += [pl.BlockSpec((None, n, D_ATTN), lambda b: (b, 0, 0))] * 2
    scratch = [pltpu.VMEM((3 * N_PAIRS, n, LANES), BF16),
               pltpu.VMEM((3, n, D_POOL), F32),
               pltpu.VMEM((ROW_SLABS, n, LANES), BF16)]
    if latent:
        scratch += [pltpu.VMEM((N_PAIRS, PAST_LEN, LANES), BF16),
                    pltpu.VMEM((N_PAIRS, PAST_LEN, LANES), BF16)]
    return pl.pallas_call(
        functools.partial(_mixer_kernel, latent=latent, has_prev=prev is not None, n=n),
        grid=(nb,),
        in_specs=in_specs,
        out_specs=out_specs,
        out_shape=out_shape,
        scratch_shapes=scratch,
        compiler_params=pltpu.CompilerParams(dimension_semantics=("arbitrary",),
                                             vmem_limit_bytes=VMEM_LIMIT),
        name="mixer_latent" if latent else "mixer_context",
    )(*args)


def _bucket_tables():
    lo, hi = [], []
    for g in range(N_GROUPS):
        for a in range(GROUP_SIZE):
            for b in range(a + 1, GROUP_SIZE):
                lo.append(g * GROUP_SIZE + a)
                hi.append(g * GROUP_SIZE + b)
    return np.asarray(lo, np.int32), np.asarray(hi, np.int32)


def _route_plan(bucket, rank, count):
    tiles_b = (count + MOE_TM - 1) // MOE_TM
    tile_end = jnp.cumsum(tiles_b)
    tile_start = tile_end - tiles_b
    pos = tile_start[bucket] * MOE_TM + rank
    row_tok = jnp.zeros((MOE_TILES * MOE_TM,), jnp.int32).at[pos].set(
        jnp.arange(N_TOKENS, dtype=jnp.int32), unique_indices=True)
    n_used = tile_end[-1]
    tile = jnp.arange(MOE_TILES, dtype=jnp.int32)
    tile_b = jnp.sum((tile[:, None] >= tile_end[None, :]).astype(jnp.int32), axis=1)
    last_b = jnp.sum((n_used - 1 >= tile_end).astype(jnp.int32))
    tile_b = jnp.minimum(tile_b, last_b)
    n_valid = jnp.clip(count[tile_b] - (tile - tile_start[tile_b]) * MOE_TM, 0, MOE_TM)
    n_valid = jnp.where(tile < n_used, n_valid, 0)
    lo_tab, hi_tab = _bucket_tables()
    return (jnp.asarray(lo_tab)[tile_b], jnp.asarray(hi_tab)[tile_b], n_valid.astype(jnp.int32),
            n_used.reshape(1).astype(jnp.int32), row_tok)


def _row_slabs(ref, row):
    return ref.at[pl.ds(pl.multiple_of(row * ROW_SLABS, ROW_SLABS), ROW_SLABS), :]


def _moe_kernel(e0_ref, e1_ref, nv_ref, nu_ref, tok_ref, h_hbm, wrt_ref, wg0_ref, wu0_ref, wd0_ref,
                wg1_ref, wu1_ref, wd1_ref, y_hbm, buf, xb_scr, stg, gsem, ssem):
    i = pl.program_id(0)
    n_used = nu_ref[0]

    def start_gather(tile):
        base = tile * MOE_TM

        def body(c, carry):
            for u in range(DMA_UNROLL):
                r = c * DMA_UNROLL + u
                pltpu.make_async_copy(_row_slabs(h_hbm, tok_ref[base + r]), _row_slabs(buf, r), gsem).start()
            return carry

        lax.fori_loop(0, MOE_TM // DMA_UNROLL, body, 0)

    def wait_gather():
        pltpu.make_async_copy(h_hbm.at[pl.ds(0, MOE_TM * ROW_SLABS), :], buf, gsem).wait()

    def wait_scatter(rows):
        n = rows * ROW_SLABS
        pltpu.make_async_copy(stg.at[pl.ds(0, n), :], y_hbm.at[pl.ds(0, n), :], ssem).wait()

    @pl.when(i == 0)
    def _():
        start_gather(0)

    @pl.when(i < n_used)
    def _():
        wait_gather()
        w0 = wrt_ref[pl.ds(e0_ref[i], 1), :]
        w1 = wrt_ref[pl.ds(e1_ref[i], 1), :]
        l0 = jnp.zeros((MOE_TM, LANES), F32)
        l1 = jnp.zeros((MOE_TM, LANES), F32)
        for j in range(ROW_SLABS):
            xj = buf[_slab(j, MOE_TM), :]
            xb_scr[:, j * LANES:(j + 1) * LANES] = xj.astype(BF16)
            l0 = l0 + xj * w0[:, j * LANES:(j + 1) * LANES]
            l1 = l1 + xj * w1[:, j * LANES:(j + 1) * LANES]

        @pl.when(i + 1 < n_used)
        def _():
            start_gather(i + 1)

        a0 = jax.nn.sigmoid(jnp.sum(l0, axis=1, keepdims=True))
        a1 = jax.nn.sigmoid(jnp.sum(l1, axis=1, keepdims=True))
        den = a0 + a1
        xb = xb_scr[...]
        acc = None
        for gate, wg_ref, wu_ref, wd_ref in ((a0 / den, wg0_ref, wu0_ref, wd0_ref),
                                             (a1 / den, wg1_ref, wu1_ref, wd1_ref)):
            a = jnp.dot(xb, wg_ref[...], preferred_element_type=F32)
            b = jnp.dot(xb, wu_ref[...], preferred_element_type=F32)
            hid = (_silu(a) * b * gate).astype(BF16)
            o = jnp.dot(hid, wd_ref[...], preferred_element_type=F32)
            acc = o if acc is None else acc + o

        @pl.when(i >= 1)
        def _():
            wait_scatter(nv_ref[jnp.maximum(i - 1, 0)])

        _store_slabs(stg, acc, MOE_TM)
        base = i * MOE_TM

        def scatter_row(r, carry):
            pltpu.make_async_copy(_row_slabs(stg, r), _row_slabs(y_hbm, tok_ref[base + r]), ssem).start()
            return carry

        def scatter_rows(c, carry):
            for u in range(DMA_UNROLL):
                scatter_row(c * DMA_UNROLL + u, carry)
            return carry

        n_full = nv_ref[i] // DMA_UNROLL
        lax.fori_loop(0, n_full, scatter_rows, 0)
        lax.fori_loop(n_full * DMA_UNROLL, nv_ref[i], scatter_row, 0)

        @pl.when(i == n_used - 1)
        def _():
            wait_scatter(nv_ref[i])


def _moe(h, plan, wrt, wg, wu, wd, layer):
    def w_spec(which, shape):
        return pl.BlockSpec((None, None) + shape,
                            lambda i, e0, e1, nv, nu, tok: (layer, (e0, e1)[which][i], 0, 0))

    up_shape, down_shape = (D_MODEL, D_FF), (D_FF, D_MODEL)
    grid_spec = pltpu.PrefetchScalarGridSpec(
        num_scalar_prefetch=5,
        grid=(MOE_TILES,),
        in_specs=[pl.BlockSpec(memory_space=pl.ANY),
                  pl.BlockSpec((N_EXPERTS, D_MODEL), lambda i, *_: (0, 0)),
                  w_spec(0, up_shape), w_spec(0, up_shape), w_spec(0, down_shape),
                  w_spec(1, up_shape), w_spec(1, up_shape), w_spec(1, down_shape)],
        out_specs=pl.BlockSpec(memory_space=pl.ANY),
        scratch_shapes=[pltpu.VMEM((MOE_TM * ROW_SLABS, LANES), F32),
                        pltpu.VMEM((MOE_TM, D_MODEL), BF16),
                        pltpu.VMEM((MOE_TM * ROW_SLABS, LANES), F32),
                        pltpu.SemaphoreType.DMA(()),
                        pltpu.SemaphoreType.DMA(())],
    )
    return pl.pallas_call(
        _moe_kernel,
        grid_spec=grid_spec,
        out_shape=jax.ShapeDtypeStruct((N_TOKENS * ROW_SLABS, LANES), F32),
        compiler_params=pltpu.CompilerParams(dimension_semantics=("arbitrary",),
                                             vmem_limit_bytes=VMEM_LIMIT),
        name="moe_routed",
    )(*plan, h, wrt, wg, wu, wd, wg, wu, wd)


def _final_kernel(x1_ref, y_ref, mod_ref, fg_ref, o_ref):
    x2 = x1_ref[...] + mod_ref[5:6, :] * _rows_to_2d(y_ref, x1_ref.shape[0])
    o_ref[...] = _rms(x2, fg_ref[...])


def _final(x1, y, mod_l, fg, *, latent):
    n, own_spec, rows_spec, mod_spec = _stream_specs(latent)
    return pl.pallas_call(
        _final_kernel,
        grid=(x1.shape[0] // n,),
        in_specs=[own_spec, rows_spec, mod_spec, pl.BlockSpec((1, D_MODEL), lambda b: (0, 0))],
        out_specs=own_spec,
        out_shape=jax.ShapeDtypeStruct(x1.shape, F32),
        name="final_norm",
    )(x1, y, mod_l, fg)


def kernel(x_prompt, x_sample, cache_k, cache_v, c, c_ctx, w_ada, b_ada, norm1_g, norm2_g, w_in, w_out, rpb, pool_w, pool_scale, conv_w, conv_b, conv_ln_g, conv_ln_b, conv_pw_w, conv_pw_b, w_router, b_router, moe_w_gate, moe_w_up, moe_w_down, final_g):
    xp = x_prompt.reshape(BATCH * SEQ, D_MODEL)
    xs = x_sample.reshape(DEC_BATCH * DEC_SEQ, D_MODEL)
    cond = jnp.concatenate([c_ctx[None, :], c, jnp.zeros((N_MOD_ROWS - 1 - DEC_BATCH, D_MODEL), F32)], axis=0)
    mod = _ada_modulation(cond, w_ada, b_ada).reshape(DEPTH, N_MOD_ROWS, 6, D_MODEL)
    bias = _rpb_tiles(rpb)
    ck = cache_k.reshape(DEC_BATCH, DEPTH, PAST_LEN, D_ATTN)
    cv = cache_v.reshape(DEC_BATCH, DEPTH, PAST_LEN, D_ATTN)
    wrt = w_router.T
    wr = jnp.pad(w_router, ((0, 0), (0, LANES - N_EXPERTS)))
    wr_hi = wr.astype(BF16)
    wr_lo = (wr - wr_hi.astype(F32)).astype(BF16)
    br = b_router.reshape(N_EXPERTS, 1)
    fg = final_g.reshape(1, D_MODEL)
    wg, wu, wd = moe_w_gate.astype(BF16), moe_w_up.astype(BF16), moe_w_down.astype(BF16)
    new_k, new_v = [], []
    prev = None
    for l in range(DEPTH):
        pool_bd = jnp.zeros((D_POOL, D_POOL), F32)
        gd = D_POOL // 4
        for g in range(4):
            pool_bd = pool_bd.at[g * gd:(g + 1) * gd, g * gd:(g + 1) * gd].set(pool_w[l, g])
        weights = [norm1_g[l].reshape(1, -1),
                   w_in[l].astype(BF16), w_out[l].astype(BF16),
                   pool_bd.astype(BF16), pool_scale[l].reshape(1, -1),
                   conv_w[l], conv_b[l].reshape(1, -1),
                   conv_ln_g[l].reshape(1, -1), conv_ln_b[l].reshape(1, -1),
                   conv_pw_w[l].astype(BF16), conv_pw_b[l].reshape(1, -1)]

        xp, k_l, v_l = _mixer(xp, mod[l], weights, latent=False, prev=prev)
        xs, = _mixer(xs, mod[l], weights, latent=True, prev=prev, cache=(ck, cv), bias=bias[l], layer=l)
        new_k.append(k_l)
        new_v.append(v_l)

        h, meta, count = _route(xp, xs, mod[l], norm2_g[l].reshape(1, -1), wr_hi, wr_lo, br)
        y = _moe(h, _route_plan(meta[0], meta[1], count[:N_BUCKETS, 0]), wrt, wg, wu, wd, l)
        prev = (y, mod[l])

    y_prompt = _final(xp, y, mod[DEPTH - 1], fg, latent=False).reshape(BATCH, SEQ, D_MODEL)
    y_sample = _final(xs, y, mod[DEPTH - 1], fg, latent=True).reshape(DEC_BATCH, DEC_SEQ, D_MODEL)
    new_cache_k = jnp.stack(new_k, axis=1).reshape(BATCH, DEPTH, SEQ, N_HEADS, HEAD_DIM)
    new_cache_v = jnp.stack(new_v, axis=1).reshape(BATCH, DEPTH, SEQ, N_HEADS, HEAD_DIM)
    return (y_prompt, y_sample, new_cache_k, new_cache_v)
```

```python
import functools

import numpy as np
import jax
import jax.numpy as jnp
from jax import lax
from jax.experimental import pallas as pl
from jax.experimental.pallas import tpu as pltpu

D_MODEL = 1024
BATCH = 32
SEQ = 256
DEPTH = 2
DEC_BATCH = 8
DEC_SEQ = 1024
PAST_LEN = 512
GRID_W = 64
GRID_ROWS = DEC_SEQ // GRID_W
D_ATTN = 512
HEAD_DIM = 64
N_HEADS = 8
N_PAIRS = N_HEADS // 2
WIN_ROWS = 8
WIN_COLS = 16
D_POOL = 256
D_CONV = 256
CONV_WIDTH = 31
D_IN = 3 * D_ATTN + D_POOL + 2 * D_CONV
N_EXPERTS = 16
N_GROUPS = 4
GROUP_SIZE = N_EXPERTS // N_GROUPS
D_FF = 512
EPS = 1e-6
N_MOD_ROWS = 16
N_DR = 2 * WIN_ROWS - 1
N_DC = 2 * WIN_COLS - 1
N_BIAS_TILES = N_DR - 1
SEQ_PAD = 16
LANES = 128
ROW_SLABS = D_MODEL // LANES
N_TOKENS = BATCH * SEQ + DEC_BATCH * DEC_SEQ
PAIRS_PER_GROUP = GROUP_SIZE * (GROUP_SIZE - 1) // 2
N_BUCKETS = N_GROUPS * PAIRS_PER_GROUP
BUCKET_ROWS = 32
META_ROWS = 8
ROUTE_TM = 512
CTX_ROUTE_BLOCKS = BATCH * SEQ // ROUTE_TM
MOE_TM = 256
MOE_TILES = N_TOKENS // MOE_TM + N_BUCKETS
DMA_UNROLL = 8
VMEM_LIMIT = 60 * 1024 * 1024

F32 = jnp.float32
BF16 = jnp.bfloat16
HIGHEST = lax.Precision.HIGHEST
NT_DIMS = (((1,), (1,)), ((), ()))


def _rms(x, g):
    return x * lax.rsqrt(jnp.mean(x * x, axis=-1, keepdims=True) + EPS) * g


def _silu(x):
    return x * jax.nn.sigmoid(x)


def _ada_kernel(cond_ref, w_ref, b_ref, o_ref):
    o_ref[...] = jnp.dot(_silu(cond_ref[...]), w_ref[...], preferred_element_type=F32,
                         precision=HIGHEST) + b_ref[...]


def _ada_modulation(cond, w_ada, b_ada):
    n_col = 6 * D_MODEL // D_MODEL
    return pl.pallas_call(
        _ada_kernel,
        grid=(DEPTH, n_col),
        in_specs=[
            pl.BlockSpec((N_MOD_ROWS, D_MODEL), lambda l, n: (0, 0)),
            pl.BlockSpec((None, D_MODEL, D_MODEL), lambda l, n: (l, 0, n)),
            pl.BlockSpec((None, 1, D_MODEL), lambda l, n: (l, 0, n)),
        ],
        out_specs=pl.BlockSpec((None, N_MOD_ROWS, D_MODEL), lambda l, n: (l, 0, n)),
        out_shape=jax.ShapeDtypeStruct((DEPTH, N_MOD_ROWS, 6 * D_MODEL), F32),
        name="ada_modulation",
    )(cond, w_ada, b_ada.reshape(DEPTH, 1, 6 * D_MODEL))


def _rpb_kernel(rpb_ref, o_ref):
    base = (pl.program_id(0) * N_HEADS + pl.program_id(1)) * (N_DR * N_DC)
    cq = lax.broadcasted_iota(jnp.int32, (GRID_W, LANES), 0)
    lane = lax.broadcasted_iota(jnp.int32, (GRID_W, LANES), 1)
    ck = lane & (GRID_W - 1)
    right = lane >= GRID_W
    dcm = jnp.clip(ck - cq, -(WIN_COLS - 1), WIN_COLS - 1) + (WIN_COLS - 1)
    cs = jnp.clip(cq - WIN_COLS // 2, 0, GRID_W - WIN_COLS)
    valid = (ck >= cs) & (ck < cs + WIN_COLS)
    for d in range(N_BIAS_TILES):
        acc = jnp.zeros((GRID_W, LANES), F32)
        for dc in range(N_DC):
            v = jnp.where(right, rpb_ref[base + (d + 1) * N_DC + dc], rpb_ref[base + d * N_DC + dc])
            acc = jnp.where(dcm == dc, v, acc)
        o_ref[d] = jnp.where(valid, acc, -jnp.inf)


def _rpb_tiles(rpb):
    return pl.pallas_call(
        _rpb_kernel,
        grid=(DEPTH, N_HEADS),
        in_specs=[pl.BlockSpec(memory_space=pltpu.SMEM)],
        out_specs=pl.BlockSpec((None, None, N_BIAS_TILES, GRID_W, LANES), lambda l, h: (l, h, 0, 0, 0)),
        out_shape=jax.ShapeDtypeStruct((DEPTH, N_HEADS, N_BIAS_TILES, GRID_W, LANES), F32),
        name="rpb_tiles",
    )(rpb.reshape(-1))


def _shift_rows(a, s):
    return a if s == 0 else pltpu.roll(a, s % a.shape[0], 0)


def _pad_rows(a):
    return jnp.concatenate([a, jnp.zeros((SEQ_PAD, a.shape[1]), a.dtype)], axis=0)


def _softmax_parts(s):
    m = jnp.max(s, axis=-1, keepdims=True)
    e = jnp.exp(s - m)
    return e, jnp.sum(e, axis=-1, keepdims=True)


def _context_attention(qkv_scr, m_scr, n):
    left = lax.broadcasted_iota(jnp.int32, (n, LANES), 1) < HEAD_DIM
    for p in range(N_PAIRS):
        q = qkv_scr[p]
        k = qkv_scr[N_PAIRS + p]
        v = qkv_scr[2 * N_PAIRS + p]
        zero = jnp.zeros_like(q)
        outs = []
        for qh in (jnp.where(left, q, zero), jnp.where(left, zero, q)):
            s = lax.dot_general(qh, k, NT_DIMS, preferred_element_type=F32)
            e, l = _softmax_parts(s)
            outs.append(jnp.dot(e.astype(BF16), v, preferred_element_type=F32) / l)
        m_scr[p] = jnp.where(left, outs[0], outs[1]).astype(BF16)


def _latent_attention(qkv_scr, kc_scr, vc_scr, bias_ref, m_scr):
    left = lax.broadcasted_iota(jnp.int32, (GRID_W, LANES), 1) < HEAD_DIM
    n_loc = WIN_ROWS * GRID_W

    def body(i, carry):
        p = i // GRID_ROWS
        r = i % GRID_ROWS
        rs = jnp.clip(r - WIN_ROWS // 2, 0, GRID_ROWS - WIN_ROWS)
        q = qkv_scr[p, pl.ds(pl.multiple_of(r * GRID_W, GRID_W), GRID_W), :]
        zero = jnp.zeros_like(q)
        qm = jnp.concatenate([jnp.where(left, q, zero), jnp.where(left, zero, q)], axis=0)
        k0 = pl.multiple_of(rs * GRID_W, GRID_W)
        kl = qkv_scr[N_PAIRS + p, pl.ds(k0, n_loc), :]
        vl = qkv_scr[2 * N_PAIRS + p, pl.ds(k0, n_loc), :]
        d0 = rs - r + (WIN_ROWS - 1)
        bias = jnp.concatenate([
            jnp.concatenate([bias_ref[2 * p + hh, d0 + 2 * j] for j in range(WIN_ROWS // 2)], axis=1)
            for hh in range(2)], axis=0)
        s_loc = lax.dot_general(qm, kl, NT_DIMS, preferred_element_type=F32) + bias
        s_ctx = lax.dot_general(qm, kc_scr[p], NT_DIMS, preferred_element_type=F32)
        m = jnp.maximum(jnp.max(s_loc, axis=-1, keepdims=True), jnp.max(s_ctx, axis=-1, keepdims=True))
        e_loc = jnp.exp(s_loc - m)
        e_ctx = jnp.exp(s_ctx - m)
        l = jnp.sum(e_loc, axis=-1, keepdims=True) + jnp.sum(e_ctx, axis=-1, keepdims=True)
        o = (jnp.dot(e_loc.astype(BF16), vl, preferred_element_type=F32)
             + jnp.dot(e_ctx.astype(BF16), vc_scr[p], preferred_element_type=F32)) / l
        out = jnp.where(left, o[:GRID_W], o[GRID_W:])
        m_scr[p, pl.ds(pl.multiple_of(r * GRID_W, GRID_W), GRID_W), :] = out.astype(BF16)
        return carry

    lax.fori_loop(0, N_PAIRS * GRID_ROWS, body, 0)


def _router_bucket(h2, wr_hi_ref, wr_lo_ref, br_ref, n):
    h_hi = h2.astype(BF16)
    h_lo = (h2 - h_hi.astype(F32)).astype(BF16)
    logits = (jnp.dot(h_hi, wr_hi_ref[...], preferred_element_type=F32)
              + (jnp.dot(h_lo, wr_hi_ref[...], preferred_element_type=F32)
                 + jnp.dot(h_hi, wr_lo_ref[...], preferred_element_type=F32)))
    sel = jax.nn.sigmoid(logits.T[:N_EXPERTS]) + br_ref[...]
    rows = [sel[e:e + 1, :] for e in range(N_EXPERTS)]
    gscore = []
    for g in range(N_GROUPS):
        r4 = rows[g * GROUP_SIZE:(g + 1) * GROUP_SIZE]
        best = None
        for i in range(GROUP_SIZE):
            for j in range(i + 1, GROUP_SIZE):
                s = r4[i] + r4[j]
                best = s if best is None else jnp.maximum(best, s)
        gscore.append(best)
    gbest = jnp.zeros((1, n), jnp.int32)
    gval = gscore[0]
    for g in range(1, N_GROUPS):
        upd = gscore[g] > gval
        gbest = jnp.where(upd, g, gbest)
        gval = jnp.where(upd, gscore[g], gval)
    picked = []
    for e in range(N_EXPERTS):
        g = e // GROUP_SIZE
        rank = jnp.zeros((1, n), jnp.int32)
        for j in range(g * GROUP_SIZE, (g + 1) * GROUP_SIZE):
            if j == e:
                continue
            ahead = (rows[j] > rows[e]) | (rows[j] == rows[e]) if j < e else rows[j] > rows[e]
            rank = rank + ahead.astype(jnp.int32)
        picked.append((rank < 2) & (gbest == g))
    e_lo = jnp.zeros((1, n), jnp.int32)
    e_hi = jnp.zeros((1, n), jnp.int32)
    for e in range(N_EXPERTS):
        e_hi = jnp.where(picked[e], e, e_hi)
        e_lo = jnp.where(picked[N_EXPERTS - 1 - e], N_EXPERTS - 1 - e, e_lo)
    a = e_lo - GROUP_SIZE * gbest
    b = e_hi - GROUP_SIZE * gbest
    return gbest * PAIRS_PER_GROUP + jnp.right_shift(a * (2 * GROUP_SIZE - 1 - a), 1) + (b - a - 1)


def _slab(j, n):
    return pl.ds(j, n, stride=ROW_SLABS)


def _rows_to_2d(ref, n):
    return jnp.concatenate([ref[_slab(j, n), :] for j in range(ROW_SLABS)], axis=1)


def _store_slabs(ref, val, n):
    for j in range(ROW_SLABS):
        ref[_slab(j, n), :] = val[:, j * LANES:(j + 1) * LANES]


def _mixer_kernel(*refs, latent, has_prev, n):
    it = iter(refs)
    x_ref, mod_ref = next(it), next(it)
    if has_prev:
        y_ref, modp_ref = next(it), next(it)
    if latent:
        ck_ref, cv_ref, bias_ref = next(it), next(it), next(it)
    (n1g_ref, w_in_ref, w_out_ref, poolw_ref, pscale_ref, convw_ref, convb_ref,
     lng_ref, lnb_ref, pww_ref, pwb_ref) = [next(it) for _ in range(11)]
    x1_ref = next(it)
    if not latent:
        knew_ref, vnew_ref = next(it), next(it)
    qkv_scr, u_scr, m_scr = next(it), next(it), next(it)
    if latent:
        kc_scr, vc_scr = next(it), next(it)

    x = x_ref[...]
    if has_prev:
        x = x + modp_ref[5:6, :] * _rows_to_2d(y_ref, n)
    sh1, sc1, g1 = mod_ref[0:1, :], mod_ref[1:2, :], mod_ref[2:3, :]

    hb = (_rms(x, n1g_ref[...]) * (1.0 + sc1) + sh1).astype(BF16)
    cw = 2 * LANES
    for j in range(D_IN // cw):
        zj = jnp.dot(hb, w_in_ref[:, j * cw:(j + 1) * cw], preferred_element_type=F32)
        if j < 6:
            part = j // 2
            zs = zj * (HEAD_DIM ** -0.5) if part == 0 else zj
            c = part * N_PAIRS + 2 * (j % 2)
            qkv_scr[c] = zs[:, :LANES].astype(BF16)
            qkv_scr[c + 1] = zs[:, LANES:].astype(BF16)
            if not latent and part == 1:
                knew_ref[:, (j % 2) * cw:(j % 2 + 1) * cw] = zj
            if not latent and part == 2:
                vnew_ref[:, (j % 2) * cw:(j % 2 + 1) * cw] = zj
        else:
            u_scr[j - 6] = zj

    if latent:
        for p in range(N_PAIRS):
            kc_scr[p] = ck_ref[:, p * LANES:(p + 1) * LANES].astype(BF16)
            vc_scr[p] = cv_ref[:, p * LANES:(p + 1) * LANES].astype(BF16)
        _latent_attention(qkv_scr, kc_scr, vc_scr, bias_ref, m_scr)
    else:
        _context_attention(qkv_scr, m_scr, n)

    row = lax.broadcasted_iota(jnp.int32, (n, D_POOL), 0)
    grp = lax.broadcasted_iota(jnp.int32, (n, D_POOL), 1) // (D_POOL // 4)
    up = u_scr[0]
    t2 = _pad_rows(up)
    t2 = t2 + _shift_rows(t2, 1)
    t4 = t2 + _shift_rows(t2, 2)
    t8 = t4 + _shift_rows(t4, 4)
    t16 = t8 + _shift_rows(t8, 8)
    s2, s4, s8, s16 = t2[:n], _shift_rows(t4, -1)[:n], _shift_rows(t8, -3)[:n], _shift_rows(t16, -7)[:n]
    wsum = jnp.where(grp == 0, s2, jnp.where(grp == 1, s4, jnp.where(grp == 2, s8, s16)))
    half = jnp.left_shift(1, grp)
    lo = jnp.maximum(row - half, 0)
    hi = jnp.minimum(row + half - 1, n - 1)
    diff = wsum / (hi - lo + 1).astype(F32) - up
    pooled = jnp.dot(diff.astype(BF16), poolw_ref[...], preferred_element_type=F32) * pscale_ref[...]
    m_scr[N_PAIRS] = pooled[:, :LANES].astype(BF16)
    m_scr[N_PAIRS + 1] = pooled[:, LANES:].astype(BF16)

    y = _pad_rows(u_scr[1] * jax.nn.sigmoid(u_scr[2]))
    acc = jnp.zeros(y.shape, F32) + convb_ref[...]
    for k in range(CONV_WIDTH):
        acc = acc + convw_ref[k:k + 1, :] * _shift_rows(y, CONV_WIDTH // 2 - k)
    acc = acc[:n]
    mu = jnp.mean(acc, axis=-1, keepdims=True)
    cen = acc - mu
    var = jnp.mean(cen * cen, axis=-1, keepdims=True)
    yn = _silu(cen * lax.rsqrt(var + EPS) * lng_ref[...] + lnb_ref[...])
    conv = jnp.dot(yn.astype(BF16), pww_ref[...], preferred_element_type=F32) + pwb_ref[...]
    m_scr[N_PAIRS + 2] = conv[:, :LANES].astype(BF16)
    m_scr[N_PAIRS + 3] = conv[:, LANES:].astype(BF16)

    mixed = jnp.concatenate([m_scr[j] for j in range(ROW_SLABS)], axis=1)
    x1_ref[...] = x + g1 * jnp.dot(mixed, w_out_ref[...], preferred_element_type=F32)


def _route_kernel(xp_ref, xs_ref, mod_ref, n2g_ref, wr_hi_ref, wr_lo_ref, br_ref,
                  h_ref, meta_ref, count_ref, run_scr):
    i = pl.program_id(0)
    n = ROUTE_TM

    @pl.when(i == 0)
    def _():
        run_scr[...] = jnp.zeros_like(run_scr)

    x1 = lax.cond(i < CTX_ROUTE_BLOCKS, lambda: xp_ref[...], lambda: xs_ref[...])
    h2 = _rms(x1, n2g_ref[...]) * (1.0 + mod_ref[4:5, :]) + mod_ref[3:4, :]
    _store_slabs(h_ref, h2, n)

    bucket = _router_bucket(h2, wr_hi_ref, wr_lo_ref, br_ref, n)
    onehot = lax.broadcasted_iota(jnp.int32, (BUCKET_ROWS, n), 0) == bucket
    upper = (lax.broadcasted_iota(jnp.int32, (n, n), 0) <= lax.broadcasted_iota(jnp.int32, (n, n), 1))
    incl = jnp.dot(onehot.astype(BF16), upper.astype(BF16), preferred_element_type=F32)
    run = run_scr[...]
    rank = jnp.sum(jnp.where(onehot, incl + run, 0.0), axis=0, keepdims=True) - 1.0
    run = run + jnp.sum(onehot.astype(F32), axis=1, keepdims=True)
    run_scr[...] = run
    meta_ref[...] = jnp.concatenate(
        [bucket, rank.astype(jnp.int32), jnp.zeros((META_ROWS - 2, n), jnp.int32)], axis=0)
    count_ref[...] = jnp.broadcast_to(run, count_ref.shape).astype(jnp.int32)


def _route(x1p, x1s, mod_l, n2g, wr_hi, wr_lo, br):
    per_seq = DEC_SEQ // ROUTE_TM
    last_ctx = CTX_ROUTE_BLOCKS - 1
    return pl.pallas_call(
        _route_kernel,
        grid=(N_TOKENS // ROUTE_TM,),
        in_specs=[
            pl.BlockSpec((ROUTE_TM, D_MODEL), lambda i: (jnp.minimum(i, last_ctx), 0)),
            pl.BlockSpec((ROUTE_TM, D_MODEL), lambda i: (jnp.maximum(i - CTX_ROUTE_BLOCKS, 0), 0)),
            pl.BlockSpec((None, 6, D_MODEL),
                         lambda i: (jnp.where(i < CTX_ROUTE_BLOCKS, 0, (i - CTX_ROUTE_BLOCKS) // per_seq + 1), 0, 0)),
            pl.BlockSpec((1, D_MODEL), lambda i: (0, 0)),
            pl.BlockSpec((D_MODEL, LANES), lambda i: (0, 0)),
            pl.BlockSpec((D_MODEL, LANES), lambda i: (0, 0)),
            pl.BlockSpec((N_EXPERTS, 1), lambda i: (0, 0)),
        ],
        out_specs=[pl.BlockSpec((ROUTE_TM * ROW_SLABS, LANES), lambda i: (i, 0)),
                   pl.BlockSpec((META_ROWS, ROUTE_TM), lambda i: (0, i)),
                   pl.BlockSpec((BUCKET_ROWS, LANES), lambda i: (0, 0))],
        out_shape=[jax.ShapeDtypeStruct((N_TOKENS * ROW_SLABS, LANES), F32),
                   jax.ShapeDtypeStruct((META_ROWS, N_TOKENS), jnp.int32),
                   jax.ShapeDtypeStruct((BUCKET_ROWS, LANES), jnp.int32)],
        scratch_shapes=[pltpu.VMEM((BUCKET_ROWS, 1), F32)],
        compiler_params=pltpu.CompilerParams(dimension_semantics=("arbitrary",)),
        name="norm_route",
    )(x1p, x1s, mod_l, n2g, wr_hi, wr_lo, br)


def _const_spec(shape):
    nd = len(shape)
    return pl.BlockSpec(shape, lambda b: (0,) * nd, pipeline_mode=pl.Buffered(1))


def _stream_specs(latent):
    n = DEC_SEQ if latent else SEQ
    first = BATCH * SEQ // n if latent else 0
    own = pl.BlockSpec((n, D_MODEL), lambda b: (b, 0))
    rows = pl.BlockSpec((n * ROW_SLABS, LANES), lambda b: (first + b, 0))
    mod = pl.BlockSpec((None, 6, D_MODEL), (lambda b: (b + 1, 0, 0)) if latent else (lambda b: (0, 0, 0)))
    return n, own, rows, mod


def _mixer(x, mod_l, weights, *, latent, prev=None, cache=None, bias=None, layer=0):
    n, own_spec, rows_spec, mod_spec = _stream_specs(latent)
    tok = x.shape[0]
    nb = tok // n
    in_specs = [own_spec, mod_spec]
    args = [x, mod_l]
    if prev is not None:
        in_specs += [rows_spec, mod_spec]
        args += list(prev)
    if latent:
        cache_spec = pl.BlockSpec((None, None, PAST_LEN, D_ATTN), lambda b: (b, layer, 0, 0))
        in_specs += [cache_spec, cache_spec, _const_spec(bias.shape)]
        args += [cache[0], cache[1], bias]
    in_specs += [_const_spec(w.shape) for w in weights]
    args += list(weights)
    out_shape = [jax.ShapeDtypeStruct((tok, D_MODEL), F32)]
    out_specs = [own_spec]
    if not latent:
        out_shape += [jax.ShapeDtypeStruct((nb, n, D_ATTN), F32)] * 2
        out_specs += [pl.BlockSpec((None, n, D_ATTN), lambda b: (b, 0, 0))] * 2
    scratch = [pltpu.VMEM((3 * N_PAIRS, n, LANES), BF16),
               pltpu.VMEM((3, n, D_POOL), F32),
               pltpu.VMEM((ROW_SLABS, n, LANES), BF16)]
    if latent:
        scratch += [pltpu.VMEM((N_PAIRS, PAST_LEN, LANES), BF16),
                    pltpu.VMEM((N_PAIRS, PAST_LEN, LANES), BF16)]
    return pl.pallas_call(
        functools.partial(_mixer_kernel, latent=latent, has_prev=prev is not None, n=n),
        grid=(nb,),
        in_specs=in_specs,
        out_specs=out_specs,
        out_shape=out_shape,
        scratch_shapes=scratch,
        compiler_params=pltpu.CompilerParams(dimension_semantics=("arbitrary",),
                                             vmem_limit_bytes=VMEM_LIMIT),
        name="mixer_latent" if latent else "mixer_context",
    )(*args)


def _bucket_tables():
    lo, hi = [], []
    for g in range(N_GROUPS):
        for a in range(GROUP_SIZE):
            for b in range(a + 1, GROUP_SIZE):
                lo.append(g * GROUP_SIZE + a)
                hi.append(g * GROUP_SIZE + b)
    return np.asarray(lo, np.int32), np.asarray(hi, np.int32)


def _route_plan(bucket, rank, count):
    tiles_b = (count + MOE_TM - 1) // MOE_TM
    tile_end = jnp.cumsum(tiles_b)
    tile_start = tile_end - tiles_b
    pos = tile_start[bucket] * MOE_TM + rank
    row_tok = jnp.zeros(((MOE_TILES + 1) * MOE_TM,), jnp.int32).at[pos].set(
        jnp.arange(N_TOKENS, dtype=jnp.int32), unique_indices=True)
    n_used = tile_end[-1]
    tile = jnp.arange(MOE_TILES, dtype=jnp.int32)
    tile_b = jnp.sum((tile[:, None] >= tile_end[None, :]).astype(jnp.int32), axis=1)
    last_b = jnp.sum((n_used - 1 >= tile_end).astype(jnp.int32))
    tile_b = jnp.minimum(tile_b, last_b)
    n_valid = jnp.clip(count[tile_b] - (tile - tile_start[tile_b]) * MOE_TM, 0, MOE_TM)
    n_valid = jnp.where(tile < n_used, n_valid, 0)
    lo_tab, hi_tab = _bucket_tables()
    return (jnp.asarray(lo_tab)[tile_b], jnp.asarray(hi_tab)[tile_b], n_valid.astype(jnp.int32),
            n_used.reshape(1).astype(jnp.int32), row_tok)


def _row_slabs(ref, row):
    return ref.at[pl.ds(pl.multiple_of(row * ROW_SLABS, ROW_SLABS), ROW_SLABS), :]


def _moe_kernel(e0_ref, e1_ref, nv_ref, nu_ref, tok_ref, h_hbm, wrt_ref, wg0_ref, wu0_ref, wd0_ref,
                wg1_ref, wu1_ref, wd1_ref, y_hbm, buf, xb_scr, stg, gsem, ssem):
    i = pl.program_id(0)
    n_used = nu_ref[0]

    def start_gather(tile):
        base = tile * MOE_TM
        for r in range(MOE_TM):
            pltpu.make_async_copy(_row_slabs(h_hbm, tok_ref[base + r]), _row_slabs(buf, r),
                                  gsem).start(priority=r % 2)

    def wait_gather():
        pltpu.make_async_copy(h_hbm.at[pl.ds(0, MOE_TM * ROW_SLABS), :], buf, gsem).wait()

    def wait_scatter(rows):
        n = rows * ROW_SLABS
        pltpu.make_async_copy(stg.at[pl.ds(0, n), :], y_hbm.at[pl.ds(0, n), :], ssem).wait()

    @pl.when(i == 0)
    def _():
        start_gather(0)

    @pl.when(i < n_used)
    def _():
        wait_gather()
        w0 = wrt_ref[pl.ds(e0_ref[i], 1), :]
        w1 = wrt_ref[pl.ds(e1_ref[i], 1), :]
        l0 = jnp.zeros((MOE_TM, LANES), F32)
        l1 = jnp.zeros((MOE_TM, LANES), F32)
        for j in range(ROW_SLABS):
            xj = buf[_slab(j, MOE_TM), :]
            xb_scr[:, j * LANES:(j + 1) * LANES] = xj.astype(BF16)
            l0 = l0 + xj * w0[:, j * LANES:(j + 1) * LANES]
            l1 = l1 + xj * w1[:, j * LANES:(j + 1) * LANES]

        start_gather(i + 1)

        a0 = jax.nn.sigmoid(jnp.sum(l0, axis=1, keepdims=True))
        a1 = jax.nn.sigmoid(jnp.sum(l1, axis=1, keepdims=True))
        den = a0 + a1
        xb = xb_scr[...]
        acc = None
        for gate, wg_ref, wu_ref, wd_ref in ((a0 / den, wg0_ref, wu0_ref, wd0_ref),
                                             (a1 / den, wg1_ref, wu1_ref, wd1_ref)):
            a = jnp.dot(xb, wg_ref[...], preferred_element_type=F32)
            b = jnp.dot(xb, wu_ref[...], preferred_element_type=F32)
            hid = (_silu(a) * b * gate).astype(BF16)
            o = jnp.dot(hid, wd_ref[...], preferred_element_type=F32)
            acc = o if acc is None else acc + o

        @pl.when(i >= 1)
        def _():
            wait_scatter(nv_ref[jnp.maximum(i - 1, 0)])

        _store_slabs(stg, acc, MOE_TM)
        base = i * MOE_TM

        def scatter_row(r, priority):
            pltpu.make_async_copy(_row_slabs(stg, r), _row_slabs(y_hbm, tok_ref[base + r]),
                                  ssem).start(priority=priority)

        def scatter_rows(c, carry):
            for u in range(DMA_UNROLL):
                scatter_row(c * DMA_UNROLL + u, u % 2)
            return carry

        def scatter_one(r, carry):
            scatter_row(r, 0)
            return carry

        n_full = nv_ref[i] // DMA_UNROLL
        lax.fori_loop(0, n_full, scatter_rows, 0)
        lax.fori_loop(n_full * DMA_UNROLL, nv_ref[i], scatter_one, 0)

        @pl.when(i == n_used - 1)
        def _():
            wait_scatter(nv_ref[i])
            wait_gather()


def _moe(h, plan, wrt, wg, wu, wd, layer):
    def w_spec(which, shape):
        return pl.BlockSpec((None, None) + shape,
                            lambda i, e0, e1, nv, nu, tok: (layer, (e0, e1)[which][i], 0, 0))

    up_shape, down_shape = (D_MODEL, D_FF), (D_FF, D_MODEL)
    grid_spec = pltpu.PrefetchScalarGridSpec(
        num_scalar_prefetch=5,
        grid=(MOE_TILES,),
        in_specs=[pl.BlockSpec(memory_space=pl.ANY),
                  pl.BlockSpec((N_EXPERTS, D_MODEL), lambda i, *_: (0, 0)),
                  w_spec(0, up_shape), w_spec(0, up_shape), w_spec(0, down_shape),
                  w_spec(1, up_shape), w_spec(1, up_shape), w_spec(1, down_shape)],
        out_specs=pl.BlockSpec(memory_space=pl.ANY),
        scratch_shapes=[pltpu.VMEM((MOE_TM * ROW_SLABS, LANES), F32),
                        pltpu.VMEM((MOE_TM, D_MODEL), BF16),
                        pltpu.VMEM((MOE_TM * ROW_SLABS, LANES), F32),
                        pltpu.SemaphoreType.DMA(()),
                        pltpu.SemaphoreType.DMA(())],
    )
    return pl.pallas_call(
        _moe_kernel,
        grid_spec=grid_spec,
        out_shape=jax.ShapeDtypeStruct((N_TOKENS * ROW_SLABS, LANES), F32),
        compiler_params=pltpu.CompilerParams(dimension_semantics=("arbitrary",),
                                             vmem_limit_bytes=VMEM_LIMIT),
        name="moe_routed",
    )(*plan, h, wrt, wg, wu, wd, wg, wu, wd)


def _final_kernel(x1_ref, y_ref, mod_ref, fg_ref, o_ref):
    x2 = x1_ref[...] + mod_ref[5:6, :] * _rows_to_2d(y_ref, x1_ref.shape[0])
    o_ref[...] = _rms(x2, fg_ref[...])


def _final(x1, y, mod_l, fg, *, latent):
    n, own_spec, rows_spec, mod_spec = _stream_specs(latent)
    return pl.pallas_call(
        _final_kernel,
        grid=(x1.shape[0] // n,),
        in_specs=[own_spec, rows_spec, mod_spec, pl.BlockSpec((1, D_MODEL), lambda b: (0, 0))],
        out_specs=own_spec,
        out_shape=jax.ShapeDtypeStruct(x1.shape, F32),
        name="final_norm",
    )(x1, y, mod_l, fg)


def kernel(x_prompt, x_sample, cache_k, cache_v, c, c_ctx, w_ada, b_ada, norm1_g, norm2_g, w_in, w_out, rpb, pool_w, pool_scale, conv_w, conv_b, conv_ln_g, conv_ln_b, conv_pw_w, conv_pw_b, w_router, b_router, moe_w_gate, moe_w_up, moe_w_down, final_g):
    xp = x_prompt.reshape(BATCH * SEQ, D_MODEL)
    xs = x_sample.reshape(DEC_BATCH * DEC_SEQ, D_MODEL)
    cond = jnp.concatenate([c_ctx[None, :], c, jnp.zeros((N_MOD_ROWS - 1 - DEC_BATCH, D_MODEL), F32)], axis=0)
    mod = _ada_modulation(cond, w_ada, b_ada).reshape(DEPTH, N_MOD_ROWS, 6, D_MODEL)
    bias = _rpb_tiles(rpb)
    ck = cache_k.reshape(DEC_BATCH, DEPTH, PAST_LEN, D_ATTN)
    cv = cache_v.reshape(DEC_BATCH, DEPTH, PAST_LEN, D_ATTN)
    wrt = w_router.T
    wr = jnp.pad(w_router, ((0, 0), (0, LANES - N_EXPERTS)))
    wr_hi = wr.astype(BF16)
    wr_lo = (wr - wr_hi.astype(F32)).astype(BF16)
    br = b_router.reshape(N_EXPERTS, 1)
    fg = final_g.reshape(1, D_MODEL)
    wg, wu, wd = moe_w_gate.astype(BF16), moe_w_up.astype(BF16), moe_w_down.astype(BF16)
    new_k, new_v = [], []
    prev = None
    for l in range(DEPTH):
        pool_bd = jnp.zeros((D_POOL, D_POOL), F32)
        gd = D_POOL // 4
        for g in range(4):
            pool_bd = pool_bd.at[g * gd:(g + 1) * gd, g * gd:(g + 1) * gd].set(pool_w[l, g])
        weights = [norm1_g[l].reshape(1, -1),
                   w_in[l].astype(BF16), w_out[l].astype(BF16),
                   pool_bd.astype(BF16), pool_scale[l].reshape(1, -1),
                   conv_w[l], conv_b[l].reshape(1, -1),
                   conv_ln_g[l].reshape(1, -1), conv_ln_b[l].reshape(1, -1),
                   conv_pw_w[l].astype(BF16), conv_pw_b[l].reshape(1, -1)]

        xp, k_l, v_l = _mixer(xp, mod[l], weights, latent=False, prev=prev)
        xs, = _mixer(xs, mod[l], weights, latent=True, prev=prev, cache=(ck, cv), bias=bias[l], layer=l)
        new_k.append(k_l)
        new_v.append(v_l)

        h, meta, count = _route(xp, xs, mod[l], norm2_g[l].reshape(1, -1), wr_hi, wr_lo, br)
        y = _moe(h, _route_plan(meta[0], meta[1], count[:N_BUCKETS, 0]), wrt, wg, wu, wd, l)
        prev = (y, mod[l])

    y_prompt = _final(xp, y, mod[DEPTH - 1], fg, latent=False).reshape(BATCH, SEQ, D_MODEL)
    y_sample = _final(xs, y, mod[DEPTH - 1], fg, latent=True).reshape(DEC_BATCH, DEC_SEQ, D_MODEL)
    new_cache_k = jnp.stack(new_k, axis=1).reshape(BATCH, DEPTH, SEQ, N_HEADS, HEAD_DIM)
    new_cache_v = jnp.stack(new_v, axis=1).reshape(BATCH, DEPTH, SEQ, N_HEADS, HEAD_DIM)
    return (y_prompt, y_sample, new_cache_k, new_cache_v)
```

```python
import functools

import numpy as np
import jax
import jax.numpy as jnp
from jax import lax
from jax.experimental import pallas as pl
from jax.experimental.pallas import tpu as pltpu

D_MODEL = 1024
BATCH = 32
SEQ = 256
DEPTH = 2
DEC_BATCH = 8
DEC_SEQ = 1024
PAST_LEN = 512
GRID_W = 64
GRID_ROWS = DEC_SEQ // GRID_W
D_ATTN = 512
HEAD_DIM = 64
N_HEADS = 8
N_PAIRS = N_HEADS // 2
WIN_ROWS = 8
WIN_COLS = 16
D_POOL = 256
D_CONV = 256
CONV_WIDTH = 31
D_IN = 3 * D_ATTN + D_POOL + 2 * D_CONV
N_EXPERTS = 16
N_GROUPS = 4
GROUP_SIZE = N_EXPERTS // N_GROUPS
D_FF = 512
EPS = 1e-6
N_MOD_ROWS = 16
N_DR = 2 * WIN_ROWS - 1
N_DC = 2 * WIN_COLS - 1
N_BIAS_TILES = N_DR - 1
SEQ_PAD = 16
LANES = 128
ROW_SLABS = D_MODEL // LANES
N_TOKENS = BATCH * SEQ + DEC_BATCH * DEC_SEQ
PAIRS_PER_GROUP = GROUP_SIZE * (GROUP_SIZE - 1) // 2
N_BUCKETS = N_GROUPS * PAIRS_PER_GROUP
BUCKET_ROWS = 32
META_ROWS = 8
ROUTE_TM = 512
CTX_ROUTE_BLOCKS = BATCH * SEQ // ROUTE_TM
MOE_TM = 256
MOE_TILES = N_TOKENS // MOE_TM + N_BUCKETS
MOE_PHASES = 6
VMEM_LIMIT = 60 * 1024 * 1024

F32 = jnp.float32
BF16 = jnp.bfloat16
HIGHEST = lax.Precision.HIGHEST
NT_DIMS = (((1,), (1,)), ((), ()))


def _rms(x, g):
    return x * lax.rsqrt(jnp.mean(x * x, axis=-1, keepdims=True) + EPS) * g


def _silu(x):
    return x * jax.nn.sigmoid(x)


def _ada_kernel(cond_ref, w_ref, b_ref, o_ref):
    o_ref[...] = jnp.dot(_silu(cond_ref[...]), w_ref[...], preferred_element_type=F32,
                         precision=HIGHEST) + b_ref[...]


def _ada_modulation(cond, w_ada, b_ada):
    n_col = 6 * D_MODEL // D_MODEL
    return pl.pallas_call(
        _ada_kernel,
        grid=(DEPTH, n_col),
        in_specs=[
            pl.BlockSpec((N_MOD_ROWS, D_MODEL), lambda l, n: (0, 0)),
            pl.BlockSpec((None, D_MODEL, D_MODEL), lambda l, n: (l, 0, n)),
            pl.BlockSpec((None, 1, D_MODEL), lambda l, n: (l, 0, n)),
        ],
        out_specs=pl.BlockSpec((None, N_MOD_ROWS, D_MODEL), lambda l, n: (l, 0, n)),
        out_shape=jax.ShapeDtypeStruct((DEPTH, N_MOD_ROWS, 6 * D_MODEL), F32),
        name="ada_modulation",
    )(cond, w_ada, b_ada.reshape(DEPTH, 1, 6 * D_MODEL))


def _rpb_kernel(rpb_ref, o_ref):
    base = (pl.program_id(0) * N_HEADS + pl.program_id(1)) * (N_DR * N_DC)
    cq = lax.broadcasted_iota(jnp.int32, (GRID_W, LANES), 0)
    lane = lax.broadcasted_iota(jnp.int32, (GRID_W, LANES), 1)
    ck = lane & (GRID_W - 1)
    right = lane >= GRID_W
    dcm = jnp.clip(ck - cq, -(WIN_COLS - 1), WIN_COLS - 1) + (WIN_COLS - 1)
    cs = jnp.clip(cq - WIN_COLS // 2, 0, GRID_W - WIN_COLS)
    valid = (ck >= cs) & (ck < cs + WIN_COLS)
    for d in range(N_BIAS_TILES):
        acc = jnp.zeros((GRID_W, LANES), F32)
        for dc in range(N_DC):
            v = jnp.where(right, rpb_ref[base + (d + 1) * N_DC + dc], rpb_ref[base + d * N_DC + dc])
            acc = jnp.where(dcm == dc, v, acc)
        o_ref[d] = jnp.where(valid, acc, -jnp.inf)


def _rpb_tiles(rpb):
    return pl.pallas_call(
        _rpb_kernel,
        grid=(DEPTH, N_HEADS),
        in_specs=[pl.BlockSpec(memory_space=pltpu.SMEM)],
        out_specs=pl.BlockSpec((None, None, N_BIAS_TILES, GRID_W, LANES), lambda l, h: (l, h, 0, 0, 0)),
        out_shape=jax.ShapeDtypeStruct((DEPTH, N_HEADS, N_BIAS_TILES, GRID_W, LANES), F32),
        name="rpb_tiles",
    )(rpb.reshape(-1))


def _shift_rows(a, s):
    return a if s == 0 else pltpu.roll(a, s % a.shape[0], 0)


def _pad_rows(a):
    return jnp.concatenate([a, jnp.zeros((SEQ_PAD, a.shape[1]), a.dtype)], axis=0)


def _softmax_parts(s):
    m = jnp.max(s, axis=-1, keepdims=True)
    e = jnp.exp(s - m)
    return e, jnp.sum(e, axis=-1, keepdims=True)


def _context_attention(qkv_scr, m_scr, n):
    left = lax.broadcasted_iota(jnp.int32, (n, LANES), 1) < HEAD_DIM
    for p in range(N_PAIRS):
        q = qkv_scr[p]
        k = qkv_scr[N_PAIRS + p]
        v = qkv_scr[2 * N_PAIRS + p]
        zero = jnp.zeros_like(q)
        outs = []
        for qh in (jnp.where(left, q, zero), jnp.where(left, zero, q)):
            s = lax.dot_general(qh, k, NT_DIMS, preferred_element_type=F32)
            e, l = _softmax_parts(s)
            outs.append(jnp.dot(e.astype(BF16), v, preferred_element_type=F32) / l)
        m_scr[p] = jnp.where(left, outs[0], outs[1]).astype(BF16)


def _latent_attention(qkv_scr, kc_scr, vc_scr, bias_ref, m_scr):
    left = lax.broadcasted_iota(jnp.int32, (GRID_W, LANES), 1) < HEAD_DIM
    n_loc = WIN_ROWS * GRID_W

    def body(i, carry):
        p = i // GRID_ROWS
        r = i % GRID_ROWS
        rs = jnp.clip(r - WIN_ROWS // 2, 0, GRID_ROWS - WIN_ROWS)
        q = qkv_scr[p, pl.ds(pl.multiple_of(r * GRID_W, GRID_W), GRID_W), :]
        zero = jnp.zeros_like(q)
        qm = jnp.concatenate([jnp.where(left, q, zero), jnp.where(left, zero, q)], axis=0)
        k0 = pl.multiple_of(rs * GRID_W, GRID_W)
        kl = qkv_scr[N_PAIRS + p, pl.ds(k0, n_loc), :]
        vl = qkv_scr[2 * N_PAIRS + p, pl.ds(k0, n_loc), :]
        d0 = rs - r + (WIN_ROWS - 1)
        bias = jnp.concatenate([
            jnp.concatenate([bias_ref[2 * p + hh, d0 + 2 * j] for j in range(WIN_ROWS // 2)], axis=1)
            for hh in range(2)], axis=0)
        s_loc = lax.dot_general(qm, kl, NT_DIMS, preferred_element_type=F32) + bias
        s_ctx = lax.dot_general(qm, kc_scr[p], NT_DIMS, preferred_element_type=F32)
        m = jnp.maximum(jnp.max(s_loc, axis=-1, keepdims=True), jnp.max(s_ctx, axis=-1, keepdims=True))
        e_loc = jnp.exp(s_loc - m)
        e_ctx = jnp.exp(s_ctx - m)
        l = jnp.sum(e_loc, axis=-1, keepdims=True) + jnp.sum(e_ctx, axis=-1, keepdims=True)
        o = (jnp.dot(e_loc.astype(BF16), vl, preferred_element_type=F32)
             + jnp.dot(e_ctx.astype(BF16), vc_scr[p], preferred_element_type=F32)) / l
        out = jnp.where(left, o[:GRID_W], o[GRID_W:])
        m_scr[p, pl.ds(pl.multiple_of(r * GRID_W, GRID_W), GRID_W), :] = out.astype(BF16)
        return carry

    lax.fori_loop(0, N_PAIRS * GRID_ROWS, body, 0)


def _router_bucket(h2, wr_hi_ref, wr_lo_ref, br_ref, n):
    h_hi = h2.astype(BF16)
    h_lo = (h2 - h_hi.astype(F32)).astype(BF16)
    logits = (jnp.dot(h_hi, wr_hi_ref[...], preferred_element_type=F32)
              + (jnp.dot(h_lo, wr_hi_ref[...], preferred_element_type=F32)
                 + jnp.dot(h_hi, wr_lo_ref[...], preferred_element_type=F32)))
    sel = jax.nn.sigmoid(logits.T[:N_EXPERTS]) + br_ref[...]
    rows = [sel[e:e + 1, :] for e in range(N_EXPERTS)]
    gscore = []
    for g in range(N_GROUPS):
        r4 = rows[g * GROUP_SIZE:(g + 1) * GROUP_SIZE]
        best = None
        for i in range(GROUP_SIZE):
            for j in range(i + 1, GROUP_SIZE):
                s = r4[i] + r4[j]
                best = s if best is None else jnp.maximum(best, s)
        gscore.append(best)
    gbest = jnp.zeros((1, n), jnp.int32)
    gval = gscore[0]
    for g in range(1, N_GROUPS):
        upd = gscore[g] > gval
        gbest = jnp.where(upd, g, gbest)
        gval = jnp.where(upd, gscore[g], gval)
    picked = []
    for e in range(N_EXPERTS):
        g = e // GROUP_SIZE
        rank = jnp.zeros((1, n), jnp.int32)
        for j in range(g * GROUP_SIZE, (g + 1) * GROUP_SIZE):
            if j == e:
                continue
            ahead = (rows[j] > rows[e]) | (rows[j] == rows[e]) if j < e else rows[j] > rows[e]
            rank = rank + ahead.astype(jnp.int32)
        picked.append((rank < 2) & (gbest == g))
    e_lo = jnp.zeros((1, n), jnp.int32)
    e_hi = jnp.zeros((1, n), jnp.int32)
    for e in range(N_EXPERTS):
        e_hi = jnp.where(picked[e], e, e_hi)
        e_lo = jnp.where(picked[N_EXPERTS - 1 - e], N_EXPERTS - 1 - e, e_lo)
    a = e_lo - GROUP_SIZE * gbest
    b = e_hi - GROUP_SIZE * gbest
    return gbest * PAIRS_PER_GROUP + jnp.right_shift(a * (2 * GROUP_SIZE - 1 - a), 1) + (b - a - 1)


def _slab(j, n):
    return pl.ds(j, n, stride=ROW_SLABS)


def _rows_to_2d(ref, n):
    return jnp.concatenate([ref[_slab(j, n), :] for j in range(ROW_SLABS)], axis=1)


def _store_slabs(ref, val, n, first_row=0):
    for j in range(ROW_SLABS):
        ref[_slab(first_row + j, n), :] = val[:, j * LANES:(j + 1) * LANES]


def _mixer_kernel(*refs, latent, has_prev, n):
    it = iter(refs)
    x_ref, mod_ref = next(it), next(it)
    if has_prev:
        y_ref, modp_ref = next(it), next(it)
        if not latent:
            kold_ref, vold_ref = next(it), next(it)
    if latent:
        ck_ref, cv_ref, bias_ref = next(it), next(it), next(it)
    (n1g_ref, w_in_ref, w_out_ref, poolw_ref, pscale_ref, convw_ref, convb_ref,
     lng_ref, lnb_ref, pww_ref, pwb_ref) = [next(it) for _ in range(11)]
    x1_ref = next(it)
    if not latent:
        knew_ref, vnew_ref = next(it), next(it)
    qkv_scr, u_scr, m_scr = next(it), next(it), next(it)
    if latent:
        kc_scr, vc_scr = next(it), next(it)

    x = x_ref[...]
    if has_prev:
        x = x + modp_ref[5:6, :] * _rows_to_2d(y_ref, n)
    layer = 0
    if has_prev and not latent:
        layer = kold_ref.shape[0]
        knew_ref[0:layer] = kold_ref[...]
        vnew_ref[0:layer] = vold_ref[...]
    sh1, sc1, g1 = mod_ref[0:1, :], mod_ref[1:2, :], mod_ref[2:3, :]

    hb = (_rms(x, n1g_ref[...]) * (1.0 + sc1) + sh1).astype(BF16)
    cw = 2 * LANES
    for j in range(D_IN // cw):
        zj = jnp.dot(hb, w_in_ref[:, j * cw:(j + 1) * cw], preferred_element_type=F32)
        if j < 6:
            part = j // 2
            zs = zj * (HEAD_DIM ** -0.5) if part == 0 else zj
            c = part * N_PAIRS + 2 * (j % 2)
            qkv_scr[c] = zs[:, :LANES].astype(BF16)
            qkv_scr[c + 1] = zs[:, LANES:].astype(BF16)
            if not latent and part == 1:
                knew_ref[layer, :, (j % 2) * cw:(j % 2 + 1) * cw] = zj
            if not latent and part == 2:
                vnew_ref[layer, :, (j % 2) * cw:(j % 2 + 1) * cw] = zj
        else:
            u_scr[j - 6] = zj

    if latent:
        for p in range(N_PAIRS):
            kc_scr[p] = ck_ref[:, p * LANES:(p + 1) * LANES].astype(BF16)
            vc_scr[p] = cv_ref[:, p * LANES:(p + 1) * LANES].astype(BF16)
        _latent_attention(qkv_scr, kc_scr, vc_scr, bias_ref, m_scr)
    else:
        _context_attention(qkv_scr, m_scr, n)

    row = lax.broadcasted_iota(jnp.int32, (n, D_POOL), 0)
    grp = lax.broadcasted_iota(jnp.int32, (n, D_POOL), 1) // (D_POOL // 4)
    up = u_scr[0]
    t2 = _pad_rows(up)
    t2 = t2 + _shift_rows(t2, 1)
    t4 = t2 + _shift_rows(t2, 2)
    t8 = t4 + _shift_rows(t4, 4)
    t16 = t8 + _shift_rows(t8, 8)
    s2, s4, s8, s16 = t2[:n], _shift_rows(t4, -1)[:n], _shift_rows(t8, -3)[:n], _shift_rows(t16, -7)[:n]
    wsum = jnp.where(grp == 0, s2, jnp.where(grp == 1, s4, jnp.where(grp == 2, s8, s16)))
    half = jnp.left_shift(1, grp)
    lo = jnp.maximum(row - half, 0)
    hi = jnp.minimum(row + half - 1, n - 1)
    diff = wsum / (hi - lo + 1).astype(F32) - up
    pooled = jnp.dot(diff.astype(BF16), poolw_ref[...], preferred_element_type=F32) * pscale_ref[...]
    m_scr[N_PAIRS] = pooled[:, :LANES].astype(BF16)
    m_scr[N_PAIRS + 1] = pooled[:, LANES:].astype(BF16)

    y = _pad_rows(u_scr[1] * jax.nn.sigmoid(u_scr[2]))
    acc = jnp.zeros(y.shape, F32) + convb_ref[...]
    for k in range(CONV_WIDTH):
        acc = acc + convw_ref[k:k + 1, :] * _shift_rows(y, CONV_WIDTH // 2 - k)
    acc = acc[:n]
    mu = jnp.mean(acc, axis=-1, keepdims=True)
    cen = acc - mu
    var = jnp.mean(cen * cen, axis=-1, keepdims=True)
    yn = _silu(cen * lax.rsqrt(var + EPS) * lng_ref[...] + lnb_ref[...])
    conv = jnp.dot(yn.astype(BF16), pww_ref[...], preferred_element_type=F32) + pwb_ref[...]
    m_scr[N_PAIRS + 2] = conv[:, :LANES].astype(BF16)
    m_scr[N_PAIRS + 3] = conv[:, LANES:].astype(BF16)

    mixed = jnp.concatenate([m_scr[j] for j in range(ROW_SLABS)], axis=1)
    x1_ref[...] = x + g1 * jnp.dot(mixed, w_out_ref[...], preferred_element_type=F32)


def _route_kernel(xp_ref, xs_ref, mod_ref, n2g_ref, wr_hi_ref, wr_lo_ref, br_ref,
                  h_ref, meta_ref, count_ref, run_scr):
    i = pl.program_id(0)
    n = ROUTE_TM

    @pl.when(i == 0)
    def _():
        run_scr[...] = jnp.zeros_like(run_scr)

    x1 = lax.cond(i < CTX_ROUTE_BLOCKS, lambda: xp_ref[...], lambda: xs_ref[...])
    h2 = _rms(x1, n2g_ref[...]) * (1.0 + mod_ref[4:5, :]) + mod_ref[3:4, :]
    _store_slabs(h_ref, h2, n)

    bucket = _router_bucket(h2, wr_hi_ref, wr_lo_ref, br_ref, n)
    onehot = lax.broadcasted_iota(jnp.int32, (BUCKET_ROWS, n), 0) == bucket
    upper = (lax.broadcasted_iota(jnp.int32, (n, n), 0) <= lax.broadcasted_iota(jnp.int32, (n, n), 1))
    incl = jnp.dot(onehot.astype(BF16), upper.astype(BF16), preferred_element_type=F32)
    run = run_scr[...]
    rank = jnp.sum(jnp.where(onehot, incl + run, 0.0), axis=0, keepdims=True) - 1.0
    run = run + jnp.sum(onehot.astype(F32), axis=1, keepdims=True)
    run_scr[...] = run
    meta_ref[...] = jnp.concatenate(
        [bucket, rank.astype(jnp.int32), jnp.zeros((META_ROWS - 2, n), jnp.int32)], axis=0)
    count_ref[...] = jnp.broadcast_to(run, count_ref.shape).astype(jnp.int32)


def _route(x1p, x1s, mod_l, n2g, wr_hi, wr_lo, br):
    per_seq = DEC_SEQ // ROUTE_TM
    last_ctx = CTX_ROUTE_BLOCKS - 1
    return pl.pallas_call(
        _route_kernel,
        grid=(N_TOKENS // ROUTE_TM,),
        in_specs=[
            pl.BlockSpec((ROUTE_TM, D_MODEL), lambda i: (jnp.minimum(i, last_ctx), 0)),
            pl.BlockSpec((ROUTE_TM, D_MODEL), lambda i: (jnp.maximum(i - CTX_ROUTE_BLOCKS, 0), 0)),
            pl.BlockSpec((None, 6, D_MODEL),
                         lambda i: (jnp.where(i < CTX_ROUTE_BLOCKS, 0, (i - CTX_ROUTE_BLOCKS) // per_seq + 1), 0, 0)),
            pl.BlockSpec((1, D_MODEL), lambda i: (0, 0)),
            pl.BlockSpec((D_MODEL, LANES), lambda i: (0, 0)),
            pl.BlockSpec((D_MODEL, LANES), lambda i: (0, 0)),
            pl.BlockSpec((N_EXPERTS, 1), lambda i: (0, 0)),
        ],
        out_specs=[pl.BlockSpec((ROUTE_TM * ROW_SLABS, LANES), lambda i: (i, 0)),
                   pl.BlockSpec((META_ROWS, ROUTE_TM), lambda i: (0, i)),
                   pl.BlockSpec((BUCKET_ROWS, LANES), lambda i: (0, 0))],
        out_shape=[jax.ShapeDtypeStruct((N_TOKENS * ROW_SLABS, LANES), F32),
                   jax.ShapeDtypeStruct((META_ROWS, N_TOKENS), jnp.int32),
                   jax.ShapeDtypeStruct((BUCKET_ROWS, LANES), jnp.int32)],
        scratch_shapes=[pltpu.VMEM((BUCKET_ROWS, 1), F32)],
        compiler_params=pltpu.CompilerParams(dimension_semantics=("arbitrary",)),
        name="norm_route",
    )(x1p, x1s, mod_l, n2g, wr_hi, wr_lo, br)


def _const_spec(shape):
    nd = len(shape)
    return pl.BlockSpec(shape, lambda b: (0,) * nd, pipeline_mode=pl.Buffered(1))


def _stream_specs(latent):
    n = DEC_SEQ if latent else SEQ
    first = BATCH * SEQ // n if latent else 0
    own = pl.BlockSpec((n, D_MODEL), lambda b: (b, 0))
    rows = pl.BlockSpec((n * ROW_SLABS, LANES), lambda b: (first + b, 0))
    mod = pl.BlockSpec((None, 6, D_MODEL), (lambda b: (b + 1, 0, 0)) if latent else (lambda b: (0, 0, 0)))
    return n, own, rows, mod


def _mixer(x, mod_l, weights, *, latent, prev=None, kv_old=None, cache=None, bias=None, layer=0):
    n, own_spec, rows_spec, mod_spec = _stream_specs(latent)
    tok = x.shape[0]
    nb = tok // n
    in_specs = [own_spec, mod_spec]
    args = [x, mod_l]
    if prev is not None:
        in_specs += [rows_spec, mod_spec]
        args += list(prev)
        if not latent:
            in_specs += [pl.BlockSpec((None, layer, n, D_ATTN), lambda b: (b, 0, 0, 0))] * 2
            args += list(kv_old)
    if latent:
        cache_spec = pl.BlockSpec((None, None, PAST_LEN, D_ATTN), lambda b: (b, layer, 0, 0))
        in_specs += [cache_spec, cache_spec, _const_spec(bias.shape)]
        args += [cache[0], cache[1], bias]
    in_specs += [_const_spec(w.shape) for w in weights]
    args += list(weights)
    out_shape = [jax.ShapeDtypeStruct((tok, D_MODEL), F32)]
    out_specs = [own_spec]
    if not latent:
        out_shape += [jax.ShapeDtypeStruct((nb, layer + 1, n, D_ATTN), F32)] * 2
        out_specs += [pl.BlockSpec((None, layer + 1, n, D_ATTN), lambda b: (b, 0, 0, 0))] * 2
    scratch = [pltpu.VMEM((3 * N_PAIRS, n, LANES), BF16),
               pltpu.VMEM((3, n, D_POOL), F32),
               pltpu.VMEM((ROW_SLABS, n, LANES), BF16)]
    if latent:
        scratch += [pltpu.VMEM((N_PAIRS, PAST_LEN, LANES), BF16),
                    pltpu.VMEM((N_PAIRS, PAST_LEN, LANES), BF16)]
    return pl.pallas_call(
        functools.partial(_mixer_kernel, latent=latent, has_prev=prev is not None, n=n),
        grid=(nb,),
        in_specs=in_specs,
        out_specs=out_specs,
        out_shape=out_shape,
        scratch_shapes=scratch,
        compiler_params=pltpu.CompilerParams(dimension_semantics=("arbitrary",),
                                             vmem_limit_bytes=VMEM_LIMIT),
        name="mixer_latent" if latent else "mixer_context",
    )(*args)


def _bucket_tables():
    lo, hi = [], []
    for g in range(N_GROUPS):
        for a in range(GROUP_SIZE):
            for b in range(a + 1, GROUP_SIZE):
                lo.append(g * GROUP_SIZE + a)
                hi.append(g * GROUP_SIZE + b)
    return np.asarray(lo, np.int32), np.asarray(hi, np.int32)


def _route_plan(bucket, rank, count):
    tiles_b = (count + MOE_TM - 1) // MOE_TM
    tile_end = jnp.cumsum(tiles_b)
    tile_start = tile_end - tiles_b
    pos = tile_start[bucket] * MOE_TM + rank
    row_tok = jnp.zeros(((MOE_TILES + 1) * MOE_TM,), jnp.int32).at[pos].set(
        jnp.arange(N_TOKENS, dtype=jnp.int32), unique_indices=True)
    n_used = tile_end[-1]
    tile = jnp.arange(MOE_TILES, dtype=jnp.int32)
    tile_b = jnp.sum((tile[:, None] >= tile_end[None, :]).astype(jnp.int32), axis=1)
    last_b = jnp.sum((n_used - 1 >= tile_end).astype(jnp.int32))
    tile_b = jnp.minimum(tile_b, last_b)
    n_valid = jnp.clip(count[tile_b] - (tile - tile_start[tile_b]) * MOE_TM, 0, MOE_TM)
    local = jnp.arange(MOE_TM, dtype=jnp.int32)[None, :]
    spare = N_TOKENS + tile_b[:, None] * MOE_TM + local
    row_dst = jnp.where(local < n_valid[:, None], row_tok[:MOE_TILES * MOE_TM].reshape(MOE_TILES, MOE_TM), spare)
    lo_tab, hi_tab = _bucket_tables()
    return (jnp.asarray(lo_tab)[tile_b], jnp.asarray(hi_tab)[tile_b],
            n_used.reshape(1).astype(jnp.int32), row_tok, row_dst.reshape(-1))


def _row_slabs(ref, row):
    return ref.at[pl.ds(pl.multiple_of(row * ROW_SLABS, ROW_SLABS), ROW_SLABS), :]


def _moe_kernel(e0_ref, e1_ref, nu_ref, tok_ref, dst_ref, h_hbm, wrt_ref, wg0_ref, wu0_ref, wd0_ref,
                wg1_ref, wu1_ref, wd1_ref, y_hbm, buf, xb_scr, stg, gsem, ssem):
    i = pl.program_id(0)
    n_used = nu_ref[0]
    half = i % 2
    chunks = [(k * MOE_TM // MOE_PHASES, (k + 1) * MOE_TM // MOE_PHASES) for k in range(MOE_PHASES)]

    def start_gather(tile, rows):
        base = tile * MOE_TM
        for r in range(*rows):
            pltpu.make_async_copy(_row_slabs(h_hbm, tok_ref[base + r]), _row_slabs(buf, r), gsem).start(priority=0)

    def start_scatter(tile, rows):
        base = tile * MOE_TM
        src0 = (tile % 2) * MOE_TM
        for r in range(*rows):
            pltpu.make_async_copy(_row_slabs(stg, src0 + r), _row_slabs(y_hbm, dst_ref[base + r]),
                                  ssem.at[tile % 2]).start(priority=1)

    def wait_gather():
        pltpu.make_async_copy(h_hbm.at[pl.ds(0, MOE_TM * ROW_SLABS), :], buf, gsem).wait()

    def wait_scatter(which):
        n = MOE_TM * ROW_SLABS
        pltpu.make_async_copy(stg.at[pl.ds(0, n), :], y_hbm.at[pl.ds(0, n), :], ssem.at[which]).wait()

    def overlap_dmas(k):
        start_gather(i + 1, chunks[k])

        @pl.when(i >= 1)
        def _():
            start_scatter(i - 1, chunks[k])

    @pl.when(i == 0)
    def _():
        start_gather(0, (0, MOE_TM))

    @pl.when(i < n_used)
    def _():
        wait_gather()
        w0 = wrt_ref[pl.ds(e0_ref[i], 1), :]
        w1 = wrt_ref[pl.ds(e1_ref[i], 1), :]
        l0 = jnp.zeros((MOE_TM, LANES), F32)
        l1 = jnp.zeros((MOE_TM, LANES), F32)
        for j in range(ROW_SLABS):
            xj = buf[_slab(j, MOE_TM), :]
            xb_scr[:, j * LANES:(j + 1) * LANES] = xj.astype(BF16)
            l0 = l0 + xj * w0[:, j * LANES:(j + 1) * LANES]
            l1 = l1 + xj * w1[:, j * LANES:(j + 1) * LANES]
        a0 = jax.nn.sigmoid(jnp.sum(l0, axis=1, keepdims=True))
        a1 = jax.nn.sigmoid(jnp.sum(l1, axis=1, keepdims=True))
        den = a0 + a1
        acc = None
        phase = 0
        for gate, wg_ref, wu_ref, wd_ref in ((a0 / den, wg0_ref, wu0_ref, wd0_ref),
                                             (a1 / den, wg1_ref, wu1_ref, wd1_ref)):
            a = jnp.dot(xb_scr[...], wg_ref[...], preferred_element_type=F32)
            overlap_dmas(phase)
            b = jnp.dot(xb_scr[...], wu_ref[...], preferred_element_type=F32)
            hid = (_silu(a) * b * gate).astype(BF16)
            overlap_dmas(phase + 1)
            o = jnp.dot(hid, wd_ref[...], preferred_element_type=F32)
            acc = o if acc is None else acc + o
            overlap_dmas(phase + 2)
            phase += 3

        @pl.when(i >= 2)
        def _():
            wait_scatter(half)

        @pl.when(half == 0)
        def _():
            _store_slabs(stg, acc, MOE_TM, 0)

        @pl.when(half == 1)
        def _():
            _store_slabs(stg, acc, MOE_TM, MOE_TM * ROW_SLABS)

        @pl.when(i == n_used - 1)
        def _():
            start_scatter(i, (0, MOE_TM))
            wait_scatter(half)

            @pl.when(i >= 1)
            def _():
                wait_scatter(1 - half)

            wait_gather()


def _moe(h, plan, wrt, wg, wu, wd, layer):
    def w_spec(which, shape):
        return pl.BlockSpec((None, None) + shape,
                            lambda i, e0, e1, nu, tok, dst: (layer, (e0, e1)[which][i], 0, 0))

    up_shape, down_shape = (D_MODEL, D_FF), (D_FF, D_MODEL)
    grid_spec = pltpu.PrefetchScalarGridSpec(
        num_scalar_prefetch=5,
        grid=(MOE_TILES,),
        in_specs=[pl.BlockSpec(memory_space=pl.ANY),
                  pl.BlockSpec((N_EXPERTS, D_MODEL), lambda i, *_: (0, 0)),
                  w_spec(0, up_shape), w_spec(0, up_shape), w_spec(0, down_shape),
                  w_spec(1, up_shape), w_spec(1, up_shape), w_spec(1, down_shape)],
        out_specs=pl.BlockSpec(memory_space=pl.ANY),
        scratch_shapes=[pltpu.VMEM((MOE_TM * ROW_SLABS, LANES), F32),
                        pltpu.VMEM((MOE_TM, D_MODEL), BF16),
                        pltpu.VMEM((2 * MOE_TM * ROW_SLABS, LANES), F32),
                        pltpu.SemaphoreType.DMA(()),
                        pltpu.SemaphoreType.DMA((2,))],
    )
    return pl.pallas_call(
        _moe_kernel,
        grid_spec=grid_spec,
        out_shape=jax.ShapeDtypeStruct((MOE_TILES * MOE_TM * ROW_SLABS, LANES), F32),
        compiler_params=pltpu.CompilerParams(dimension_semantics=("arbitrary",),
                                             vmem_limit_bytes=VMEM_LIMIT),
        name="moe_routed",
    )(*plan, h, wrt, wg, wu, wd, wg, wu, wd)


def _final_kernel(x1_ref, y_ref, mod_ref, fg_ref, o_ref):
    x2 = x1_ref[...] + mod_ref[5:6, :] * _rows_to_2d(y_ref, x1_ref.shape[0])
    o_ref[...] = _rms(x2, fg_ref[...])


def _final(x1, y, mod_l, fg, *, latent):
    n, own_spec, rows_spec, mod_spec = _stream_specs(latent)
    return pl.pallas_call(
        _final_kernel,
        grid=(x1.shape[0] // n,),
        in_specs=[own_spec, rows_spec, mod_spec, pl.BlockSpec((1, D_MODEL), lambda b: (0, 0))],
        out_specs=own_spec,
        out_shape=jax.ShapeDtypeStruct(x1.shape, F32),
        name="final_norm",
    )(x1, y, mod_l, fg)


def kernel(x_prompt, x_sample, cache_k, cache_v, c, c_ctx, w_ada, b_ada, norm1_g, norm2_g, w_in, w_out, rpb, pool_w, pool_scale, conv_w, conv_b, conv_ln_g, conv_ln_b, conv_pw_w, conv_pw_b, w_router, b_router, moe_w_gate, moe_w_up, moe_w_down, final_g):
    xp = x_prompt.reshape(BATCH * SEQ, D_MODEL)
    xs = x_sample.reshape(DEC_BATCH * DEC_SEQ, D_MODEL)
    cond = jnp.concatenate([c_ctx[None, :], c, jnp.zeros((N_MOD_ROWS - 1 - DEC_BATCH, D_MODEL), F32)], axis=0)
    mod = _ada_modulation(cond, w_ada, b_ada).reshape(DEPTH, N_MOD_ROWS, 6, D_MODEL)
    bias = _rpb_tiles(rpb)
    ck = cache_k.reshape(DEC_BATCH, DEPTH, PAST_LEN, D_ATTN)
    cv = cache_v.reshape(DEC_BATCH, DEPTH, PAST_LEN, D_ATTN)
    wrt = w_router.T
    wr = jnp.pad(w_router, ((0, 0), (0, LANES - N_EXPERTS)))
    wr_hi = wr.astype(BF16)
    wr_lo = (wr - wr_hi.astype(F32)).astype(BF16)
    br = b_router.reshape(N_EXPERTS, 1)
    fg = final_g.reshape(1, D_MODEL)
    wg, wu, wd = moe_w_gate.astype(BF16), moe_w_up.astype(BF16), moe_w_down.astype(BF16)
    new_k = new_v = None
    prev = None
    for l in range(DEPTH):
        pool_bd = jnp.zeros((D_POOL, D_POOL), F32)
        gd = D_POOL // 4
        for g in range(4):
            pool_bd = pool_bd.at[g * gd:(g + 1) * gd, g * gd:(g + 1) * gd].set(pool_w[l, g])
        weights = [norm1_g[l].reshape(1, -1),
                   w_in[l].astype(BF16), w_out[l].astype(BF16),
                   pool_bd.astype(BF16), pool_scale[l].reshape(1, -1),
                   conv_w[l], conv_b[l].reshape(1, -1),
                   conv_ln_g[l].reshape(1, -1), conv_ln_b[l].reshape(1, -1),
                   conv_pw_w[l].astype(BF16), conv_pw_b[l].reshape(1, -1)]

        xp, new_k, new_v = _mixer(xp, mod[l], weights, latent=False, prev=prev, kv_old=(new_k, new_v), layer=l)
        xs, = _mixer(xs, mod[l], weights, latent=True, prev=prev, cache=(ck, cv), bias=bias[l], layer=l)

        h, meta, count = _route(xp, xs, mod[l], norm2_g[l].reshape(1, -1), wr_hi, wr_lo, br)
        y = _moe(h, _route_plan(meta[0], meta[1], count[:N_BUCKETS, 0]), wrt, wg, wu, wd, l)
        prev = (y, mod[l])

    y_prompt = _final(xp, y, mod[DEPTH - 1], fg, latent=False).reshape(BATCH, SEQ, D_MODEL)
    y_sample = _final(xs, y, mod[DEPTH - 1], fg, latent=True).reshape(DEC_BATCH, DEC_SEQ, D_MODEL)
    new_cache_k = new_k.reshape(BATCH, DEPTH, SEQ, N_HEADS, HEAD_DIM)
    new_cache_v = new_v.reshape(BATCH, DEPTH, SEQ, N_HEADS, HEAD_DIM)
    return (y_prompt, y_sample, new_cache_k, new_cache_v)
```

```python
import functools

import numpy as np
import jax
import jax.numpy as jnp
from jax import lax
from jax.experimental import pallas as pl
from jax.experimental.pallas import tpu as pltpu

D_MODEL = 1024
BATCH = 32
SEQ = 256
DEPTH = 2
DEC_BATCH = 8
DEC_SEQ = 1024
PAST_LEN = 512
GRID_W = 64
GRID_ROWS = DEC_SEQ // GRID_W
D_ATTN = 512
HEAD_DIM = 64
N_HEADS = 8
N_PAIRS = N_HEADS // 2
WIN_ROWS = 8
WIN_COLS = 16
D_POOL = 256
D_CONV = 256
CONV_WIDTH = 31
D_IN = 3 * D_ATTN + D_POOL + 2 * D_CONV
N_EXPERTS = 16
N_GROUPS = 4
GROUP_SIZE = N_EXPERTS // N_GROUPS
D_FF = 512
EPS = 1e-6
N_MOD_ROWS = 16
N_DR = 2 * WIN_ROWS - 1
N_DC = 2 * WIN_COLS - 1
N_BIAS_TILES = N_DR - 1
SEQ_PAD = 16
LANES = 128
ROW_SLABS = D_MODEL // LANES
N_TOKENS = BATCH * SEQ + DEC_BATCH * DEC_SEQ
PAIRS_PER_GROUP = GROUP_SIZE * (GROUP_SIZE - 1) // 2
N_BUCKETS = N_GROUPS * PAIRS_PER_GROUP
BUCKET_ROWS = 32
META_ROWS = 8
ROUTE_TM = 512
CTX_ROUTE_BLOCKS = BATCH * SEQ // ROUTE_TM
MOE_TM = 256
MOE_TILES = N_TOKENS // MOE_TM + N_BUCKETS
PERM_TM = 512
DMA_UNROLL = 8
VMEM_LIMIT = 60 * 1024 * 1024

F32 = jnp.float32
BF16 = jnp.bfloat16
HIGHEST = lax.Precision.HIGHEST
NT_DIMS = (((1,), (1,)), ((), ()))


def _rms(x, g):
    return x * lax.rsqrt(jnp.mean(x * x, axis=-1, keepdims=True) + EPS) * g


def _silu(x):
    return x * jax.nn.sigmoid(x)


def _ada_kernel(cond_ref, w_ref, b_ref, o_ref):
    o_ref[...] = jnp.dot(_silu(cond_ref[...]), w_ref[...], preferred_element_type=F32,
                         precision=HIGHEST) + b_ref[...]


def _ada_modulation(cond, w_ada, b_ada):
    n_col = 6 * D_MODEL // D_MODEL
    return pl.pallas_call(
        _ada_kernel,
        grid=(DEPTH, n_col),
        in_specs=[
            pl.BlockSpec((N_MOD_ROWS, D_MODEL), lambda l, n: (0, 0)),
            pl.BlockSpec((None, D_MODEL, D_MODEL), lambda l, n: (l, 0, n)),
            pl.BlockSpec((None, 1, D_MODEL), lambda l, n: (l, 0, n)),
        ],
        out_specs=pl.BlockSpec((None, N_MOD_ROWS, D_MODEL), lambda l, n: (l, 0, n)),
        out_shape=jax.ShapeDtypeStruct((DEPTH, N_MOD_ROWS, 6 * D_MODEL), F32),
        name="ada_modulation",
    )(cond, w_ada, b_ada.reshape(DEPTH, 1, 6 * D_MODEL))


def _rpb_kernel(rpb_ref, o_ref):
    base = (pl.program_id(0) * N_HEADS + pl.program_id(1)) * (N_DR * N_DC)
    cq = lax.broadcasted_iota(jnp.int32, (GRID_W, LANES), 0)
    lane = lax.broadcasted_iota(jnp.int32, (GRID_W, LANES), 1)
    ck = lane & (GRID_W - 1)
    right = lane >= GRID_W
    dcm = jnp.clip(ck - cq, -(WIN_COLS - 1), WIN_COLS - 1) + (WIN_COLS - 1)
    cs = jnp.clip(cq - WIN_COLS // 2, 0, GRID_W - WIN_COLS)
    valid = (ck >= cs) & (ck < cs + WIN_COLS)
    for d in range(N_BIAS_TILES):
        acc = jnp.zeros((GRID_W, LANES), F32)
        for dc in range(N_DC):
            v = jnp.where(right, rpb_ref[base + (d + 1) * N_DC + dc], rpb_ref[base + d * N_DC + dc])
            acc = jnp.where(dcm == dc, v, acc)
        o_ref[d] = jnp.where(valid, acc, -jnp.inf)


def _rpb_tiles(rpb):
    return pl.pallas_call(
        _rpb_kernel,
        grid=(DEPTH, N_HEADS),
        in_specs=[pl.BlockSpec(memory_space=pltpu.SMEM)],
        out_specs=pl.BlockSpec((None, None, N_BIAS_TILES, GRID_W, LANES), lambda l, h: (l, h, 0, 0, 0)),
        out_shape=jax.ShapeDtypeStruct((DEPTH, N_HEADS, N_BIAS_TILES, GRID_W, LANES), F32),
        name="rpb_tiles",
    )(rpb.reshape(-1))


def _shift_rows(a, s):
    return a if s == 0 else pltpu.roll(a, s % a.shape[0], 0)


def _pad_rows(a):
    return jnp.concatenate([a, jnp.zeros((SEQ_PAD, a.shape[1]), a.dtype)], axis=0)


def _softmax_parts(s):
    m = jnp.max(s, axis=-1, keepdims=True)
    e = jnp.exp(s - m)
    return e, jnp.sum(e, axis=-1, keepdims=True)


def _context_attention(qkv_scr, m_scr, n):
    left = lax.broadcasted_iota(jnp.int32, (n, LANES), 1) < HEAD_DIM
    for p in range(N_PAIRS):
        q = qkv_scr[p]
        k = qkv_scr[N_PAIRS + p]
        v = qkv_scr[2 * N_PAIRS + p]
        zero = jnp.zeros_like(q)
        outs = []
        for qh in (jnp.where(left, q, zero), jnp.where(left, zero, q)):
            s = lax.dot_general(qh, k, NT_DIMS, preferred_element_type=F32)
            e, l = _softmax_parts(s)
            outs.append(jnp.dot(e.astype(BF16), v, preferred_element_type=F32) / l)
        m_scr[p] = jnp.where(left, outs[0], outs[1]).astype(BF16)


def _latent_attention(qkv_scr, kc_scr, vc_scr, bias_ref, m_scr):
    left = lax.broadcasted_iota(jnp.int32, (GRID_W, LANES), 1) < HEAD_DIM
    n_loc = WIN_ROWS * GRID_W

    def body(i, carry):
        p = i // GRID_ROWS
        r = i % GRID_ROWS
        rs = jnp.clip(r - WIN_ROWS // 2, 0, GRID_ROWS - WIN_ROWS)
        q = qkv_scr[p, pl.ds(pl.multiple_of(r * GRID_W, GRID_W), GRID_W), :]
        zero = jnp.zeros_like(q)
        qm = jnp.concatenate([jnp.where(left, q, zero), jnp.where(left, zero, q)], axis=0)
        k0 = pl.multiple_of(rs * GRID_W, GRID_W)
        kl = qkv_scr[N_PAIRS + p, pl.ds(k0, n_loc), :]
        vl = qkv_scr[2 * N_PAIRS + p, pl.ds(k0, n_loc), :]
        d0 = rs - r + (WIN_ROWS - 1)
        bias = jnp.concatenate([
            jnp.concatenate([bias_ref[2 * p + hh, d0 + 2 * j] for j in range(WIN_ROWS // 2)], axis=1)
            for hh in range(2)], axis=0)
        s_loc = lax.dot_general(qm, kl, NT_DIMS, preferred_element_type=F32) + bias
        s_ctx = lax.dot_general(qm, kc_scr[p], NT_DIMS, preferred_element_type=F32)
        m = jnp.maximum(jnp.max(s_loc, axis=-1, keepdims=True), jnp.max(s_ctx, axis=-1, keepdims=True))
        e_loc = jnp.exp(s_loc - m)
        e_ctx = jnp.exp(s_ctx - m)
        l = jnp.sum(e_loc, axis=-1, keepdims=True) + jnp.sum(e_ctx, axis=-1, keepdims=True)
        o = (jnp.dot(e_loc.astype(BF16), vl, preferred_element_type=F32)
             + jnp.dot(e_ctx.astype(BF16), vc_scr[p], preferred_element_type=F32)) / l
        out = jnp.where(left, o[:GRID_W], o[GRID_W:])
        m_scr[p, pl.ds(pl.multiple_of(r * GRID_W, GRID_W), GRID_W), :] = out.astype(BF16)
        return carry

    lax.fori_loop(0, N_PAIRS * GRID_ROWS, body, 0)


def _router_bucket(h2, wr_hi_ref, wr_lo_ref, br_ref, n):
    h_hi = h2.astype(BF16)
    h_lo = (h2 - h_hi.astype(F32)).astype(BF16)
    logits = (jnp.dot(h_hi, wr_hi_ref[...], preferred_element_type=F32)
              + (jnp.dot(h_lo, wr_hi_ref[...], preferred_element_type=F32)
                 + jnp.dot(h_hi, wr_lo_ref[...], preferred_element_type=F32)))
    sel = jax.nn.sigmoid(logits.T[:N_EXPERTS]) + br_ref[...]
    rows = [sel[e:e + 1, :] for e in range(N_EXPERTS)]
    gscore = []
    for g in range(N_GROUPS):
        r4 = rows[g * GROUP_SIZE:(g + 1) * GROUP_SIZE]
        best = None
        for i in range(GROUP_SIZE):
            for j in range(i + 1, GROUP_SIZE):
                s = r4[i] + r4[j]
                best = s if best is None else jnp.maximum(best, s)
        gscore.append(best)
    gbest = jnp.zeros((1, n), jnp.int32)
    gval = gscore[0]
    for g in range(1, N_GROUPS):
        upd = gscore[g] > gval
        gbest = jnp.where(upd, g, gbest)
        gval = jnp.where(upd, gscore[g], gval)
    picked = []
    for e in range(N_EXPERTS):
        g = e // GROUP_SIZE
        rank = jnp.zeros((1, n), jnp.int32)
        for j in range(g * GROUP_SIZE, (g + 1) * GROUP_SIZE):
            if j == e:
                continue
            ahead = (rows[j] > rows[e]) | (rows[j] == rows[e]) if j < e else rows[j] > rows[e]
            rank = rank + ahead.astype(jnp.int32)
        picked.append((rank < 2) & (gbest == g))
    e_lo = jnp.zeros((1, n), jnp.int32)
    e_hi = jnp.zeros((1, n), jnp.int32)
    for e in range(N_EXPERTS):
        e_hi = jnp.where(picked[e], e, e_hi)
        e_lo = jnp.where(picked[N_EXPERTS - 1 - e], N_EXPERTS - 1 - e, e_lo)
    a = e_lo - GROUP_SIZE * gbest
    b = e_hi - GROUP_SIZE * gbest
    return gbest * PAIRS_PER_GROUP + jnp.right_shift(a * (2 * GROUP_SIZE - 1 - a), 1) + (b - a - 1)


def _slab(j, n):
    return pl.ds(j, n, stride=ROW_SLABS)


def _rows_to_2d(ref, n):
    return jnp.concatenate([ref[_slab(j, n), :] for j in range(ROW_SLABS)], axis=1)


def _store_slabs(ref, val, n, first_row=0):
    for j in range(ROW_SLABS):
        ref[_slab(first_row + j, n), :] = val[:, j * LANES:(j + 1) * LANES]


def _mixer_kernel(*refs, latent, has_prev, n):
    it = iter(refs)
    x_ref, mod_ref = next(it), next(it)
    if has_prev:
        y_ref, modp_ref = next(it), next(it)
        if not latent:
            kold_ref, vold_ref = next(it), next(it)
    if latent:
        ck_ref, cv_ref, bias_ref = next(it), next(it), next(it)
    (n1g_ref, w_in_ref, w_out_ref, poolw_ref, pscale_ref, convw_ref, convb_ref,
     lng_ref, lnb_ref, pww_ref, pwb_ref) = [next(it) for _ in range(11)]
    x1_ref = next(it)
    if not latent:
        knew_ref, vnew_ref = next(it), next(it)
    qkv_scr, u_scr, m_scr = next(it), next(it), next(it)
    if latent:
        kc_scr, vc_scr = next(it), next(it)

    x = x_ref[...]
    if has_prev:
        x = x + modp_ref[5:6, :] * _rows_to_2d(y_ref, n)
    layer = 0
    if has_prev and not latent:
        layer = kold_ref.shape[0]
        knew_ref[0:layer] = kold_ref[...]
        vnew_ref[0:layer] = vold_ref[...]
    sh1, sc1, g1 = mod_ref[0:1, :], mod_ref[1:2, :], mod_ref[2:3, :]

    hb = (_rms(x, n1g_ref[...]) * (1.0 + sc1) + sh1).astype(BF16)
    cw = 2 * LANES
    for j in range(D_IN // cw):
        zj = jnp.dot(hb, w_in_ref[:, j * cw:(j + 1) * cw], preferred_element_type=F32)
        if j < 6:
            part = j // 2
            zs = zj * (HEAD_DIM ** -0.5) if part == 0 else zj
            c = part * N_PAIRS + 2 * (j % 2)
            qkv_scr[c] = zs[:, :LANES].astype(BF16)
            qkv_scr[c + 1] = zs[:, LANES:].astype(BF16)
            if not latent and part == 1:
                knew_ref[layer, :, (j % 2) * cw:(j % 2 + 1) * cw] = zj
            if not latent and part == 2:
                vnew_ref[layer, :, (j % 2) * cw:(j % 2 + 1) * cw] = zj
        else:
            u_scr[j - 6] = zj

    if latent:
        for p in range(N_PAIRS):
            kc_scr[p] = ck_ref[:, p * LANES:(p + 1) * LANES].astype(BF16)
            vc_scr[p] = cv_ref[:, p * LANES:(p + 1) * LANES].astype(BF16)
        _latent_attention(qkv_scr, kc_scr, vc_scr, bias_ref, m_scr)
    else:
        _context_attention(qkv_scr, m_scr, n)

    row = lax.broadcasted_iota(jnp.int32, (n, D_POOL), 0)
    grp = lax.broadcasted_iota(jnp.int32, (n, D_POOL), 1) // (D_POOL // 4)
    up = u_scr[0]
    t2 = _pad_rows(up)
    t2 = t2 + _shift_rows(t2, 1)
    t4 = t2 + _shift_rows(t2, 2)
    t8 = t4 + _shift_rows(t4, 4)
    t16 = t8 + _shift_rows(t8, 8)
    s2, s4, s8, s16 = t2[:n], _shift_rows(t4, -1)[:n], _shift_rows(t8, -3)[:n], _shift_rows(t16, -7)[:n]
    wsum = jnp.where(grp == 0, s2, jnp.where(grp == 1, s4, jnp.where(grp == 2, s8, s16)))
    half = jnp.left_shift(1, grp)
    lo = jnp.maximum(row - half, 0)
    hi = jnp.minimum(row + half - 1, n - 1)
    diff = wsum / (hi - lo + 1).astype(F32) - up
    pooled = jnp.dot(diff.astype(BF16), poolw_ref[...], preferred_element_type=F32) * pscale_ref[...]
    m_scr[N_PAIRS] = pooled[:, :LANES].astype(BF16)
    m_scr[N_PAIRS + 1] = pooled[:, LANES:].astype(BF16)

    y = _pad_rows(u_scr[1] * jax.nn.sigmoid(u_scr[2]))
    acc = jnp.zeros(y.shape, F32) + convb_ref[...]
    for k in range(CONV_WIDTH):
        acc = acc + convw_ref[k:k + 1, :] * _shift_rows(y, CONV_WIDTH // 2 - k)
    acc = acc[:n]
    mu = jnp.mean(acc, axis=-1, keepdims=True)
    cen = acc - mu
    var = jnp.mean(cen * cen, axis=-1, keepdims=True)
    yn = _silu(cen * lax.rsqrt(var + EPS) * lng_ref[...] + lnb_ref[...])
    conv = jnp.dot(yn.astype(BF16), pww_ref[...], preferred_element_type=F32) + pwb_ref[...]
    m_scr[N_PAIRS + 2] = conv[:, :LANES].astype(BF16)
    m_scr[N_PAIRS + 3] = conv[:, LANES:].astype(BF16)

    mixed = jnp.concatenate([m_scr[j] for j in range(ROW_SLABS)], axis=1)
    x1_ref[...] = x + g1 * jnp.dot(mixed, w_out_ref[...], preferred_element_type=F32)


def _route_kernel(xp_ref, xs_ref, mod_ref, n2g_ref, wr_hi_ref, wr_lo_ref, br_ref,
                  h_ref, meta_ref, count_ref, run_scr):
    i = pl.program_id(0)
    n = ROUTE_TM

    @pl.when(i == 0)
    def _():
        run_scr[...] = jnp.zeros_like(run_scr)

    x1 = lax.cond(i < CTX_ROUTE_BLOCKS, lambda: xp_ref[...], lambda: xs_ref[...])
    h2 = _rms(x1, n2g_ref[...]) * (1.0 + mod_ref[4:5, :]) + mod_ref[3:4, :]
    _store_slabs(h_ref, h2, n)

    bucket = _router_bucket(h2, wr_hi_ref, wr_lo_ref, br_ref, n)
    onehot = lax.broadcasted_iota(jnp.int32, (BUCKET_ROWS, n), 0) == bucket
    upper = (lax.broadcasted_iota(jnp.int32, (n, n), 0) <= lax.broadcasted_iota(jnp.int32, (n, n), 1))
    incl = jnp.dot(onehot.astype(BF16), upper.astype(BF16), preferred_element_type=F32)
    run = run_scr[...]
    rank = jnp.sum(jnp.where(onehot, incl + run, 0.0), axis=0, keepdims=True) - 1.0
    run = run + jnp.sum(onehot.astype(F32), axis=1, keepdims=True)
    run_scr[...] = run
    meta_ref[...] = jnp.concatenate(
        [bucket, rank.astype(jnp.int32), jnp.zeros((META_ROWS - 2, n), jnp.int32)], axis=0)
    count_ref[...] = jnp.broadcast_to(run, count_ref.shape).astype(jnp.int32)


def _route(x1p, x1s, mod_l, n2g, wr_hi, wr_lo, br):
    per_seq = DEC_SEQ // ROUTE_TM
    last_ctx = CTX_ROUTE_BLOCKS - 1
    return pl.pallas_call(
        _route_kernel,
        grid=(N_TOKENS // ROUTE_TM,),
        in_specs=[
            pl.BlockSpec((ROUTE_TM, D_MODEL), lambda i: (jnp.minimum(i, last_ctx), 0)),
            pl.BlockSpec((ROUTE_TM, D_MODEL), lambda i: (jnp.maximum(i - CTX_ROUTE_BLOCKS, 0), 0)),
            pl.BlockSpec((None, 6, D_MODEL),
                         lambda i: (jnp.where(i < CTX_ROUTE_BLOCKS, 0, (i - CTX_ROUTE_BLOCKS) // per_seq + 1), 0, 0)),
            pl.BlockSpec((1, D_MODEL), lambda i: (0, 0)),
            pl.BlockSpec((D_MODEL, LANES), lambda i: (0, 0)),
            pl.BlockSpec((D_MODEL, LANES), lambda i: (0, 0)),
            pl.BlockSpec((N_EXPERTS, 1), lambda i: (0, 0)),
        ],
        out_specs=[pl.BlockSpec((ROUTE_TM * ROW_SLABS, LANES), lambda i: (i, 0)),
                   pl.BlockSpec((META_ROWS, ROUTE_TM), lambda i: (0, i)),
                   pl.BlockSpec((BUCKET_ROWS, LANES), lambda i: (0, 0))],
        out_shape=[jax.ShapeDtypeStruct((N_TOKENS * ROW_SLABS, LANES), F32),
                   jax.ShapeDtypeStruct((META_ROWS, N_TOKENS), jnp.int32),
                   jax.ShapeDtypeStruct((BUCKET_ROWS, LANES), jnp.int32)],
        scratch_shapes=[pltpu.VMEM((BUCKET_ROWS, 1), F32)],
        compiler_params=pltpu.CompilerParams(dimension_semantics=("arbitrary",)),
        name="norm_route",
    )(x1p, x1s, mod_l, n2g, wr_hi, wr_lo, br)


def _const_spec(shape):
    nd = len(shape)
    return pl.BlockSpec(shape, lambda b: (0,) * nd, pipeline_mode=pl.Buffered(1))


def _stream_specs(latent):
    n = DEC_SEQ if latent else SEQ
    first = BATCH * SEQ // n if latent else 0
    own = pl.BlockSpec((n, D_MODEL), lambda b: (b, 0))
    rows = pl.BlockSpec((n * ROW_SLABS, LANES), lambda b: (first + b, 0))
    mod = pl.BlockSpec((None, 6, D_MODEL), (lambda b: (b + 1, 0, 0)) if latent else (lambda b: (0, 0, 0)))
    return n, own, rows, mod


def _mixer(x, mod_l, weights, *, latent, prev=None, kv_old=None, cache=None, bias=None, layer=0):
    n, own_spec, rows_spec, mod_spec = _stream_specs(latent)
    tok = x.shape[0]
    nb = tok // n
    in_specs = [own_spec, mod_spec]
    args = [x, mod_l]
    if prev is not None:
        in_specs += [rows_spec, mod_spec]
        args += list(prev)
        if not latent:
            in_specs += [pl.BlockSpec((None, layer, n, D_ATTN), lambda b: (b, 0, 0, 0))] * 2
            args += list(kv_old)
    if latent:
        cache_spec = pl.BlockSpec((None, None, PAST_LEN, D_ATTN), lambda b: (b, layer, 0, 0))
        in_specs += [cache_spec, cache_spec, _const_spec(bias.shape)]
        args += [cache[0], cache[1], bias]
    in_specs += [_const_spec(w.shape) for w in weights]
    args += list(weights)
    out_shape = [jax.ShapeDtypeStruct((tok, D_MODEL), F32)]
    out_specs = [own_spec]
    if not latent:
        out_shape += [jax.ShapeDtypeStruct((nb, layer + 1, n, D_ATTN), F32)] * 2
        out_specs += [pl.BlockSpec((None, layer + 1, n, D_ATTN), lambda b: (b, 0, 0, 0))] * 2
    scratch = [pltpu.VMEM((3 * N_PAIRS, n, LANES), BF16),
               pltpu.VMEM((3, n, D_POOL), F32),
               pltpu.VMEM((ROW_SLABS, n, LANES), BF16)]
    if latent:
        scratch += [pltpu.VMEM((N_PAIRS, PAST_LEN, LANES), BF16),
                    pltpu.VMEM((N_PAIRS, PAST_LEN, LANES), BF16)]
    return pl.pallas_call(
        functools.partial(_mixer_kernel, latent=latent, has_prev=prev is not None, n=n),
        grid=(nb,),
        in_specs=in_specs,
        out_specs=out_specs,
        out_shape=out_shape,
        scratch_shapes=scratch,
        compiler_params=pltpu.CompilerParams(dimension_semantics=("arbitrary",),
                                             vmem_limit_bytes=VMEM_LIMIT),
        name="mixer_latent" if latent else "mixer_context",
    )(*args)


def _bucket_tables():
    lo, hi = [], []
    for g in range(N_GROUPS):
        for a in range(GROUP_SIZE):
            for b in range(a + 1, GROUP_SIZE):
                lo.append(g * GROUP_SIZE + a)
                hi.append(g * GROUP_SIZE + b)
    return np.asarray(lo, np.int32), np.asarray(hi, np.int32)


def _route_plan(bucket, rank, count):
    tiles_b = (count + MOE_TM - 1) // MOE_TM
    tile_end = jnp.cumsum(tiles_b)
    tile_start = tile_end - tiles_b
    pos = tile_start[bucket] * MOE_TM + rank
    n_used = tile_end[-1]
    fill_lo = jnp.concatenate([tile_start * MOE_TM + count, (n_used * MOE_TM)[None]])
    fill_hi = jnp.concatenate([tile_end * MOE_TM, jnp.full((1,), MOE_TILES * MOE_TM, jnp.int32)])
    tile = jnp.arange(MOE_TILES, dtype=jnp.int32)
    tile_b = jnp.sum((tile[:, None] >= tile_end[None, :]).astype(jnp.int32), axis=1)
    last_b = jnp.sum((n_used - 1 >= tile_end).astype(jnp.int32))
    tile_b = jnp.minimum(tile_b, last_b)
    n_valid = jnp.clip(count[tile_b] - (tile - tile_start[tile_b]) * MOE_TM, 0, MOE_TM)
    lo_tab, hi_tab = _bucket_tables()
    i32 = lambda a: a.astype(jnp.int32)
    return (i32(pos), i32(fill_lo), i32(fill_hi)), (
        jnp.asarray(lo_tab)[tile_b], jnp.asarray(hi_tab)[tile_b], i32(n_valid), i32(n_used).reshape(1))


def _permute_kernel(pos_ref, lo_ref, hi_ref, h_hbm, hs_hbm, tok_ref, buf, lsem, ssem, fsem):
    i = pl.program_id(0)
    last = pl.num_programs(0) - 1
    half = i % 2
    blk = PERM_TM * ROW_SLABS

    def load(block, h):
        return pltpu.make_async_copy(h_hbm.at[pl.ds(block * blk, blk), :], buf.at[pl.ds(h * blk, blk), :], lsem.at[h])

    def wait_rows(sem, rows):
        n = rows * ROW_SLABS
        pltpu.make_async_copy(hs_hbm.at[pl.ds(0, n), :], hs_hbm.at[pl.ds(0, n), :], sem).wait()

    @pl.when(i == 0)
    def _():
        load(0, 0).start()

    load(i, half).wait()

    @pl.when(i >= 1)
    def _():
        wait_rows(ssem.at[1 - half], PERM_TM)

    @pl.when(i < last)
    def _():
        load(i + 1, 1 - half).start()

    base = i * PERM_TM
    for r in range(PERM_TM):
        p = pos_ref[base + r]
        pltpu.make_async_copy(_row_slabs(buf, half * PERM_TM + r), _row_slabs(hs_hbm, p),
                              ssem.at[half]).start(priority=r % 2)
        tok_ref[p] = base + r

    @pl.when(i == 0)
    def _():
        def fill(p, carry):
            pltpu.make_async_copy(_row_slabs(buf, 0), _row_slabs(hs_hbm, p), fsem).start()
            tok_ref[p] = 0
            return carry

        for b in range(N_BUCKETS + 1):
            lax.fori_loop(lo_ref[b], hi_ref[b], fill, 0)
        wait_rows(fsem, MOE_TILES * MOE_TM - N_TOKENS)

    @pl.when(i == last)
    def _():
        wait_rows(ssem.at[half], PERM_TM)


def _permute(h, pos, fill_lo, fill_hi):
    grid_spec = pltpu.PrefetchScalarGridSpec(
        num_scalar_prefetch=3,
        grid=(N_TOKENS // PERM_TM,),
        in_specs=[pl.BlockSpec(memory_space=pl.ANY)],
        out_specs=[pl.BlockSpec(memory_space=pl.ANY), pl.BlockSpec(memory_space=pltpu.SMEM)],
        scratch_shapes=[pltpu.VMEM((2 * PERM_TM * ROW_SLABS, LANES), F32),
                        pltpu.SemaphoreType.DMA((2,)),
                        pltpu.SemaphoreType.DMA((2,)),
                        pltpu.SemaphoreType.DMA(())],
    )
    return pl.pallas_call(
        _permute_kernel,
        grid_spec=grid_spec,
        out_shape=[jax.ShapeDtypeStruct((MOE_TILES * MOE_TM * ROW_SLABS, LANES), F32),
                   jax.ShapeDtypeStruct((MOE_TILES * MOE_TM,), jnp.int32)],
        compiler_params=pltpu.CompilerParams(dimension_semantics=("arbitrary",)),
        name="permute_rows",
    )(pos, fill_lo, fill_hi, h)


def _row_slabs(ref, row):
    return ref.at[pl.ds(pl.multiple_of(row * ROW_SLABS, ROW_SLABS), ROW_SLABS), :]


def _moe_kernel(e0_ref, e1_ref, nv_ref, nu_ref, tok_ref, hs_ref, wrt_ref, wg0_ref, wu0_ref, wd0_ref,
                wg1_ref, wu1_ref, wd1_ref, y_hbm, xb_scr, stg, ssem):
    i = pl.program_id(0)
    n_used = nu_ref[0]

    def wait_scatter(rows):
        n = rows * ROW_SLABS
        pltpu.make_async_copy(y_hbm.at[pl.ds(0, n), :], y_hbm.at[pl.ds(0, n), :], ssem).wait()

    @pl.when(i < n_used)
    def _():
        w0 = wrt_ref[pl.ds(e0_ref[i], 1), :]
        w1 = wrt_ref[pl.ds(e1_ref[i], 1), :]
        l0 = jnp.zeros((MOE_TM, LANES), F32)
        l1 = jnp.zeros((MOE_TM, LANES), F32)
        for j in range(ROW_SLABS):
            xj = hs_ref[_slab(j, MOE_TM), :]
            xb_scr[:, j * LANES:(j + 1) * LANES] = xj.astype(BF16)
            l0 = l0 + xj * w0[:, j * LANES:(j + 1) * LANES]
            l1 = l1 + xj * w1[:, j * LANES:(j + 1) * LANES]
        a0 = jax.nn.sigmoid(jnp.sum(l0, axis=1, keepdims=True))
        a1 = jax.nn.sigmoid(jnp.sum(l1, axis=1, keepdims=True))
        den = a0 + a1
        acc = None
        for gate, wg_ref, wu_ref, wd_ref in ((a0 / den, wg0_ref, wu0_ref, wd0_ref),
                                             (a1 / den, wg1_ref, wu1_ref, wd1_ref)):
            a = jnp.dot(xb_scr[...], wg_ref[...], preferred_element_type=F32)
            b = jnp.dot(xb_scr[...], wu_ref[...], preferred_element_type=F32)
            hid = (_silu(a) * b * gate).astype(BF16)
            o = jnp.dot(hid, wd_ref[...], preferred_element_type=F32)
            acc = o if acc is None else acc + o

        @pl.when(i >= 1)
        def _():
            wait_scatter(nv_ref[jnp.maximum(i - 1, 0)])

        _store_slabs(stg, acc, MOE_TM)
        base = i * MOE_TM

        def scatter_row(r, priority):
            pltpu.make_async_copy(_row_slabs(stg, r), _row_slabs(y_hbm, tok_ref[base + r]),
                                  ssem).start(priority=priority)

        def scatter_rows(c, carry):
            for u in range(DMA_UNROLL):
                scatter_row(c * DMA_UNROLL + u, u % 2)
            return carry

        def scatter_one(r, carry):
            scatter_row(r, 0)
            return carry

        n_full = nv_ref[i] // DMA_UNROLL
        lax.fori_loop(0, n_full, scatter_rows, 0)
        lax.fori_loop(n_full * DMA_UNROLL, nv_ref[i], scatter_one, 0)

        @pl.when(i == n_used - 1)
        def _():
            wait_scatter(nv_ref[i])


def _moe(hs, row_tok, tiles, wrt, wg, wu, wd, layer):
    e0, e1, n_valid, n_used = tiles

    def w_spec(which, shape):
        return pl.BlockSpec((None, None) + shape,
                            lambda i, e0, e1, nv, nu, tok: (layer, (e0, e1)[which][i], 0, 0))

    up_shape, down_shape = (D_MODEL, D_FF), (D_FF, D_MODEL)
    grid_spec = pltpu.PrefetchScalarGridSpec(
        num_scalar_prefetch=5,
        grid=(MOE_TILES,),
        in_specs=[pl.BlockSpec((MOE_TM * ROW_SLABS, LANES),
                               lambda i, e0, e1, nv, nu, tok: (jnp.minimum(i, nu[0] - 1), 0)),
                  pl.BlockSpec((N_EXPERTS, D_MODEL), lambda i, *_: (0, 0)),
                  w_spec(0, up_shape), w_spec(0, up_shape), w_spec(0, down_shape),
                  w_spec(1, up_shape), w_spec(1, up_shape), w_spec(1, down_shape)],
        out_specs=pl.BlockSpec(memory_space=pl.ANY),
        scratch_shapes=[pltpu.VMEM((MOE_TM, D_MODEL), BF16),
                        pltpu.VMEM((MOE_TM * ROW_SLABS, LANES), F32),
                        pltpu.SemaphoreType.DMA(())],
    )
    return pl.pallas_call(
        _moe_kernel,
        grid_spec=grid_spec,
        out_shape=jax.ShapeDtypeStruct((N_TOKENS * ROW_SLABS, LANES), F32),
        compiler_params=pltpu.CompilerParams(dimension_semantics=("arbitrary",),
                                             vmem_limit_bytes=VMEM_LIMIT),
        name="moe_routed",
    )(e0, e1, n_valid, n_used, row_tok, hs, wrt, wg, wu, wd, wg, wu, wd)


def _final_kernel(x1_ref, y_ref, mod_ref, fg_ref, o_ref):
    x2 = x1_ref[...] + mod_ref[5:6, :] * _rows_to_2d(y_ref, x1_ref.shape[0])
    o_ref[...] = _rms(x2, fg_ref[...])


def _final(x1, y, mod_l, fg, *, latent):
    n, own_spec, rows_spec, mod_spec = _stream_specs(latent)
    return pl.pallas_call(
        _final_kernel,
        grid=(x1.shape[0] // n,),
        in_specs=[own_spec, rows_spec, mod_spec, pl.BlockSpec((1, D_MODEL), lambda b: (0, 0))],
        out_specs=own_spec,
        out_shape=jax.ShapeDtypeStruct(x1.shape, F32),
        name="final_norm",
    )(x1, y, mod_l, fg)


def kernel(x_prompt, x_sample, cache_k, cache_v, c, c_ctx, w_ada, b_ada, norm1_g, norm2_g, w_in, w_out, rpb, pool_w, pool_scale, conv_w, conv_b, conv_ln_g, conv_ln_b, conv_pw_w, conv_pw_b, w_router, b_router, moe_w_gate, moe_w_up, moe_w_down, final_g):
    xp = x_prompt.reshape(BATCH * SEQ, D_MODEL)
    xs = x_sample.reshape(DEC_BATCH * DEC_SEQ, D_MODEL)
    cond = jnp.concatenate([c_ctx[None, :], c, jnp.zeros((N_MOD_ROWS - 1 - DEC_BATCH, D_MODEL), F32)], axis=0)
    mod = _ada_modulation(cond, w_ada, b_ada).reshape(DEPTH, N_MOD_ROWS, 6, D_MODEL)
    bias = _rpb_tiles(rpb)
    ck = cache_k.reshape(DEC_BATCH, DEPTH, PAST_LEN, D_ATTN)
    cv = cache_v.reshape(DEC_BATCH, DEPTH, PAST_LEN, D_ATTN)
    wrt = w_router.T
    wr = jnp.pad(w_router, ((0, 0), (0, LANES - N_EXPERTS)))
    wr_hi = wr.astype(BF16)
    wr_lo = (wr - wr_hi.astype(F32)).astype(BF16)
    br = b_router.reshape(N_EXPERTS, 1)
    fg = final_g.reshape(1, D_MODEL)
    wg, wu, wd = moe_w_gate.astype(BF16), moe_w_up.astype(BF16), moe_w_down.astype(BF16)
    new_k = new_v = None
    prev = None
    for l in range(DEPTH):
        pool_bd = jnp.zeros((D_POOL, D_POOL), F32)
        gd = D_POOL // 4
        for g in range(4):
            pool_bd = pool_bd.at[g * gd:(g + 1) * gd, g * gd:(g + 1) * gd].set(pool_w[l, g])
        weights = [norm1_g[l].reshape(1, -1),
                   w_in[l].astype(BF16), w_out[l].astype(BF16),
                   pool_bd.astype(BF16), pool_scale[l].reshape(1, -1),
                   conv_w[l], conv_b[l].reshape(1, -1),
                   conv_ln_g[l].reshape(1, -1), conv_ln_b[l].reshape(1, -1),
                   conv_pw_w[l].astype(BF16), conv_pw_b[l].reshape(1, -1)]

        xp, new_k, new_v = _mixer(xp, mod[l], weights, latent=False, prev=prev, kv_old=(new_k, new_v), layer=l)
        xs, = _mixer(xs, mod[l], weights, latent=True, prev=prev, cache=(ck, cv), bias=bias[l], layer=l)

        h, meta, count = _route(xp, xs, mod[l], norm2_g[l].reshape(1, -1), wr_hi, wr_lo, br)
        perm, tiles = _route_plan(meta[0], meta[1], count[:N_BUCKETS, 0])
        hs, row_tok = _permute(h, *perm)
        y = _moe(hs, row_tok, tiles, wrt, wg, wu, wd, l)
        prev = (y, mod[l])

    y_prompt = _final(xp, y, mod[DEPTH - 1], fg, latent=False).reshape(BATCH, SEQ, D_MODEL)
    y_sample = _final(xs, y, mod[DEPTH - 1], fg, latent=True).reshape(DEC_BATCH, DEC_SEQ, D_MODEL)
    new_cache_k = new_k.reshape(BATCH, DEPTH, SEQ, N_HEADS, HEAD_DIM)
    new_cache_v = new_v.reshape(BATCH, DEPTH, SEQ, N_HEADS, HEAD_DIM)
    return (y_prompt, y_sample, new_cache_k, new_cache_v)
```

```python
import functools

import numpy as np
import jax
import jax.numpy as jnp
from jax import lax
from jax.experimental import pallas as pl
from jax.experimental.pallas import tpu as pltpu

D_MODEL = 1024
BATCH = 32
SEQ = 256
DEPTH = 2
DEC_BATCH = 8
DEC_SEQ = 1024
PAST_LEN = 512
GRID_W = 64
GRID_ROWS = DEC_SEQ // GRID_W
D_ATTN = 512
HEAD_DIM = 64
N_HEADS = 8
N_PAIRS = N_HEADS // 2
WIN_ROWS = 8
WIN_COLS = 16
D_POOL = 256
D_CONV = 256
CONV_WIDTH = 31
D_IN = 3 * D_ATTN + D_POOL + 2 * D_CONV
N_EXPERTS = 16
N_GROUPS = 4
GROUP_SIZE = N_EXPERTS // N_GROUPS
D_FF = 512
EPS = 1e-6
N_MOD_ROWS = 16
N_DR = 2 * WIN_ROWS - 1
N_DC = 2 * WIN_COLS - 1
N_BIAS_TILES = N_DR - 1
LAT_GROUP_ROWS = 4
SEQ_PAD = 16
LANES = 128
ROW_SLABS = D_MODEL // LANES
N_TOKENS = BATCH * SEQ + DEC_BATCH * DEC_SEQ
PAIRS_PER_GROUP = GROUP_SIZE * (GROUP_SIZE - 1) // 2
N_BUCKETS = N_GROUPS * PAIRS_PER_GROUP
BUCKET_ROWS = 32
META_ROWS = 8
ROUTE_TM = 512
CTX_ROUTE_BLOCKS = BATCH * SEQ // ROUTE_TM
MOE_TM = 256
MOE_TILES = N_TOKENS // MOE_TM + N_BUCKETS
PERM_TM = 512
DMA_UNROLL = 8
VMEM_LIMIT = 60 * 1024 * 1024

F32 = jnp.float32
BF16 = jnp.bfloat16
HIGHEST = lax.Precision.HIGHEST
NT_DIMS = (((1,), (1,)), ((), ()))


def _rms(x, g):
    return x * lax.rsqrt(jnp.mean(x * x, axis=-1, keepdims=True) + EPS) * g


def _silu(x):
    return x * jax.nn.sigmoid(x)


def _ada_kernel(cond_ref, w_ref, b_ref, o_ref):
    o_ref[...] = jnp.dot(_silu(cond_ref[...]), w_ref[...], preferred_element_type=F32,
                         precision=HIGHEST) + b_ref[...]


def _ada_modulation(cond, w_ada, b_ada):
    n_col = 6 * D_MODEL // D_MODEL
    return pl.pallas_call(
        _ada_kernel,
        grid=(DEPTH, n_col),
        in_specs=[
            pl.BlockSpec((N_MOD_ROWS, D_MODEL), lambda l, n: (0, 0)),
            pl.BlockSpec((None, D_MODEL, D_MODEL), lambda l, n: (l, 0, n)),
            pl.BlockSpec((None, 1, D_MODEL), lambda l, n: (l, 0, n)),
        ],
        out_specs=pl.BlockSpec((None, N_MOD_ROWS, D_MODEL), lambda l, n: (l, 0, n)),
        out_shape=jax.ShapeDtypeStruct((DEPTH, N_MOD_ROWS, 6 * D_MODEL), F32),
        name="ada_modulation",
    )(cond, w_ada, b_ada.reshape(DEPTH, 1, 6 * D_MODEL))


def _rpb_kernel(rpb_ref, o_ref):
    base = (pl.program_id(0) * N_HEADS + pl.program_id(1)) * (N_DR * N_DC)
    cq = lax.broadcasted_iota(jnp.int32, (GRID_W, LANES), 0)
    lane = lax.broadcasted_iota(jnp.int32, (GRID_W, LANES), 1)
    ck = lane & (GRID_W - 1)
    right = lane >= GRID_W
    dcm = jnp.clip(ck - cq, -(WIN_COLS - 1), WIN_COLS - 1) + (WIN_COLS - 1)
    cs = jnp.clip(cq - WIN_COLS // 2, 0, GRID_W - WIN_COLS)
    valid = (ck >= cs) & (ck < cs + WIN_COLS)
    for d in range(N_BIAS_TILES):
        acc = jnp.zeros((GRID_W, LANES), F32)
        for dc in range(N_DC):
            v = jnp.where(right, rpb_ref[base + (d + 1) * N_DC + dc], rpb_ref[base + d * N_DC + dc])
            acc = jnp.where(dcm == dc, v, acc)
        o_ref[d] = jnp.where(valid, acc, -jnp.inf)


def _rpb_tiles(rpb):
    return pl.pallas_call(
        _rpb_kernel,
        grid=(DEPTH, N_HEADS),
        in_specs=[pl.BlockSpec(memory_space=pltpu.SMEM)],
        out_specs=pl.BlockSpec((None, None, N_BIAS_TILES, GRID_W, LANES), lambda l, h: (l, h, 0, 0, 0)),
        out_shape=jax.ShapeDtypeStruct((DEPTH, N_HEADS, N_BIAS_TILES, GRID_W, LANES), F32),
        name="rpb_tiles",
    )(rpb.reshape(-1))


def _shift_rows(a, s):
    return a if s == 0 else pltpu.roll(a, s % a.shape[0], 0)


def _pad_rows(a):
    return jnp.concatenate([a, jnp.zeros((SEQ_PAD, a.shape[1]), a.dtype)], axis=0)


def _softmax_parts(s):
    m = jnp.max(s, axis=-1, keepdims=True)
    e = jnp.exp(s - m)
    return e, jnp.sum(e, axis=-1, keepdims=True)


def _context_attention(qkv_scr, m_scr, n):
    left = lax.broadcasted_iota(jnp.int32, (n, LANES), 1) < HEAD_DIM
    for p in range(N_PAIRS):
        q = qkv_scr[p]
        k = qkv_scr[N_PAIRS + p]
        v = qkv_scr[2 * N_PAIRS + p]
        zero = jnp.zeros_like(q)
        outs = []
        for qh in (jnp.where(left, q, zero), jnp.where(left, zero, q)):
            s = lax.dot_general(qh, k, NT_DIMS, preferred_element_type=F32)
            e, l = _softmax_parts(s)
            outs.append(jnp.dot(e.astype(BF16), v, preferred_element_type=F32) / l)
        m_scr[p] = jnp.where(left, outs[0], outs[1]).astype(BF16)


def _window_start(qr):
    return min(max(qr - WIN_ROWS // 2, 0), GRID_ROWS - WIN_ROWS)


def _latent_groups():
    groups = []
    for r0 in range(0, GRID_ROWS, LAT_GROUP_ROWS):
        k0 = _window_start(r0)
        k1 = _window_start(r0 + LAT_GROUP_ROWS - 1) + WIN_ROWS
        nk = k1 - k0 + (k1 - k0) % 2
        k0 = min(k0, GRID_ROWS - nk)
        groups.append((r0, LAT_GROUP_ROWS, k0, nk))
    return groups


def _group_bias(bias_ref, p, r0, nq, k0, nk):
    left = lax.broadcasted_iota(jnp.int32, (GRID_W, LANES), 1) < HEAD_DIM
    neg = jnp.full((GRID_W, LANES), -jnp.inf, F32)
    blocks = []
    for hh in range(2):
        for qr in range(r0, r0 + nq):
            rs = _window_start(qr)
            pieces = []
            for kr in range(k0, k0 + nk, 2):
                in0 = rs <= kr < rs + WIN_ROWS
                in1 = rs <= kr + 1 < rs + WIN_ROWS
                d = kr - qr + (WIN_ROWS - 1)
                if in0 and in1:
                    pieces.append(bias_ref[2 * p + hh, d])
                elif in0:
                    pieces.append(jnp.where(left, bias_ref[2 * p + hh, d], neg))
                elif in1:
                    pieces.append(jnp.where(left, neg, bias_ref[2 * p + hh, d]))
                else:
                    pieces.append(neg)
            blocks.append(jnp.concatenate(pieces, axis=1))
    return jnp.concatenate(blocks, axis=0)


def _latent_attention(qkv_scr, kc_scr, vc_scr, bias_ref, m_scr):
    def body(p, carry):
        for r0, nq, k0, nk in _latent_groups():
            rows = nq * GRID_W
            left = lax.broadcasted_iota(jnp.int32, (rows, LANES), 1) < HEAD_DIM
            q = qkv_scr[p, r0 * GRID_W:r0 * GRID_W + rows, :]
            zero = jnp.zeros_like(q)
            qm = jnp.concatenate([jnp.where(left, q, zero), jnp.where(left, zero, q)], axis=0)
            kl = qkv_scr[N_PAIRS + p, k0 * GRID_W:(k0 + nk) * GRID_W, :]
            vl = qkv_scr[2 * N_PAIRS + p, k0 * GRID_W:(k0 + nk) * GRID_W, :]
            s_loc = (lax.dot_general(qm, kl, NT_DIMS, preferred_element_type=F32)
                     + _group_bias(bias_ref, p, r0, nq, k0, nk))
            s_ctx = lax.dot_general(qm, kc_scr[p], NT_DIMS, preferred_element_type=F32)
            m = jnp.maximum(jnp.max(s_loc, axis=-1, keepdims=True), jnp.max(s_ctx, axis=-1, keepdims=True))
            e_loc = jnp.exp(s_loc - m)
            e_ctx = jnp.exp(s_ctx - m)
            l = jnp.sum(e_loc, axis=-1, keepdims=True) + jnp.sum(e_ctx, axis=-1, keepdims=True)
            o = (jnp.dot(e_loc.astype(BF16), vl, preferred_element_type=F32)
                 + jnp.dot(e_ctx.astype(BF16), vc_scr[p], preferred_element_type=F32)) / l
            m_scr[p, r0 * GRID_W:r0 * GRID_W + rows, :] = jnp.where(left, o[:rows], o[rows:]).astype(BF16)
        return carry

    lax.fori_loop(0, N_PAIRS, body, 0)


def _router_bucket(h2, wr_hi_ref, wr_lo_ref, br_ref, n):
    h_hi = h2.astype(BF16)
    h_lo = (h2 - h_hi.astype(F32)).astype(BF16)
    logits = (jnp.dot(h_hi, wr_hi_ref[...], preferred_element_type=F32)
              + (jnp.dot(h_lo, wr_hi_ref[...], preferred_element_type=F32)
                 + jnp.dot(h_hi, wr_lo_ref[...], preferred_element_type=F32)))
    sel = jax.nn.sigmoid(logits.T[:N_EXPERTS]) + br_ref[...]
    rows = [sel[e:e + 1, :] for e in range(N_EXPERTS)]
    gscore = []
    for g in range(N_GROUPS):
        r4 = rows[g * GROUP_SIZE:(g + 1) * GROUP_SIZE]
        best = None
        for i in range(GROUP_SIZE):
            for j in range(i + 1, GROUP_SIZE):
                s = r4[i] + r4[j]
                best = s if best is None else jnp.maximum(best, s)
        gscore.append(best)
    gbest = jnp.zeros((1, n), jnp.int32)
    gval = gscore[0]
    for g in range(1, N_GROUPS):
        upd = gscore[g] > gval
        gbest = jnp.where(upd, g, gbest)
        gval = jnp.where(upd, gscore[g], gval)
    picked = []
    for e in range(N_EXPERTS):
        g = e // GROUP_SIZE
        rank = jnp.zeros((1, n), jnp.int32)
        for j in range(g * GROUP_SIZE, (g + 1) * GROUP_SIZE):
            if j == e:
                continue
            ahead = (rows[j] > rows[e]) | (rows[j] == rows[e]) if j < e else rows[j] > rows[e]
            rank = rank + ahead.astype(jnp.int32)
        picked.append((rank < 2) & (gbest == g))
    e_lo = jnp.zeros((1, n), jnp.int32)
    e_hi = jnp.zeros((1, n), jnp.int32)
    for e in range(N_EXPERTS):
        e_hi = jnp.where(picked[e], e, e_hi)
        e_lo = jnp.where(picked[N_EXPERTS - 1 - e], N_EXPERTS - 1 - e, e_lo)
    a = e_lo - GROUP_SIZE * gbest
    b = e_hi - GROUP_SIZE * gbest
    return gbest * PAIRS_PER_GROUP + jnp.right_shift(a * (2 * GROUP_SIZE - 1 - a), 1) + (b - a - 1)


def _slab(j, n):
    return pl.ds(j, n, stride=ROW_SLABS)


def _rows_to_2d(ref, n):
    return jnp.concatenate([ref[_slab(j, n), :] for j in range(ROW_SLABS)], axis=1)


def _store_slabs(ref, val, n, first_row=0):
    for j in range(ROW_SLABS):
        ref[_slab(first_row + j, n), :] = val[:, j * LANES:(j + 1) * LANES]


def _mixer_kernel(*refs, latent, has_prev, n):
    it = iter(refs)
    x_ref, mod_ref = next(it), next(it)
    if has_prev:
        y_ref, modp_ref = next(it), next(it)
        if not latent:
            kold_ref, vold_ref = next(it), next(it)
    if latent:
        ck_ref, cv_ref, bias_ref = next(it), next(it), next(it)
    (n1g_ref, w_in_ref, w_out_ref, poolw_ref, pscale_ref, convw_ref, convb_ref,
     lng_ref, lnb_ref, pww_ref, pwb_ref) = [next(it) for _ in range(11)]
    x1_ref = next(it)
    if not latent:
        knew_ref, vnew_ref = next(it), next(it)
    qkv_scr, u_scr, m_scr = next(it), next(it), next(it)
    if latent:
        kc_scr, vc_scr = next(it), next(it)

    x = x_ref[...]
    if has_prev:
        x = x + modp_ref[5:6, :] * _rows_to_2d(y_ref, n)
    layer = 0
    if has_prev and not latent:
        layer = kold_ref.shape[0]
        knew_ref[0:layer] = kold_ref[...]
        vnew_ref[0:layer] = vold_ref[...]
    sh1, sc1, g1 = mod_ref[0:1, :], mod_ref[1:2, :], mod_ref[2:3, :]

    hb = (_rms(x, n1g_ref[...]) * (1.0 + sc1) + sh1).astype(BF16)
    cw = 2 * LANES
    for j in range(D_IN // cw):
        zj = jnp.dot(hb, w_in_ref[:, j * cw:(j + 1) * cw], preferred_element_type=F32)
        if j < 6:
            part = j // 2
            zs = zj * (HEAD_DIM ** -0.5) if part == 0 else zj
            c = part * N_PAIRS + 2 * (j % 2)
            qkv_scr[c] = zs[:, :LANES].astype(BF16)
            qkv_scr[c + 1] = zs[:, LANES:].astype(BF16)
            if not latent and part == 1:
                knew_ref[layer, :, (j % 2) * cw:(j % 2 + 1) * cw] = zj
            if not latent and part == 2:
                vnew_ref[layer, :, (j % 2) * cw:(j % 2 + 1) * cw] = zj
        else:
            u_scr[j - 6] = zj

    if latent:
        for p in range(N_PAIRS):
            kc_scr[p] = ck_ref[:, p * LANES:(p + 1) * LANES].astype(BF16)
            vc_scr[p] = cv_ref[:, p * LANES:(p + 1) * LANES].astype(BF16)
        _latent_attention(qkv_scr, kc_scr, vc_scr, bias_ref, m_scr)
    else:
        _context_attention(qkv_scr, m_scr, n)

    row = lax.broadcasted_iota(jnp.int32, (n, D_POOL), 0)
    grp = lax.broadcasted_iota(jnp.int32, (n, D_POOL), 1) // (D_POOL // 4)
    up = u_scr[0]
    t2 = _pad_rows(up)
    t2 = t2 + _shift_rows(t2, 1)
    t4 = t2 + _shift_rows(t2, 2)
    t8 = t4 + _shift_rows(t4, 4)
    t16 = t8 + _shift_rows(t8, 8)
    s2, s4, s8, s16 = t2[:n], _shift_rows(t4, -1)[:n], _shift_rows(t8, -3)[:n], _shift_rows(t16, -7)[:n]
    wsum = jnp.where(grp == 0, s2, jnp.where(grp == 1, s4, jnp.where(grp == 2, s8, s16)))
    half = jnp.left_shift(1, grp)
    lo = jnp.maximum(row - half, 0)
    hi = jnp.minimum(row + half - 1, n - 1)
    diff = wsum / (hi - lo + 1).astype(F32) - up
    pooled = jnp.dot(diff.astype(BF16), poolw_ref[...], preferred_element_type=F32) * pscale_ref[...]
    m_scr[N_PAIRS] = pooled[:, :LANES].astype(BF16)
    m_scr[N_PAIRS + 1] = pooled[:, LANES:].astype(BF16)

    y = _pad_rows(u_scr[1] * jax.nn.sigmoid(u_scr[2]))
    acc = jnp.zeros(y.shape, F32) + convb_ref[...]
    for k in range(CONV_WIDTH):
        acc = acc + convw_ref[k:k + 1, :] * _shift_rows(y, CONV_WIDTH // 2 - k)
    acc = acc[:n]
    mu = jnp.mean(acc, axis=-1, keepdims=True)
    cen = acc - mu
    var = jnp.mean(cen * cen, axis=-1, keepdims=True)
    yn = _silu(cen * lax.rsqrt(var + EPS) * lng_ref[...] + lnb_ref[...])
    conv = jnp.dot(yn.astype(BF16), pww_ref[...], preferred_element_type=F32) + pwb_ref[...]
    m_scr[N_PAIRS + 2] = conv[:, :LANES].astype(BF16)
    m_scr[N_PAIRS + 3] = conv[:, LANES:].astype(BF16)

    mixed = jnp.concatenate([m_scr[j] for j in range(ROW_SLABS)], axis=1)
    x1_ref[...] = x + g1 * jnp.dot(mixed, w_out_ref[...], preferred_element_type=F32)


def _route_kernel(xp_ref, xs_ref, mod_ref, n2g_ref, wr_hi_ref, wr_lo_ref, br_ref,
                  h_ref, meta_ref, count_ref, run_scr):
    i = pl.program_id(0)
    n = ROUTE_TM

    @pl.when(i == 0)
    def _():
        run_scr[...] = jnp.zeros_like(run_scr)

    x1 = lax.cond(i < CTX_ROUTE_BLOCKS, lambda: xp_ref[...], lambda: xs_ref[...])
    h2 = _rms(x1, n2g_ref[...]) * (1.0 + mod_ref[4:5, :]) + mod_ref[3:4, :]
    _store_slabs(h_ref, h2, n)

    bucket = _router_bucket(h2, wr_hi_ref, wr_lo_ref, br_ref, n)
    onehot = lax.broadcasted_iota(jnp.int32, (BUCKET_ROWS, n), 0) == bucket
    upper = (lax.broadcasted_iota(jnp.int32, (n, n), 0) <= lax.broadcasted_iota(jnp.int32, (n, n), 1))
    incl = jnp.dot(onehot.astype(BF16), upper.astype(BF16), preferred_element_type=F32)
    run = run_scr[...]
    rank = jnp.sum(jnp.where(onehot, incl + run, 0.0), axis=0, keepdims=True) - 1.0
    run = run + jnp.sum(onehot.astype(F32), axis=1, keepdims=True)
    run_scr[...] = run
    meta_ref[...] = jnp.concatenate(
        [bucket, rank.astype(jnp.int32), jnp.zeros((META_ROWS - 2, n), jnp.int32)], axis=0)
    count_ref[...] = jnp.broadcast_to(run, count_ref.shape).astype(jnp.int32)


def _route(x1p, x1s, mod_l, n2g, wr_hi, wr_lo, br):
    per_seq = DEC_SEQ // ROUTE_TM
    last_ctx = CTX_ROUTE_BLOCKS - 1
    return pl.pallas_call(
        _route_kernel,
        grid=(N_TOKENS // ROUTE_TM,),
        in_specs=[
            pl.BlockSpec((ROUTE_TM, D_MODEL), lambda i: (jnp.minimum(i, last_ctx), 0)),
            pl.BlockSpec((ROUTE_TM, D_MODEL), lambda i: (jnp.maximum(i - CTX_ROUTE_BLOCKS, 0), 0)),
            pl.BlockSpec((None, 6, D_MODEL),
                         lambda i: (jnp.where(i < CTX_ROUTE_BLOCKS, 0, (i - CTX_ROUTE_BLOCKS) // per_seq + 1), 0, 0)),
            pl.BlockSpec((1, D_MODEL), lambda i: (0, 0)),
            pl.BlockSpec((D_MODEL, LANES), lambda i: (0, 0)),
            pl.BlockSpec((D_MODEL, LANES), lambda i: (0, 0)),
            pl.BlockSpec((N_EXPERTS, 1), lambda i: (0, 0)),
        ],
        out_specs=[pl.BlockSpec((ROUTE_TM * ROW_SLABS, LANES), lambda i: (i, 0)),
                   pl.BlockSpec((META_ROWS, ROUTE_TM), lambda i: (0, i)),
                   pl.BlockSpec((BUCKET_ROWS, LANES), lambda i: (0, 0))],
        out_shape=[jax.ShapeDtypeStruct((N_TOKENS * ROW_SLABS, LANES), F32),
                   jax.ShapeDtypeStruct((META_ROWS, N_TOKENS), jnp.int32),
                   jax.ShapeDtypeStruct((BUCKET_ROWS, LANES), jnp.int32)],
        scratch_shapes=[pltpu.VMEM((BUCKET_ROWS, 1), F32)],
        compiler_params=pltpu.CompilerParams(dimension_semantics=("arbitrary",)),
        name="norm_route",
    )(x1p, x1s, mod_l, n2g, wr_hi, wr_lo, br)


def _const_spec(shape):
    nd = len(shape)
    return pl.BlockSpec(shape, lambda b: (0,) * nd, pipeline_mode=pl.Buffered(1))


def _stream_specs(latent, single_buffer_rows=False):
    n = DEC_SEQ if latent else SEQ
    first = BATCH * SEQ // n if latent else 0
    own = pl.BlockSpec((n, D_MODEL), lambda b: (b, 0))
    rows = pl.BlockSpec((n * ROW_SLABS, LANES), lambda b: (first + b, 0),
                        pipeline_mode=pl.Buffered(1) if single_buffer_rows else None)
    mod = pl.BlockSpec((None, 6, D_MODEL), (lambda b: (b + 1, 0, 0)) if latent else (lambda b: (0, 0, 0)))
    return n, own, rows, mod


def _mixer(x, mod_l, weights, *, latent, prev=None, kv_old=None, cache=None, bias=None, layer=0):
    n, own_spec, rows_spec, mod_spec = _stream_specs(latent, single_buffer_rows=latent)
    tok = x.shape[0]
    nb = tok // n
    in_specs = [own_spec, mod_spec]
    args = [x, mod_l]
    if prev is not None:
        in_specs += [rows_spec, mod_spec]
        args += list(prev)
        if not latent:
            in_specs += [pl.BlockSpec((None, layer, n, D_ATTN), lambda b: (b, 0, 0, 0))] * 2
            args += list(kv_old)
    if latent:
        cache_spec = pl.BlockSpec((None, None, PAST_LEN, D_ATTN), lambda b: (b, layer, 0, 0),
                                  pipeline_mode=pl.Buffered(1))
        in_specs += [cache_spec, cache_spec, _const_spec(bias.shape)]
        args += [cache[0], cache[1], bias]
    in_specs += [_const_spec(w.shape) for w in weights]
    args += list(weights)
    out_shape = [jax.ShapeDtypeStruct((tok, D_MODEL), F32)]
    out_specs = [own_spec]
    if not latent:
        out_shape += [jax.ShapeDtypeStruct((nb, layer + 1, n, D_ATTN), F32)] * 2
        out_specs += [pl.BlockSpec((None, layer + 1, n, D_ATTN), lambda b: (b, 0, 0, 0))] * 2
    scratch = [pltpu.VMEM((3 * N_PAIRS, n, LANES), BF16),
               pltpu.VMEM((3, n, D_POOL), F32),
               pltpu.VMEM((ROW_SLABS, n, LANES), BF16)]
    if latent:
        scratch += [pltpu.VMEM((N_PAIRS, PAST_LEN, LANES), BF16),
                    pltpu.VMEM((N_PAIRS, PAST_LEN, LANES), BF16)]
    return pl.pallas_call(
        functools.partial(_mixer_kernel, latent=latent, has_prev=prev is not None, n=n),
        grid=(nb,),
        in_specs=in_specs,
        out_specs=out_specs,
        out_shape=out_shape,
        scratch_shapes=scratch,
        compiler_params=pltpu.CompilerParams(dimension_semantics=("arbitrary",),
                                             vmem_limit_bytes=VMEM_LIMIT),
        name="mixer_latent" if latent else "mixer_context",
    )(*args)


def _bucket_tables():
    lo, hi = [], []
    for g in range(N_GROUPS):
        for a in range(GROUP_SIZE):
            for b in range(a + 1, GROUP_SIZE):
                lo.append(g * GROUP_SIZE + a)
                hi.append(g * GROUP_SIZE + b)
    return np.asarray(lo, np.int32), np.asarray(hi, np.int32)


def _route_plan(bucket, rank, count):
    tiles_b = (count + MOE_TM - 1) // MOE_TM
    tile_end = jnp.cumsum(tiles_b)
    tile_start = tile_end - tiles_b
    pos = tile_start[bucket] * MOE_TM + rank
    n_used = tile_end[-1]
    fill_lo = jnp.concatenate([tile_start * MOE_TM + count, (n_used * MOE_TM)[None]])
    fill_hi = jnp.concatenate([tile_end * MOE_TM, jnp.full((1,), MOE_TILES * MOE_TM, jnp.int32)])
    tile = jnp.arange(MOE_TILES, dtype=jnp.int32)
    tile_b = jnp.sum((tile[:, None] >= tile_end[None, :]).astype(jnp.int32), axis=1)
    last_b = jnp.sum((n_used - 1 >= tile_end).astype(jnp.int32))
    tile_b = jnp.minimum(tile_b, last_b)
    n_valid = jnp.clip(count[tile_b] - (tile - tile_start[tile_b]) * MOE_TM, 0, MOE_TM)
    lo_tab, hi_tab = _bucket_tables()
    i32 = lambda a: a.astype(jnp.int32)
    return (i32(pos), i32(fill_lo), i32(fill_hi)), (
        jnp.asarray(lo_tab)[tile_b], jnp.asarray(hi_tab)[tile_b], i32(n_valid), i32(n_used).reshape(1))


def _permute_kernel(pos_ref, lo_ref, hi_ref, h_hbm, hs_hbm, tok_ref, buf, lsem, ssem, fsem):
    i = pl.program_id(0)
    last = pl.num_programs(0) - 1
    half = i % 2
    blk = PERM_TM * ROW_SLABS

    def load(block, h):
        return pltpu.make_async_copy(h_hbm.at[pl.ds(block * blk, blk), :], buf.at[pl.ds(h * blk, blk), :], lsem.at[h])

    def wait_rows(sem, rows):
        n = rows * ROW_SLABS
        pltpu.make_async_copy(hs_hbm.at[pl.ds(0, n), :], hs_hbm.at[pl.ds(0, n), :], sem).wait()

    @pl.when(i == 0)
    def _():
        load(0, 0).start()

    load(i, half).wait()

    @pl.when(i >= 1)
    def _():
        wait_rows(ssem.at[1 - half], PERM_TM)

    @pl.when(i < last)
    def _():
        load(i + 1, 1 - half).start()

    base = i * PERM_TM
    for r in range(PERM_TM):
        p = pos_ref[base + r]
        pltpu.make_async_copy(_row_slabs(buf, half * PERM_TM + r), _row_slabs(hs_hbm, p),
                              ssem.at[half]).start(priority=r % 2)
        tok_ref[p] = base + r

    @pl.when(i == 0)
    def _():
        def fill(p, carry):
            pltpu.make_async_copy(_row_slabs(buf, 0), _row_slabs(hs_hbm, p), fsem).start()
            tok_ref[p] = 0
            return carry

        for b in range(N_BUCKETS + 1):
            lax.fori_loop(lo_ref[b], hi_ref[b], fill, 0)
        wait_rows(fsem, MOE_TILES * MOE_TM - N_TOKENS)

    @pl.when(i == last)
    def _():
        wait_rows(ssem.at[half], PERM_TM)


def _permute(h, pos, fill_lo, fill_hi):
    grid_spec = pltpu.PrefetchScalarGridSpec(
        num_scalar_prefetch=3,
        grid=(N_TOKENS // PERM_TM,),
        in_specs=[pl.BlockSpec(memory_space=pl.ANY)],
        out_specs=[pl.BlockSpec(memory_space=pl.ANY), pl.BlockSpec(memory_space=pltpu.SMEM)],
        scratch_shapes=[pltpu.VMEM((2 * PERM_TM * ROW_SLABS, LANES), F32),
                        pltpu.SemaphoreType.DMA((2,)),
                        pltpu.SemaphoreType.DMA((2,)),
                        pltpu.SemaphoreType.DMA(())],
    )
    return pl.pallas_call(
        _permute_kernel,
        grid_spec=grid_spec,
        out_shape=[jax.ShapeDtypeStruct((MOE_TILES * MOE_TM * ROW_SLABS, LANES), F32),
                   jax.ShapeDtypeStruct((MOE_TILES * MOE_TM,), jnp.int32)],
        compiler_params=pltpu.CompilerParams(dimension_semantics=("arbitrary",)),
        name="permute_rows",
    )(pos, fill_lo, fill_hi, h)


def _row_slabs(ref, row):
    return ref.at[pl.ds(pl.multiple_of(row * ROW_SLABS, ROW_SLABS), ROW_SLABS), :]


def _moe_kernel(e0_ref, e1_ref, nv_ref, nu_ref, tok_ref, hs_ref, wrt_ref, wg0_ref, wu0_ref, wd0_ref,
                wg1_ref, wu1_ref, wd1_ref, y_hbm, xb_scr, stg, ssem):
    i = pl.program_id(0)
    n_used = nu_ref[0]

    def wait_scatter(rows):
        n = rows * ROW_SLABS
        pltpu.make_async_copy(y_hbm.at[pl.ds(0, n), :], y_hbm.at[pl.ds(0, n), :], ssem).wait()

    @pl.when(i < n_used)
    def _():
        w0 = wrt_ref[pl.ds(e0_ref[i], 1), :]
        w1 = wrt_ref[pl.ds(e1_ref[i], 1), :]
        l0 = jnp.zeros((MOE_TM, LANES), F32)
        l1 = jnp.zeros((MOE_TM, LANES), F32)
        for j in range(ROW_SLABS):
            xj = hs_ref[_slab(j, MOE_TM), :]
            xb_scr[:, j * LANES:(j + 1) * LANES] = xj.astype(BF16)
            l0 = l0 + xj * w0[:, j * LANES:(j + 1) * LANES]
            l1 = l1 + xj * w1[:, j * LANES:(j + 1) * LANES]
        a0 = jax.nn.sigmoid(jnp.sum(l0, axis=1, keepdims=True))
        a1 = jax.nn.sigmoid(jnp.sum(l1, axis=1, keepdims=True))
        den = a0 + a1
        acc = None
        for gate, wg_ref, wu_ref, wd_ref in ((a0 / den, wg0_ref, wu0_ref, wd0_ref),
                                             (a1 / den, wg1_ref, wu1_ref, wd1_ref)):
            a = jnp.dot(xb_scr[...], wg_ref[...], preferred_element_type=F32)
            b = jnp.dot(xb_scr[...], wu_ref[...], preferred_element_type=F32)
            hid = (_silu(a) * b * gate).astype(BF16)
            o = jnp.dot(hid, wd_ref[...], preferred_element_type=F32)
            acc = o if acc is None else acc + o

        @pl.when(i >= 1)
        def _():
            wait_scatter(nv_ref[jnp.maximum(i - 1, 0)])

        _store_slabs(stg, acc, MOE_TM)
        base = i * MOE_TM

        def scatter_row(r, priority):
            pltpu.make_async_copy(_row_slabs(stg, r), _row_slabs(y_hbm, tok_ref[base + r]),
                                  ssem).start(priority=priority)

        def scatter_rows(c, carry):
            for u in range(DMA_UNROLL):
                scatter_row(c * DMA_UNROLL + u, u % 2)
            return carry

        def scatter_one(r, carry):
            scatter_row(r, 0)
            return carry

        n_full = nv_ref[i] // DMA_UNROLL
        lax.fori_loop(0, n_full, scatter_rows, 0)
        lax.fori_loop(n_full * DMA_UNROLL, nv_ref[i], scatter_one, 0)

        @pl.when(i == n_used - 1)
        def _():
            wait_scatter(nv_ref[i])


def _moe(hs, row_tok, tiles, wrt, wg, wu, wd, layer):
    e0, e1, n_valid, n_used = tiles

    def w_spec(which, shape):
        return pl.BlockSpec((None, None) + shape,
                            lambda i, e0, e1, nv, nu, tok: (layer, (e0, e1)[which][i], 0, 0))

    up_shape, down_shape = (D_MODEL, D_FF), (D_FF, D_MODEL)
    grid_spec = pltpu.PrefetchScalarGridSpec(
        num_scalar_prefetch=5,
        grid=(MOE_TILES,),
        in_specs=[pl.BlockSpec((MOE_TM * ROW_SLABS, LANES),
                               lambda i, e0, e1, nv, nu, tok: (jnp.minimum(i, nu[0] - 1), 0)),
                  pl.BlockSpec((N_EXPERTS, D_MODEL), lambda i, *_: (0, 0)),
                  w_spec(0, up_shape), w_spec(0, up_shape), w_spec(0, down_shape),
                  w_spec(1, up_shape), w_spec(1, up_shape), w_spec(1, down_shape)],
        out_specs=pl.BlockSpec(memory_space=pl.ANY),
        scratch_shapes=[pltpu.VMEM((MOE_TM, D_MODEL), BF16),
                        pltpu.VMEM((MOE_TM * ROW_SLABS, LANES), F32),
                        pltpu.SemaphoreType.DMA(())],
    )
    return pl.pallas_call(
        _moe_kernel,
        grid_spec=grid_spec,
        out_shape=jax.ShapeDtypeStruct((N_TOKENS * ROW_SLABS, LANES), F32),
        compiler_params=pltpu.CompilerParams(dimension_semantics=("arbitrary",),
                                             vmem_limit_bytes=VMEM_LIMIT),
        name="moe_routed",
    )(e0, e1, n_valid, n_used, row_tok, hs, wrt, wg, wu, wd, wg, wu, wd)


def _final_kernel(x1_ref, y_ref, mod_ref, fg_ref, o_ref):
    x2 = x1_ref[...] + mod_ref[5:6, :] * _rows_to_2d(y_ref, x1_ref.shape[0])
    o_ref[...] = _rms(x2, fg_ref[...])


def _final(x1, y, mod_l, fg, *, latent):
    n, own_spec, rows_spec, mod_spec = _stream_specs(latent)
    return pl.pallas_call(
        _final_kernel,
        grid=(x1.shape[0] // n,),
        in_specs=[own_spec, rows_spec, mod_spec, pl.BlockSpec((1, D_MODEL), lambda b: (0, 0))],
        out_specs=own_spec,
        out_shape=jax.ShapeDtypeStruct(x1.shape, F32),
        name="final_norm",
    )(x1, y, mod_l, fg)


def kernel(x_prompt, x_sample, cache_k, cache_v, c, c_ctx, w_ada, b_ada, norm1_g, norm2_g, w_in, w_out, rpb, pool_w, pool_scale, conv_w, conv_b, conv_ln_g, conv_ln_b, conv_pw_w, conv_pw_b, w_router, b_router, moe_w_gate, moe_w_up, moe_w_down, final_g):
    xp = x_prompt.reshape(BATCH * SEQ, D_MODEL)
    xs = x_sample.reshape(DEC_BATCH * DEC_SEQ, D_MODEL)
    cond = jnp.concatenate([c_ctx[None, :], c, jnp.zeros((N_MOD_ROWS - 1 - DEC_BATCH, D_MODEL), F32)], axis=0)
    mod = _ada_modulation(cond, w_ada, b_ada).reshape(DEPTH, N_MOD_ROWS, 6, D_MODEL)
    bias = _rpb_tiles(rpb)
    ck = cache_k.reshape(DEC_BATCH, DEPTH, PAST_LEN, D_ATTN)
    cv = cache_v.reshape(DEC_BATCH, DEPTH, PAST_LEN, D_ATTN)
    wrt = w_router.T
    wr = jnp.pad(w_router, ((0, 0), (0, LANES - N_EXPERTS)))
    wr_hi = wr.astype(BF16)
    wr_lo = (wr - wr_hi.astype(F32)).astype(BF16)
    br = b_router.reshape(N_EXPERTS, 1)
    fg = final_g.reshape(1, D_MODEL)
    wg, wu, wd = moe_w_gate.astype(BF16), moe_w_up.astype(BF16), moe_w_down.astype(BF16)
    new_k = new_v = None
    prev = None
    for l in range(DEPTH):
        pool_bd = jnp.zeros((D_POOL, D_POOL), F32)
        gd = D_POOL // 4
        for g in range(4):
            pool_bd = pool_bd.at[g * gd:(g + 1) * gd, g * gd:(g + 1) * gd].set(pool_w[l, g])
        weights = [norm1_g[l].reshape(1, -1),
                   w_in[l].astype(BF16), w_out[l].astype(BF16),
                   pool_bd.astype(BF16), pool_scale[l].reshape(1, -1),
                   conv_w[l], conv_b[l].reshape(1, -1),
                   conv_ln_g[l].reshape(1, -1), conv_ln_b[l].reshape(1, -1),
                   conv_pw_w[l].astype(BF16), conv_pw_b[l].reshape(1, -1)]

        xp, new_k, new_v = _mixer(xp, mod[l], weights, latent=False, prev=prev, kv_old=(new_k, new_v), layer=l)
        xs, = _mixer(xs, mod[l], weights, latent=True, prev=prev, cache=(ck, cv), bias=bias[l], layer=l)

        h, meta, count = _route(xp, xs, mod[l], norm2_g[l].reshape(1, -1), wr_hi, wr_lo, br)
        perm, tiles = _route_plan(meta[0], meta[1], count[:N_BUCKETS, 0])
        hs, row_tok = _permute(h, *perm)
        y = _moe(hs, row_tok, tiles, wrt, wg, wu, wd, l)
        prev = (y, mod[l])

    y_prompt = _final(xp, y, mod[DEPTH - 1], fg, latent=False).reshape(BATCH, SEQ, D_MODEL)
    y_sample = _final(xs, y, mod[DEPTH - 1], fg, latent=True).reshape(DEC_BATCH, DEC_SEQ, D_MODEL)
    new_cache_k = new_k.reshape(BATCH, DEPTH, SEQ, N_HEADS, HEAD_DIM)
    new_cache_v = new_v.reshape(BATCH, DEPTH, SEQ, N_HEADS, HEAD_DIM)
    return (y_prompt, y_sample, new_cache_k, new_cache_v)
```

```python
import functools

import numpy as np
import jax
import jax.numpy as jnp
from jax import lax
from jax.experimental import pallas as pl
from jax.experimental.pallas import tpu as pltpu

D_MODEL = 1024
BATCH = 32
SEQ = 256
DEPTH = 2
DEC_BATCH = 8
DEC_SEQ = 1024
PAST_LEN = 512
GRID_W = 64
GRID_ROWS = DEC_SEQ // GRID_W
D_ATTN = 512
HEAD_DIM = 64
N_HEADS = 8
N_PAIRS = N_HEADS // 2
WIN_ROWS = 8
WIN_COLS = 16
D_POOL = 256
D_CONV = 256
CONV_WIDTH = 31
D_IN = 3 * D_ATTN + D_POOL + 2 * D_CONV
N_EXPERTS = 16
N_GROUPS = 4
GROUP_SIZE = N_EXPERTS // N_GROUPS
D_FF = 512
EPS = 1e-6
N_MOD_ROWS = 16
N_DR = 2 * WIN_ROWS - 1
N_DC = 2 * WIN_COLS - 1
N_BIAS_TILES = N_DR - 1
LAT_GROUP_ROWS = 4
SEQ_PAD = 16
LANES = 128
ROW_SLABS = D_MODEL // LANES
N_TOKENS = BATCH * SEQ + DEC_BATCH * DEC_SEQ
PAIRS_PER_GROUP = GROUP_SIZE * (GROUP_SIZE - 1) // 2
N_BUCKETS = N_GROUPS * PAIRS_PER_GROUP
BUCKET_ROWS = 32
META_ROWS = 8
ROUTE_TM = 512
ROUTE_CHUNK = 32
CTX_ROUTE_BLOCKS = BATCH * SEQ // ROUTE_TM
MOE_TM = 256
MOE_TILES = N_TOKENS // MOE_TM + N_BUCKETS
PERM_TM = 512
DMA_UNROLL = 8
VMEM_LIMIT = 60 * 1024 * 1024

F32 = jnp.float32
BF16 = jnp.bfloat16
HIGHEST = lax.Precision.HIGHEST
NT_DIMS = (((1,), (1,)), ((), ()))


def _rms(x, g):
    return x * lax.rsqrt(jnp.mean(x * x, axis=-1, keepdims=True) + EPS) * g


def _silu(x):
    return x * jax.nn.sigmoid(x)


def _ada_kernel(cond_ref, w_ref, b_ref, o_ref):
    o_ref[...] = jnp.dot(_silu(cond_ref[...]), w_ref[...], preferred_element_type=F32,
                         precision=HIGHEST) + b_ref[...]


def _ada_modulation(cond, w_ada, b_ada):
    n_col = 6 * D_MODEL // D_MODEL
    return pl.pallas_call(
        _ada_kernel,
        grid=(DEPTH, n_col),
        in_specs=[
            pl.BlockSpec((N_MOD_ROWS, D_MODEL), lambda l, n: (0, 0)),
            pl.BlockSpec((None, D_MODEL, D_MODEL), lambda l, n: (l, 0, n)),
            pl.BlockSpec((None, 1, D_MODEL), lambda l, n: (l, 0, n)),
        ],
        out_specs=pl.BlockSpec((None, N_MOD_ROWS, D_MODEL), lambda l, n: (l, 0, n)),
        out_shape=jax.ShapeDtypeStruct((DEPTH, N_MOD_ROWS, 6 * D_MODEL), F32),
        name="ada_modulation",
    )(cond, w_ada, b_ada.reshape(DEPTH, 1, 6 * D_MODEL))


def _rpb_kernel(rpb_ref, o_ref):
    base = (pl.program_id(0) * N_HEADS + pl.program_id(1)) * (N_DR * N_DC)
    cq = lax.broadcasted_iota(jnp.int32, (GRID_W, LANES), 0)
    lane = lax.broadcasted_iota(jnp.int32, (GRID_W, LANES), 1)
    ck = lane & (GRID_W - 1)
    right = lane >= GRID_W
    dcm = jnp.clip(ck - cq, -(WIN_COLS - 1), WIN_COLS - 1) + (WIN_COLS - 1)
    cs = jnp.clip(cq - WIN_COLS // 2, 0, GRID_W - WIN_COLS)
    valid = (ck >= cs) & (ck < cs + WIN_COLS)
    for d in range(N_BIAS_TILES):
        acc = jnp.zeros((GRID_W, LANES), F32)
        for dc in range(N_DC):
            v = jnp.where(right, rpb_ref[base + (d + 1) * N_DC + dc], rpb_ref[base + d * N_DC + dc])
            acc = jnp.where(dcm == dc, v, acc)
        o_ref[d] = jnp.where(valid, acc, -jnp.inf)


def _rpb_tiles(rpb):
    return pl.pallas_call(
        _rpb_kernel,
        grid=(DEPTH, N_HEADS),
        in_specs=[pl.BlockSpec(memory_space=pltpu.SMEM)],
        out_specs=pl.BlockSpec((None, None, N_BIAS_TILES, GRID_W, LANES), lambda l, h: (l, h, 0, 0, 0)),
        out_shape=jax.ShapeDtypeStruct((DEPTH, N_HEADS, N_BIAS_TILES, GRID_W, LANES), F32),
        name="rpb_tiles",
    )(rpb.reshape(-1))


def _shift_rows(a, s):
    return a if s == 0 else pltpu.roll(a, s % a.shape[0], 0)


def _pad_rows(a):
    return jnp.concatenate([a, jnp.zeros((SEQ_PAD, a.shape[1]), a.dtype)], axis=0)


def _softmax_parts(s):
    m = jnp.max(s, axis=-1, keepdims=True)
    e = jnp.exp(s - m)
    return e, jnp.sum(e, axis=-1, keepdims=True)


def _context_attention(qkv_scr, m_scr, n):
    left = lax.broadcasted_iota(jnp.int32, (n, LANES), 1) < HEAD_DIM
    for p in range(N_PAIRS):
        q = qkv_scr[p]
        k = qkv_scr[N_PAIRS + p]
        v = qkv_scr[2 * N_PAIRS + p]
        zero = jnp.zeros_like(q)
        outs = []
        for qh in (jnp.where(left, q, zero), jnp.where(left, zero, q)):
            s = lax.dot_general(qh, k, NT_DIMS, preferred_element_type=F32)
            e, l = _softmax_parts(s)
            outs.append(jnp.dot(e.astype(BF16), v, preferred_element_type=F32) / l)
        m_scr[p] = jnp.where(left, outs[0], outs[1]).astype(BF16)


def _window_start(qr):
    return min(max(qr - WIN_ROWS // 2, 0), GRID_ROWS - WIN_ROWS)


def _latent_groups():
    groups = []
    for r0 in range(0, GRID_ROWS, LAT_GROUP_ROWS):
        k0 = _window_start(r0)
        k1 = _window_start(r0 + LAT_GROUP_ROWS - 1) + WIN_ROWS
        nk = k1 - k0 + (k1 - k0) % 2
        k0 = min(k0, GRID_ROWS - nk)
        groups.append((r0, LAT_GROUP_ROWS, k0, nk))
    return groups


def _group_bias(bias_ref, p, r0, nq, k0, nk):
    left = lax.broadcasted_iota(jnp.int32, (GRID_W, LANES), 1) < HEAD_DIM
    neg = jnp.full((GRID_W, LANES), -jnp.inf, F32)
    blocks = []
    for hh in range(2):
        for qr in range(r0, r0 + nq):
            rs = _window_start(qr)
            pieces = []
            for kr in range(k0, k0 + nk, 2):
                in0 = rs <= kr < rs + WIN_ROWS
                in1 = rs <= kr + 1 < rs + WIN_ROWS
                d = kr - qr + (WIN_ROWS - 1)
                if in0 and in1:
                    pieces.append(bias_ref[2 * p + hh, d])
                elif in0:
                    pieces.append(jnp.where(left, bias_ref[2 * p + hh, d], neg))
                elif in1:
                    pieces.append(jnp.where(left, neg, bias_ref[2 * p + hh, d]))
                else:
                    pieces.append(neg)
            blocks.append(jnp.concatenate(pieces, axis=1))
    return jnp.concatenate(blocks, axis=0)


def _latent_attention(qkv_scr, kc_scr, vc_scr, bias_ref, m_scr):
    def body(p, carry):
        for r0, nq, k0, nk in _latent_groups():
            rows = nq * GRID_W
            left = lax.broadcasted_iota(jnp.int32, (rows, LANES), 1) < HEAD_DIM
            q = qkv_scr[p, r0 * GRID_W:r0 * GRID_W + rows, :]
            zero = jnp.zeros_like(q)
            qm = jnp.concatenate([jnp.where(left, q, zero), jnp.where(left, zero, q)], axis=0)
            kl = qkv_scr[N_PAIRS + p, k0 * GRID_W:(k0 + nk) * GRID_W, :]
            vl = qkv_scr[2 * N_PAIRS + p, k0 * GRID_W:(k0 + nk) * GRID_W, :]
            s_loc = (lax.dot_general(qm, kl, NT_DIMS, preferred_element_type=F32)
                     + _group_bias(bias_ref, p, r0, nq, k0, nk))
            s_ctx = lax.dot_general(qm, kc_scr[p], NT_DIMS, preferred_element_type=F32)
            m = jnp.maximum(jnp.max(s_loc, axis=-1, keepdims=True), jnp.max(s_ctx, axis=-1, keepdims=True))
            e_loc = jnp.exp(s_loc - m)
            e_ctx = jnp.exp(s_ctx - m)
            l = jnp.sum(e_loc, axis=-1, keepdims=True) + jnp.sum(e_ctx, axis=-1, keepdims=True)
            o = (jnp.dot(e_loc.astype(BF16), vl, preferred_element_type=F32)
                 + jnp.dot(e_ctx.astype(BF16), vc_scr[p], preferred_element_type=F32)) / l
            m_scr[p, r0 * GRID_W:r0 * GRID_W + rows, :] = jnp.where(left, o[:rows], o[rows:]).astype(BF16)
        return carry

    lax.fori_loop(0, N_PAIRS, body, 0)


def _router_bucket(h_hi, h_lo, wr_hi_ref, wr_lo_ref, br_ref, n):
    logits = (jnp.dot(h_hi, wr_hi_ref[...], preferred_element_type=F32)
              + (jnp.dot(h_lo, wr_hi_ref[...], preferred_element_type=F32)
                 + jnp.dot(h_hi, wr_lo_ref[...], preferred_element_type=F32)))
    sel = jax.nn.sigmoid(logits.T[:N_EXPERTS]) + br_ref[...]
    rows = [sel[e:e + 1, :] for e in range(N_EXPERTS)]
    gscore = []
    for g in range(N_GROUPS):
        r4 = rows[g * GROUP_SIZE:(g + 1) * GROUP_SIZE]
        best = None
        for i in range(GROUP_SIZE):
            for j in range(i + 1, GROUP_SIZE):
                s = r4[i] + r4[j]
                best = s if best is None else jnp.maximum(best, s)
        gscore.append(best)
    gbest = jnp.zeros((1, n), jnp.int32)
    gval = gscore[0]
    for g in range(1, N_GROUPS):
        upd = gscore[g] > gval
        gbest = jnp.where(upd, g, gbest)
        gval = jnp.where(upd, gscore[g], gval)
    picked = []
    for e in range(N_EXPERTS):
        g = e // GROUP_SIZE
        rank = jnp.zeros((1, n), jnp.int32)
        for j in range(g * GROUP_SIZE, (g + 1) * GROUP_SIZE):
            if j == e:
                continue
            ahead = (rows[j] > rows[e]) | (rows[j] == rows[e]) if j < e else rows[j] > rows[e]
            rank = rank + ahead.astype(jnp.int32)
        picked.append((rank < 2) & (gbest == g))
    e_lo = jnp.zeros((1, n), jnp.int32)
    e_hi = jnp.zeros((1, n), jnp.int32)
    for e in range(N_EXPERTS):
        e_hi = jnp.where(picked[e], e, e_hi)
        e_lo = jnp.where(picked[N_EXPERTS - 1 - e], N_EXPERTS - 1 - e, e_lo)
    a = e_lo - GROUP_SIZE * gbest
    b = e_hi - GROUP_SIZE * gbest
    return gbest * PAIRS_PER_GROUP + jnp.right_shift(a * (2 * GROUP_SIZE - 1 - a), 1) + (b - a - 1)


def _slab(j, n):
    return pl.ds(j, n, stride=ROW_SLABS)


def _rows_to_2d(ref, n):
    return jnp.concatenate([ref[_slab(j, n), :] for j in range(ROW_SLABS)], axis=1)


def _store_slabs(ref, val, n, first_row=0):
    for j in range(ROW_SLABS):
        ref[_slab(first_row + j, n), :] = val[:, j * LANES:(j + 1) * LANES]


def _mixer_kernel(*refs, latent, has_prev, n):
    it = iter(refs)
    x_ref, mod_ref = next(it), next(it)
    if has_prev:
        y_ref, modp_ref = next(it), next(it)
        if not latent:
            kold_ref, vold_ref = next(it), next(it)
    if latent:
        ck_ref, cv_ref, bias_ref = next(it), next(it), next(it)
    (n1g_ref, w_in_ref, w_out_ref, poolw_ref, pscale_ref, convw_ref, convb_ref,
     lng_ref, lnb_ref, pww_ref, pwb_ref) = [next(it) for _ in range(11)]
    x1_ref = next(it)
    if not latent:
        knew_ref, vnew_ref = next(it), next(it)
    qkv_scr, u_scr, m_scr = next(it), next(it), next(it)
    if latent:
        kc_scr, vc_scr = next(it), next(it)

    x = x_ref[...]
    if has_prev:
        x = x + modp_ref[5:6, :] * _rows_to_2d(y_ref, n)
    layer = 0
    if has_prev and not latent:
        layer = kold_ref.shape[0]
        knew_ref[0:layer] = kold_ref[...]
        vnew_ref[0:layer] = vold_ref[...]
    sh1, sc1, g1 = mod_ref[0:1, :], mod_ref[1:2, :], mod_ref[2:3, :]

    hb = (_rms(x, n1g_ref[...]) * (1.0 + sc1) + sh1).astype(BF16)
    cw = 2 * LANES
    for j in range(D_IN // cw):
        zj = jnp.dot(hb, w_in_ref[:, j * cw:(j + 1) * cw], preferred_element_type=F32)
        if j < 6:
            part = j // 2
            zs = zj * (HEAD_DIM ** -0.5) if part == 0 else zj
            c = part * N_PAIRS + 2 * (j % 2)
            qkv_scr[c] = zs[:, :LANES].astype(BF16)
            qkv_scr[c + 1] = zs[:, LANES:].astype(BF16)
            if not latent and part == 1:
                knew_ref[layer, :, (j % 2) * cw:(j % 2 + 1) * cw] = zj
            if not latent and part == 2:
                vnew_ref[layer, :, (j % 2) * cw:(j % 2 + 1) * cw] = zj
        else:
            u_scr[j - 6] = zj

    if latent:
        for p in range(N_PAIRS):
            kc_scr[p] = ck_ref[:, p * LANES:(p + 1) * LANES].astype(BF16)
            vc_scr[p] = cv_ref[:, p * LANES:(p + 1) * LANES].astype(BF16)
        _latent_attention(qkv_scr, kc_scr, vc_scr, bias_ref, m_scr)
    else:
        _context_attention(qkv_scr, m_scr, n)

    row = lax.broadcasted_iota(jnp.int32, (n, D_POOL), 0)
    grp = lax.broadcasted_iota(jnp.int32, (n, D_POOL), 1) // (D_POOL // 4)
    up = u_scr[0]
    t2 = _pad_rows(up)
    t2 = t2 + _shift_rows(t2, 1)
    t4 = t2 + _shift_rows(t2, 2)
    t8 = t4 + _shift_rows(t4, 4)
    t16 = t8 + _shift_rows(t8, 8)
    s2, s4, s8, s16 = t2[:n], _shift_rows(t4, -1)[:n], _shift_rows(t8, -3)[:n], _shift_rows(t16, -7)[:n]
    wsum = jnp.where(grp == 0, s2, jnp.where(grp == 1, s4, jnp.where(grp == 2, s8, s16)))
    half = jnp.left_shift(1, grp)
    lo = jnp.maximum(row - half, 0)
    hi = jnp.minimum(row + half - 1, n - 1)
    diff = wsum / (hi - lo + 1).astype(F32) - up
    pooled = jnp.dot(diff.astype(BF16), poolw_ref[...], preferred_element_type=F32) * pscale_ref[...]
    m_scr[N_PAIRS] = pooled[:, :LANES].astype(BF16)
    m_scr[N_PAIRS + 1] = pooled[:, LANES:].astype(BF16)

    y = _pad_rows(u_scr[1] * jax.nn.sigmoid(u_scr[2]))
    acc = jnp.zeros(y.shape, F32) + convb_ref[...]
    for k in range(CONV_WIDTH):
        acc = acc + convw_ref[k:k + 1, :] * _shift_rows(y, CONV_WIDTH // 2 - k)
    acc = acc[:n]
    mu = jnp.mean(acc, axis=-1, keepdims=True)
    cen = acc - mu
    var = jnp.mean(cen * cen, axis=-1, keepdims=True)
    yn = _silu(cen * lax.rsqrt(var + EPS) * lng_ref[...] + lnb_ref[...])
    conv = jnp.dot(yn.astype(BF16), pww_ref[...], preferred_element_type=F32) + pwb_ref[...]
    m_scr[N_PAIRS + 2] = conv[:, :LANES].astype(BF16)
    m_scr[N_PAIRS + 3] = conv[:, LANES:].astype(BF16)

    mixed = jnp.concatenate([m_scr[j] for j in range(ROW_SLABS)], axis=1)
    x1_ref[...] = x + g1 * jnp.dot(mixed, w_out_ref[...], preferred_element_type=F32)


def _route_kernel(xp_ref, xs_ref, mod_ref, n2g_ref, wr_hi_ref, wr_lo_ref, br_ref,
                  h_ref, meta_ref, count_ref, run_scr, hi_scr, lo_scr):
    i = pl.program_id(0)
    n = ROUTE_TM

    @pl.when(i == 0)
    def _():
        run_scr[...] = jnp.zeros_like(run_scr)

    def norm_rows(x_ref):
        gain, scale, shift = n2g_ref[...], 1.0 + mod_ref[4:5, :], mod_ref[3:4, :]
        for r0 in range(0, n, ROUTE_CHUNK):
            h2 = _rms(x_ref[r0:r0 + ROUTE_CHUNK, :], gain) * scale + shift
            _store_slabs(h_ref, h2, ROUTE_CHUNK, r0 * ROW_SLABS)
            hi = h2.astype(BF16)
            hi_scr[r0:r0 + ROUTE_CHUNK, :] = hi
            lo_scr[r0:r0 + ROUTE_CHUNK, :] = (h2 - hi.astype(F32)).astype(BF16)

    pl.when(i < CTX_ROUTE_BLOCKS)(lambda: norm_rows(xp_ref))
    pl.when(i >= CTX_ROUTE_BLOCKS)(lambda: norm_rows(xs_ref))

    bucket = _router_bucket(hi_scr[...], lo_scr[...], wr_hi_ref, wr_lo_ref, br_ref, n)
    onehot = lax.broadcasted_iota(jnp.int32, (BUCKET_ROWS, n), 0) == bucket
    upper = (lax.broadcasted_iota(jnp.int32, (n, n), 0) <= lax.broadcasted_iota(jnp.int32, (n, n), 1))
    incl = jnp.dot(onehot.astype(BF16), upper.astype(BF16), preferred_element_type=F32)
    run = run_scr[...]
    rank = jnp.sum(jnp.where(onehot, incl + run, 0.0), axis=0, keepdims=True) - 1.0
    run = run + jnp.sum(onehot.astype(F32), axis=1, keepdims=True)
    run_scr[...] = run
    meta_ref[...] = jnp.concatenate(
        [bucket, rank.astype(jnp.int32), jnp.zeros((META_ROWS - 2, n), jnp.int32)], axis=0)
    count_ref[...] = jnp.broadcast_to(run, count_ref.shape).astype(jnp.int32)


def _route(x1p, x1s, mod_l, n2g, wr_hi, wr_lo, br):
    per_seq = DEC_SEQ // ROUTE_TM
    last_ctx = CTX_ROUTE_BLOCKS - 1
    return pl.pallas_call(
        _route_kernel,
        grid=(N_TOKENS // ROUTE_TM,),
        in_specs=[
            pl.BlockSpec((ROUTE_TM, D_MODEL), lambda i: (jnp.minimum(i, last_ctx), 0)),
            pl.BlockSpec((ROUTE_TM, D_MODEL), lambda i: (jnp.maximum(i - CTX_ROUTE_BLOCKS, 0), 0)),
            pl.BlockSpec((None, 6, D_MODEL),
                         lambda i: (jnp.where(i < CTX_ROUTE_BLOCKS, 0, (i - CTX_ROUTE_BLOCKS) // per_seq + 1), 0, 0)),
            pl.BlockSpec((1, D_MODEL), lambda i: (0, 0)),
            pl.BlockSpec((D_MODEL, LANES), lambda i: (0, 0)),
            pl.BlockSpec((D_MODEL, LANES), lambda i: (0, 0)),
            pl.BlockSpec((N_EXPERTS, 1), lambda i: (0, 0)),
        ],
        out_specs=[pl.BlockSpec((ROUTE_TM * ROW_SLABS, LANES), lambda i: (i, 0)),
                   pl.BlockSpec((META_ROWS, ROUTE_TM), lambda i: (0, i)),
                   pl.BlockSpec((BUCKET_ROWS, LANES), lambda i: (0, 0))],
        out_shape=[jax.ShapeDtypeStruct((N_TOKENS * ROW_SLABS, LANES), F32),
                   jax.ShapeDtypeStruct((META_ROWS, N_TOKENS), jnp.int32),
                   jax.ShapeDtypeStruct((BUCKET_ROWS, LANES), jnp.int32)],
        scratch_shapes=[pltpu.VMEM((BUCKET_ROWS, 1), F32),
                        pltpu.VMEM((ROUTE_TM, D_MODEL), BF16),
                        pltpu.VMEM((ROUTE_TM, D_MODEL), BF16)],
        compiler_params=pltpu.CompilerParams(dimension_semantics=("arbitrary",)),
        name="norm_route",
    )(x1p, x1s, mod_l, n2g, wr_hi, wr_lo, br)


def _const_spec(shape):
    nd = len(shape)
    return pl.BlockSpec(shape, lambda b: (0,) * nd, pipeline_mode=pl.Buffered(1))


def _stream_specs(latent, single_buffer_rows=False):
    n = DEC_SEQ if latent else SEQ
    first = BATCH * SEQ // n if latent else 0
    own = pl.BlockSpec((n, D_MODEL), lambda b: (b, 0))
    rows = pl.BlockSpec((n * ROW_SLABS, LANES), lambda b: (first + b, 0),
                        pipeline_mode=pl.Buffered(1) if single_buffer_rows else None)
    mod = pl.BlockSpec((None, 6, D_MODEL), (lambda b: (b + 1, 0, 0)) if latent else (lambda b: (0, 0, 0)))
    return n, own, rows, mod


def _mixer(x, mod_l, weights, *, latent, prev=None, kv_old=None, cache=None, bias=None, layer=0):
    n, own_spec, rows_spec, mod_spec = _stream_specs(latent, single_buffer_rows=latent)
    tok = x.shape[0]
    nb = tok // n
    in_specs = [own_spec, mod_spec]
    args = [x, mod_l]
    if prev is not None:
        in_specs += [rows_spec, mod_spec]
        args += list(prev)
        if not latent:
            in_specs += [pl.BlockSpec((None, layer, n, D_ATTN), lambda b: (b, 0, 0, 0))] * 2
            args += list(kv_old)
    if latent:
        cache_spec = pl.BlockSpec((None, None, PAST_LEN, D_ATTN), lambda b: (b, layer, 0, 0),
                                  pipeline_mode=pl.Buffered(1))
        in_specs += [cache_spec, cache_spec, _const_spec(bias.shape)]
        args += [cache[0], cache[1], bias]
    in_specs += [_const_spec(w.shape) for w in weights]
    args += list(weights)
    out_shape = [jax.ShapeDtypeStruct((tok, D_MODEL), F32)]
    out_specs = [own_spec]
    if not latent:
        out_shape += [jax.ShapeDtypeStruct((nb, layer + 1, n, D_ATTN), F32)] * 2
        out_specs += [pl.BlockSpec((None, layer + 1, n, D_ATTN), lambda b: (b, 0, 0, 0))] * 2
    scratch = [pltpu.VMEM((3 * N_PAIRS, n, LANES), BF16),
               pltpu.VMEM((3, n, D_POOL), F32),
               pltpu.VMEM((ROW_SLABS, n, LANES), BF16)]
    if latent:
        scratch += [pltpu.VMEM((N_PAIRS, PAST_LEN, LANES), BF16),
                    pltpu.VMEM((N_PAIRS, PAST_LEN, LANES), BF16)]
    return pl.pallas_call(
        functools.partial(_mixer_kernel, latent=latent, has_prev=prev is not None, n=n),
        grid=(nb,),
        in_specs=in_specs,
        out_specs=out_specs,
        out_shape=out_shape,
        scratch_shapes=scratch,
        compiler_params=pltpu.CompilerParams(dimension_semantics=("arbitrary",),
                                             vmem_limit_bytes=VMEM_LIMIT),
        name="mixer_latent" if latent else "mixer_context",
    )(*args)


def _bucket_tables():
    lo, hi = [], []
    for g in range(N_GROUPS):
        for a in range(GROUP_SIZE):
            for b in range(a + 1, GROUP_SIZE):
                lo.append(g * GROUP_SIZE + a)
                hi.append(g * GROUP_SIZE + b)
    return np.asarray(lo, np.int32), np.asarray(hi, np.int32)


def _route_plan(bucket, rank, count):
    tiles_b = (count + MOE_TM - 1) // MOE_TM
    tile_end = jnp.cumsum(tiles_b)
    tile_start = tile_end - tiles_b
    pos = tile_start[bucket] * MOE_TM + rank
    n_used = tile_end[-1]
    fill_lo = jnp.concatenate([tile_start * MOE_TM + count, (n_used * MOE_TM)[None]])
    fill_hi = jnp.concatenate([tile_end * MOE_TM, jnp.full((1,), MOE_TILES * MOE_TM, jnp.int32)])
    tile = jnp.arange(MOE_TILES, dtype=jnp.int32)
    tile_b = jnp.sum((tile[:, None] >= tile_end[None, :]).astype(jnp.int32), axis=1)
    last_b = jnp.sum((n_used - 1 >= tile_end).astype(jnp.int32))
    tile_b = jnp.minimum(tile_b, last_b)
    n_valid = jnp.clip(count[tile_b] - (tile - tile_start[tile_b]) * MOE_TM, 0, MOE_TM)
    lo_tab, hi_tab = _bucket_tables()
    i32 = lambda a: a.astype(jnp.int32)
    return (i32(pos), i32(fill_lo), i32(fill_hi)), (
        jnp.asarray(lo_tab)[tile_b], jnp.asarray(hi_tab)[tile_b], i32(n_valid), i32(n_used).reshape(1))


def _permute_kernel(pos_ref, lo_ref, hi_ref, h_hbm, hs_hbm, tok_ref, buf, lsem, ssem, fsem):
    i = pl.program_id(0)
    last = pl.num_programs(0) - 1
    half = i % 2
    blk = PERM_TM * ROW_SLABS

    def load(block, h):
        return pltpu.make_async_copy(h_hbm.at[pl.ds(block * blk, blk), :], buf.at[pl.ds(h * blk, blk), :], lsem.at[h])

    def wait_rows(sem, rows):
        n = rows * ROW_SLABS
        pltpu.make_async_copy(hs_hbm.at[pl.ds(0, n), :], hs_hbm.at[pl.ds(0, n), :], sem).wait()

    @pl.when(i == 0)
    def _():
        load(0, 0).start()

    load(i, half).wait()

    @pl.when(i >= 1)
    def _():
        wait_rows(ssem.at[1 - half], PERM_TM)

    @pl.when(i < last)
    def _():
        load(i + 1, 1 - half).start()

    base = i * PERM_TM
    for r in range(PERM_TM):
        p = pos_ref[base + r]
        pltpu.make_async_copy(_row_slabs(buf, half * PERM_TM + r), _row_slabs(hs_hbm, p),
                              ssem.at[half]).start(priority=r % 2)
        tok_ref[p] = base + r

    @pl.when(i == 0)
    def _():
        def fill(first, rows):
            n = rows * ROW_SLABS
            pltpu.make_async_copy(buf.at[pl.ds(0, n), :],
                                  hs_hbm.at[pl.ds(pl.multiple_of(first * ROW_SLABS, ROW_SLABS), n), :], fsem).start()

        def zero_tok(p, carry):
            tok_ref[p] = 0
            return carry

        for b in range(N_BUCKETS + 1):
            lax.fori_loop(lo_ref[b], hi_ref[b], zero_tok, 0)
        for b in range(N_BUCKETS):
            first = lo_ref[b]
            length = hi_ref[b] - first
            run = MOE_TM // 2
            while run >= 1:
                take = (length & run) != 0
                pl.when(take)(functools.partial(fill, first, run))
                first = first + jnp.where(take, run, 0)
                run //= 2

        def fill_tile(t, carry):
            fill(t * MOE_TM, MOE_TM)
            return carry

        lax.fori_loop(lo_ref[N_BUCKETS] // MOE_TM, MOE_TILES, fill_tile, 0)
        wait_rows(fsem, MOE_TILES * MOE_TM - N_TOKENS)

    @pl.when(i == last)
    def _():
        wait_rows(ssem.at[half], PERM_TM)


def _permute(h, pos, fill_lo, fill_hi):
    grid_spec = pltpu.PrefetchScalarGridSpec(
        num_scalar_prefetch=3,
        grid=(N_TOKENS // PERM_TM,),
        in_specs=[pl.BlockSpec(memory_space=pl.ANY)],
        out_specs=[pl.BlockSpec(memory_space=pl.ANY), pl.BlockSpec(memory_space=pltpu.SMEM)],
        scratch_shapes=[pltpu.VMEM((2 * PERM_TM * ROW_SLABS, LANES), F32),
                        pltpu.SemaphoreType.DMA((2,)),
                        pltpu.SemaphoreType.DMA((2,)),
                        pltpu.SemaphoreType.DMA(())],
    )
    return pl.pallas_call(
        _permute_kernel,
        grid_spec=grid_spec,
        out_shape=[jax.ShapeDtypeStruct((MOE_TILES * MOE_TM * ROW_SLABS, LANES), F32),
                   jax.ShapeDtypeStruct((MOE_TILES * MOE_TM,), jnp.int32)],
        compiler_params=pltpu.CompilerParams(dimension_semantics=("arbitrary",)),
        name="permute_rows",
    )(pos, fill_lo, fill_hi, h)


def _row_slabs(ref, row):
    return ref.at[pl.ds(pl.multiple_of(row * ROW_SLABS, ROW_SLABS), ROW_SLABS), :]


def _moe_kernel(e0_ref, e1_ref, nv_ref, nu_ref, tok_ref, hs_ref, wrt_ref, wg0_ref, wu0_ref, wd0_ref,
                wg1_ref, wu1_ref, wd1_ref, y_hbm, xb_scr, stg, ssem):
    i = pl.program_id(0)
    n_used = nu_ref[0]

    def wait_scatter(rows):
        n = rows * ROW_SLABS
        pltpu.make_async_copy(y_hbm.at[pl.ds(0, n), :], y_hbm.at[pl.ds(0, n), :], ssem).wait()

    @pl.when(i < n_used)
    def _():
        w0 = wrt_ref[pl.ds(e0_ref[i], 1), :]
        w1 = wrt_ref[pl.ds(e1_ref[i], 1), :]
        l0 = jnp.zeros((MOE_TM, LANES), F32)
        l1 = jnp.zeros((MOE_TM, LANES), F32)
        for j in range(ROW_SLABS):
            xj = hs_ref[_slab(j, MOE_TM), :]
            xb_scr[:, j * LANES:(j + 1) * LANES] = xj.astype(BF16)
            l0 = l0 + xj * w0[:, j * LANES:(j + 1) * LANES]
            l1 = l1 + xj * w1[:, j * LANES:(j + 1) * LANES]
        a0 = jax.nn.sigmoid(jnp.sum(l0, axis=1, keepdims=True))
        a1 = jax.nn.sigmoid(jnp.sum(l1, axis=1, keepdims=True))
        den = a0 + a1
        acc = None
        for gate, wg_ref, wu_ref, wd_ref in ((a0 / den, wg0_ref, wu0_ref, wd0_ref),
                                             (a1 / den, wg1_ref, wu1_ref, wd1_ref)):
            a = jnp.dot(xb_scr[...], wg_ref[...], preferred_element_type=F32)
            b = jnp.dot(xb_scr[...], wu_ref[...], preferred_element_type=F32)
            hid = (_silu(a) * b * gate).astype(BF16)
            o = jnp.dot(hid, wd_ref[...], preferred_element_type=F32)
            acc = o if acc is None else acc + o

        @pl.when(i >= 1)
        def _():
            wait_scatter(nv_ref[jnp.maximum(i - 1, 0)])

        _store_slabs(stg, acc, MOE_TM)
        base = i * MOE_TM

        def scatter_row(r, priority):
            pltpu.make_async_copy(_row_slabs(stg, r), _row_slabs(y_hbm, tok_ref[base + r]),
                                  ssem).start(priority=priority)

        def scatter_rows(c, carry):
            for u in range(DMA_UNROLL):
                scatter_row(c * DMA_UNROLL + u, u % 2)
            return carry

        def scatter_one(r, carry):
            scatter_row(r, 0)
            return carry

        n_full = nv_ref[i] // DMA_UNROLL
        lax.fori_loop(0, n_full, scatter_rows, 0)
        lax.fori_loop(n_full * DMA_UNROLL, nv_ref[i], scatter_one, 0)

        @pl.when(i == n_used - 1)
        def _():
            wait_scatter(nv_ref[i])


def _moe(hs, row_tok, tiles, wrt, wg, wu, wd, layer):
    e0, e1, n_valid, n_used = tiles

    def w_spec(which, shape):
        return pl.BlockSpec((None, None) + shape,
                            lambda i, e0, e1, nv, nu, tok: (layer, (e0, e1)[which][i], 0, 0))

    up_shape, down_shape = (D_MODEL, D_FF), (D_FF, D_MODEL)
    grid_spec = pltpu.PrefetchScalarGridSpec(
        num_scalar_prefetch=5,
        grid=(MOE_TILES,),
        in_specs=[pl.BlockSpec((MOE_TM * ROW_SLABS, LANES),
                               lambda i, e0, e1, nv, nu, tok: (jnp.minimum(i, nu[0] - 1), 0)),
                  pl.BlockSpec((N_EXPERTS, D_MODEL), lambda i, *_: (0, 0)),
                  w_spec(0, up_shape), w_spec(0, up_shape), w_spec(0, down_shape),
                  w_spec(1, up_shape), w_spec(1, up_shape), w_spec(1, down_shape)],
        out_specs=pl.BlockSpec(memory_space=pl.ANY),
        scratch_shapes=[pltpu.VMEM((MOE_TM, D_MODEL), BF16),
                        pltpu.VMEM((MOE_TM * ROW_SLABS, LANES), F32),
                        pltpu.SemaphoreType.DMA(())],
    )
    return pl.pallas_call(
        _moe_kernel,
        grid_spec=grid_spec,
        out_shape=jax.ShapeDtypeStruct((N_TOKENS * ROW_SLABS, LANES), F32),
        compiler_params=pltpu.CompilerParams(dimension_semantics=("arbitrary",),
                                             vmem_limit_bytes=VMEM_LIMIT),
        name="moe_routed",
    )(e0, e1, n_valid, n_used, row_tok, hs, wrt, wg, wu, wd, wg, wu, wd)


def _final_kernel(x1_ref, y_ref, mod_ref, fg_ref, o_ref):
    x2 = x1_ref[...] + mod_ref[5:6, :] * _rows_to_2d(y_ref, x1_ref.shape[0])
    o_ref[...] = _rms(x2, fg_ref[...])


def _final(x1, y, mod_l, fg, *, latent):
    n, own_spec, rows_spec, mod_spec = _stream_specs(latent)
    return pl.pallas_call(
        _final_kernel,
        grid=(x1.shape[0] // n,),
        in_specs=[own_spec, rows_spec, mod_spec, pl.BlockSpec((1, D_MODEL), lambda b: (0, 0))],
        out_specs=own_spec,
        out_shape=jax.ShapeDtypeStruct(x1.shape, F32),
        name="final_norm",
    )(x1, y, mod_l, fg)


def kernel(x_prompt, x_sample, cache_k, cache_v, c, c_ctx, w_ada, b_ada, norm1_g, norm2_g, w_in, w_out, rpb, pool_w, pool_scale, conv_w, conv_b, conv_ln_g, conv_ln_b, conv_pw_w, conv_pw_b, w_router, b_router, moe_w_gate, moe_w_up, moe_w_down, final_g):
    xp = x_prompt.reshape(BATCH * SEQ, D_MODEL)
    xs = x_sample.reshape(DEC_BATCH * DEC_SEQ, D_MODEL)
    cond = jnp.concatenate([c_ctx[None, :], c, jnp.zeros((N_MOD_ROWS - 1 - DEC_BATCH, D_MODEL), F32)], axis=0)
    mod = _ada_modulation(cond, w_ada, b_ada).reshape(DEPTH, N_MOD_ROWS, 6, D_MODEL)
    bias = _rpb_tiles(rpb)
    ck = cache_k.reshape(DEC_BATCH, DEPTH, PAST_LEN, D_ATTN)
    cv = cache_v.reshape(DEC_BATCH, DEPTH, PAST_LEN, D_ATTN)
    wrt = w_router.T
    wr = jnp.pad(w_router, ((0, 0), (0, LANES - N_EXPERTS)))
    wr_hi = wr.astype(BF16)
    wr_lo = (wr - wr_hi.astype(F32)).astype(BF16)
    br = b_router.reshape(N_EXPERTS, 1)
    fg = final_g.reshape(1, D_MODEL)
    wg, wu, wd = moe_w_gate.astype(BF16), moe_w_up.astype(BF16), moe_w_down.astype(BF16)
    new_k = new_v = None
    prev = None
    for l in range(DEPTH):
        pool_bd = jnp.zeros((D_POOL, D_POOL), F32)
        gd = D_POOL // 4
        for g in range(4):
            pool_bd = pool_bd.at[g * gd:(g + 1) * gd, g * gd:(g + 1) * gd].set(pool_w[l, g])
        weights = [norm1_g[l].reshape(1, -1),
                   w_in[l].astype(BF16), w_out[l].astype(BF16),
                   pool_bd.astype(BF16), pool_scale[l].reshape(1, -1),
                   conv_w[l], conv_b[l].reshape(1, -1),
                   conv_ln_g[l].reshape(1, -1), conv_ln_b[l].reshape(1, -1),
                   conv_pw_w[l].astype(BF16), conv_pw_b[l].reshape(1, -1)]

        xp, new_k, new_v = _mixer(xp, mod[l], weights, latent=False, prev=prev, kv_old=(new_k, new_v), layer=l)
        xs, = _mixer(xs, mod[l], weights, latent=True, prev=prev, cache=(ck, cv), bias=bias[l], layer=l)

        h, meta, count = _route(xp, xs, mod[l], norm2_g[l].reshape(1, -1), wr_hi, wr_lo, br)
        perm, tiles = _route_plan(meta[0], meta[1], count[:N_BUCKETS, 0])
        hs, row_tok = _permute(h, *perm)
        y = _moe(hs, row_tok, tiles, wrt, wg, wu, wd, l)
        prev = (y, mod[l])

    y_prompt = _final(xp, y, mod[DEPTH - 1], fg, latent=False).reshape(BATCH, SEQ, D_MODEL)
    y_sample = _final(xs, y, mod[DEPTH - 1], fg, latent=True).reshape(DEC_BATCH, DEC_SEQ, D_MODEL)
    new_cache_k = new_k.reshape(BATCH, DEPTH, SEQ, N_HEADS, HEAD_DIM)
    new_cache_v = new_v.reshape(BATCH, DEPTH, SEQ, N_HEADS, HEAD_DIM)
    return (y_prompt, y_sample, new_cache_k, new_cache_v)
```

```python
import functools

import numpy as np
import jax
import jax.numpy as jnp
from jax import lax
from jax.experimental import pallas as pl
from jax.experimental.pallas import tpu as pltpu

D_MODEL = 1024
BATCH = 32
SEQ = 256
DEPTH = 2
DEC_BATCH = 8
DEC_SEQ = 1024
PAST_LEN = 512
GRID_W = 64
GRID_ROWS = DEC_SEQ // GRID_W
D_ATTN = 512
HEAD_DIM = 64
N_HEADS = 8
N_PAIRS = N_HEADS // 2
WIN_ROWS = 8
WIN_COLS = 16
D_POOL = 256
D_CONV = 256
CONV_WIDTH = 31
D_IN = 3 * D_ATTN + D_POOL + 2 * D_CONV
N_EXPERTS = 16
N_GROUPS = 4
GROUP_SIZE = N_EXPERTS // N_GROUPS
D_FF = 512
EPS = 1e-6
N_MOD_ROWS = 16
N_DR = 2 * WIN_ROWS - 1
N_DC = 2 * WIN_COLS - 1
N_BIAS_TILES = N_DR - 1
LAT_GROUP_ROWS = 4
SEQ_PAD = 16
LANES = 128
ROW_SLABS = D_MODEL // LANES
N_TOKENS = BATCH * SEQ + DEC_BATCH * DEC_SEQ
PAIRS_PER_GROUP = GROUP_SIZE * (GROUP_SIZE - 1) // 2
N_BUCKETS = N_GROUPS * PAIRS_PER_GROUP
BUCKET_ROWS = 32
META_ROWS = 8
ROUTE_TM = 512
ROUTE_CHUNK = 32
CTX_ROUTE_BLOCKS = BATCH * SEQ // ROUTE_TM
MOE_TM = 256
MOE_TILES = N_TOKENS // MOE_TM + N_BUCKETS
PERM_TM = 512
MOE_PHASES = 6
VMEM_LIMIT = 62 * 1024 * 1024

F32 = jnp.float32
BF16 = jnp.bfloat16
HIGHEST = lax.Precision.HIGHEST
NT_DIMS = (((1,), (1,)), ((), ()))


def _rms(x, g):
    return x * lax.rsqrt(jnp.mean(x * x, axis=-1, keepdims=True) + EPS) * g


def _silu(x):
    return x * jax.nn.sigmoid(x)


def _ada_kernel(cond_ref, w_ref, b_ref, o_ref):
    o_ref[...] = jnp.dot(_silu(cond_ref[...]), w_ref[...], preferred_element_type=F32,
                         precision=HIGHEST) + b_ref[...]


def _ada_modulation(cond, w_ada, b_ada):
    n_col = 6 * D_MODEL // D_MODEL
    return pl.pallas_call(
        _ada_kernel,
        grid=(DEPTH, n_col),
        in_specs=[
            pl.BlockSpec((N_MOD_ROWS, D_MODEL), lambda l, n: (0, 0)),
            pl.BlockSpec((None, D_MODEL, D_MODEL), lambda l, n: (l, 0, n)),
            pl.BlockSpec((None, 1, D_MODEL), lambda l, n: (l, 0, n)),
        ],
        out_specs=pl.BlockSpec((None, N_MOD_ROWS, D_MODEL), lambda l, n: (l, 0, n)),
        out_shape=jax.ShapeDtypeStruct((DEPTH, N_MOD_ROWS, 6 * D_MODEL), F32),
        name="ada_modulation",
    )(cond, w_ada, b_ada.reshape(DEPTH, 1, 6 * D_MODEL))


def _rpb_kernel(rpb_ref, o_ref):
    base = (pl.program_id(0) * N_HEADS + pl.program_id(1)) * (N_DR * N_DC)
    cq = lax.broadcasted_iota(jnp.int32, (GRID_W, LANES), 0)
    lane = lax.broadcasted_iota(jnp.int32, (GRID_W, LANES), 1)
    ck = lane & (GRID_W - 1)
    right = lane >= GRID_W
    dcm = jnp.clip(ck - cq, -(WIN_COLS - 1), WIN_COLS - 1) + (WIN_COLS - 1)
    cs = jnp.clip(cq - WIN_COLS // 2, 0, GRID_W - WIN_COLS)
    valid = (ck >= cs) & (ck < cs + WIN_COLS)
    for d in range(N_BIAS_TILES):
        acc = jnp.zeros((GRID_W, LANES), F32)
        for dc in range(N_DC):
            v = jnp.where(right, rpb_ref[base + (d + 1) * N_DC + dc], rpb_ref[base + d * N_DC + dc])
            acc = jnp.where(dcm == dc, v, acc)
        o_ref[d] = jnp.where(valid, acc, -jnp.inf)


def _rpb_tiles(rpb):
    return pl.pallas_call(
        _rpb_kernel,
        grid=(DEPTH, N_HEADS),
        in_specs=[pl.BlockSpec(memory_space=pltpu.SMEM)],
        out_specs=pl.BlockSpec((None, None, N_BIAS_TILES, GRID_W, LANES), lambda l, h: (l, h, 0, 0, 0)),
        out_shape=jax.ShapeDtypeStruct((DEPTH, N_HEADS, N_BIAS_TILES, GRID_W, LANES), F32),
        name="rpb_tiles",
    )(rpb.reshape(-1))


def _shift_rows(a, s):
    return a if s == 0 else pltpu.roll(a, s % a.shape[0], 0)


def _pad_rows(a):
    return jnp.concatenate([a, jnp.zeros((SEQ_PAD, a.shape[1]), a.dtype)], axis=0)


def _softmax_parts(s):
    m = jnp.max(s, axis=-1, keepdims=True)
    e = jnp.exp(s - m)
    return e, jnp.sum(e, axis=-1, keepdims=True)


def _context_attention(qkv_scr, m_scr, n):
    left = lax.broadcasted_iota(jnp.int32, (n, LANES), 1) < HEAD_DIM
    for p in range(N_PAIRS):
        q = qkv_scr[p]
        k = qkv_scr[N_PAIRS + p]
        v = qkv_scr[2 * N_PAIRS + p]
        zero = jnp.zeros_like(q)
        outs = []
        for qh in (jnp.where(left, q, zero), jnp.where(left, zero, q)):
            s = lax.dot_general(qh, k, NT_DIMS, preferred_element_type=F32)
            e, l = _softmax_parts(s)
            outs.append(jnp.dot(e.astype(BF16), v, preferred_element_type=F32) / l)
        m_scr[p] = jnp.where(left, outs[0], outs[1]).astype(BF16)


def _window_start(qr):
    return min(max(qr - WIN_ROWS // 2, 0), GRID_ROWS - WIN_ROWS)


def _latent_groups():
    groups = []
    for r0 in range(0, GRID_ROWS, LAT_GROUP_ROWS):
        k0 = _window_start(r0)
        k1 = _window_start(r0 + LAT_GROUP_ROWS - 1) + WIN_ROWS
        nk = k1 - k0 + (k1 - k0) % 2
        k0 = min(k0, GRID_ROWS - nk)
        groups.append((r0, LAT_GROUP_ROWS, k0, nk))
    return groups


def _group_bias(bias_ref, p, r0, nq, k0, nk):
    left = lax.broadcasted_iota(jnp.int32, (GRID_W, LANES), 1) < HEAD_DIM
    neg = jnp.full((GRID_W, LANES), -jnp.inf, F32)
    blocks = []
    for hh in range(2):
        for qr in range(r0, r0 + nq):
            rs = _window_start(qr)
            pieces = []
            for kr in range(k0, k0 + nk, 2):
                in0 = rs <= kr < rs + WIN_ROWS
                in1 = rs <= kr + 1 < rs + WIN_ROWS
                d = kr - qr + (WIN_ROWS - 1)
                if in0 and in1:
                    pieces.append(bias_ref[2 * p + hh, d])
                elif in0:
                    pieces.append(jnp.where(left, bias_ref[2 * p + hh, d], neg))
                elif in1:
                    pieces.append(jnp.where(left, neg, bias_ref[2 * p + hh, d]))
                else:
                    pieces.append(neg)
            blocks.append(jnp.concatenate(pieces, axis=1))
    return jnp.concatenate(blocks, axis=0)


def _latent_attention(qkv_scr, kc_scr, vc_scr, bias_ref, m_scr):
    def body(p, carry):
        for r0, nq, k0, nk in _latent_groups():
            rows = nq * GRID_W
            left = lax.broadcasted_iota(jnp.int32, (rows, LANES), 1) < HEAD_DIM
            q = qkv_scr[p, r0 * GRID_W:r0 * GRID_W + rows, :]
            zero = jnp.zeros_like(q)
            qm = jnp.concatenate([jnp.where(left, q, zero), jnp.where(left, zero, q)], axis=0)
            kl = qkv_scr[N_PAIRS + p, k0 * GRID_W:(k0 + nk) * GRID_W, :]
            vl = qkv_scr[2 * N_PAIRS + p, k0 * GRID_W:(k0 + nk) * GRID_W, :]
            s_loc = (lax.dot_general(qm, kl, NT_DIMS, preferred_element_type=F32)
                     + _group_bias(bias_ref, p, r0, nq, k0, nk))
            s_ctx = lax.dot_general(qm, kc_scr[p], NT_DIMS, preferred_element_type=F32)
            m = jnp.maximum(jnp.max(s_loc, axis=-1, keepdims=True), jnp.max(s_ctx, axis=-1, keepdims=True))
            e_loc = jnp.exp(s_loc - m)
            e_ctx = jnp.exp(s_ctx - m)
            l = jnp.sum(e_loc, axis=-1, keepdims=True) + jnp.sum(e_ctx, axis=-1, keepdims=True)
            o = (jnp.dot(e_loc.astype(BF16), vl, preferred_element_type=F32)
                 + jnp.dot(e_ctx.astype(BF16), vc_scr[p], preferred_element_type=F32)) / l
            m_scr[p, r0 * GRID_W:r0 * GRID_W + rows, :] = jnp.where(left, o[:rows], o[rows:]).astype(BF16)
        return carry

    lax.fori_loop(0, N_PAIRS, body, 0)


def _router_bucket(h_hi, h_lo, wr_hi_ref, wr_lo_ref, br_ref, n):
    logits = (jnp.dot(h_hi, wr_hi_ref[...], preferred_element_type=F32)
              + (jnp.dot(h_lo, wr_hi_ref[...], preferred_element_type=F32)
                 + jnp.dot(h_hi, wr_lo_ref[...], preferred_element_type=F32)))
    sel = jax.nn.sigmoid(logits.T[:N_EXPERTS]) + br_ref[...]
    rows = [sel[e:e + 1, :] for e in range(N_EXPERTS)]
    gscore = []
    for g in range(N_GROUPS):
        r4 = rows[g * GROUP_SIZE:(g + 1) * GROUP_SIZE]
        best = None
        for i in range(GROUP_SIZE):
            for j in range(i + 1, GROUP_SIZE):
                s = r4[i] + r4[j]
                best = s if best is None else jnp.maximum(best, s)
        gscore.append(best)
    gbest = jnp.zeros((1, n), jnp.int32)
    gval = gscore[0]
    for g in range(1, N_GROUPS):
        upd = gscore[g] > gval
        gbest = jnp.where(upd, g, gbest)
        gval = jnp.where(upd, gscore[g], gval)
    picked = []
    for e in range(N_EXPERTS):
        g = e // GROUP_SIZE
        rank = jnp.zeros((1, n), jnp.int32)
        for j in range(g * GROUP_SIZE, (g + 1) * GROUP_SIZE):
            if j == e:
                continue
            ahead = (rows[j] > rows[e]) | (rows[j] == rows[e]) if j < e else rows[j] > rows[e]
            rank = rank + ahead.astype(jnp.int32)
        picked.append((rank < 2) & (gbest == g))
    e_lo = jnp.zeros((1, n), jnp.int32)
    e_hi = jnp.zeros((1, n), jnp.int32)
    for e in range(N_EXPERTS):
        e_hi = jnp.where(picked[e], e, e_hi)
        e_lo = jnp.where(picked[N_EXPERTS - 1 - e], N_EXPERTS - 1 - e, e_lo)
    a = e_lo - GROUP_SIZE * gbest
    b = e_hi - GROUP_SIZE * gbest
    return gbest * PAIRS_PER_GROUP + jnp.right_shift(a * (2 * GROUP_SIZE - 1 - a), 1) + (b - a - 1)


def _slab(j, n):
    return pl.ds(j, n, stride=ROW_SLABS)


def _rows_to_2d(ref, n):
    return jnp.concatenate([ref[_slab(j, n), :] for j in range(ROW_SLABS)], axis=1)


def _store_slabs(ref, val, n, first_row=0):
    for j in range(ROW_SLABS):
        ref[_slab(first_row + j, n), :] = val[:, j * LANES:(j + 1) * LANES]


def _mixer_kernel(*refs, latent, has_prev, n):
    it = iter(refs)
    x_ref, mod_ref = next(it), next(it)
    if has_prev:
        y_ref, modp_ref = next(it), next(it)
        if not latent:
            kold_ref, vold_ref = next(it), next(it)
    if latent:
        ck_ref, cv_ref, bias_ref = next(it), next(it), next(it)
    (n1g_ref, w_in_ref, w_out_ref, poolw_ref, pscale_ref, convw_ref, convb_ref,
     lng_ref, lnb_ref, pww_ref, pwb_ref) = [next(it) for _ in range(11)]
    x1_ref = next(it)
    if not latent:
        knew_ref, vnew_ref = next(it), next(it)
    qkv_scr, u_scr, m_scr = next(it), next(it), next(it)
    if latent:
        kc_scr, vc_scr = next(it), next(it)

    x = x_ref[...]
    if has_prev:
        x = x + modp_ref[5:6, :] * _rows_to_2d(y_ref, n)
    layer = 0
    if has_prev and not latent:
        layer = kold_ref.shape[0]
        knew_ref[0:layer] = kold_ref[...]
        vnew_ref[0:layer] = vold_ref[...]
    sh1, sc1, g1 = mod_ref[0:1, :], mod_ref[1:2, :], mod_ref[2:3, :]

    hb = (_rms(x, n1g_ref[...]) * (1.0 + sc1) + sh1).astype(BF16)
    cw = 2 * LANES
    for j in range(D_IN // cw):
        zj = jnp.dot(hb, w_in_ref[:, j * cw:(j + 1) * cw], preferred_element_type=F32)
        if j < 6:
            part = j // 2
            zs = zj * (HEAD_DIM ** -0.5) if part == 0 else zj
            c = part * N_PAIRS + 2 * (j % 2)
            qkv_scr[c] = zs[:, :LANES].astype(BF16)
            qkv_scr[c + 1] = zs[:, LANES:].astype(BF16)
            if not latent and part == 1:
                knew_ref[layer, :, (j % 2) * cw:(j % 2 + 1) * cw] = zj
            if not latent and part == 2:
                vnew_ref[layer, :, (j % 2) * cw:(j % 2 + 1) * cw] = zj
        else:
            u_scr[j - 6] = zj

    if latent:
        for p in range(N_PAIRS):
            kc_scr[p] = ck_ref[:, p * LANES:(p + 1) * LANES].astype(BF16)
            vc_scr[p] = cv_ref[:, p * LANES:(p + 1) * LANES].astype(BF16)
        _latent_attention(qkv_scr, kc_scr, vc_scr, bias_ref, m_scr)
    else:
        _context_attention(qkv_scr, m_scr, n)

    row = lax.broadcasted_iota(jnp.int32, (n, D_POOL), 0)
    grp = lax.broadcasted_iota(jnp.int32, (n, D_POOL), 1) // (D_POOL // 4)
    up = u_scr[0]
    t2 = _pad_rows(up)
    t2 = t2 + _shift_rows(t2, 1)
    t4 = t2 + _shift_rows(t2, 2)
    t8 = t4 + _shift_rows(t4, 4)
    t16 = t8 + _shift_rows(t8, 8)
    s2, s4, s8, s16 = t2[:n], _shift_rows(t4, -1)[:n], _shift_rows(t8, -3)[:n], _shift_rows(t16, -7)[:n]
    wsum = jnp.where(grp == 0, s2, jnp.where(grp == 1, s4, jnp.where(grp == 2, s8, s16)))
    half = jnp.left_shift(1, grp)
    lo = jnp.maximum(row - half, 0)
    hi = jnp.minimum(row + half - 1, n - 1)
    diff = wsum / (hi - lo + 1).astype(F32) - up
    pooled = jnp.dot(diff.astype(BF16), poolw_ref[...], preferred_element_type=F32) * pscale_ref[...]
    m_scr[N_PAIRS] = pooled[:, :LANES].astype(BF16)
    m_scr[N_PAIRS + 1] = pooled[:, LANES:].astype(BF16)

    y = _pad_rows(u_scr[1] * jax.nn.sigmoid(u_scr[2]))
    acc = jnp.zeros(y.shape, F32) + convb_ref[...]
    for k in range(CONV_WIDTH):
        acc = acc + convw_ref[k:k + 1, :] * _shift_rows(y, CONV_WIDTH // 2 - k)
    acc = acc[:n]
    mu = jnp.mean(acc, axis=-1, keepdims=True)
    cen = acc - mu
    var = jnp.mean(cen * cen, axis=-1, keepdims=True)
    yn = _silu(cen * lax.rsqrt(var + EPS) * lng_ref[...] + lnb_ref[...])
    conv = jnp.dot(yn.astype(BF16), pww_ref[...], preferred_element_type=F32) + pwb_ref[...]
    m_scr[N_PAIRS + 2] = conv[:, :LANES].astype(BF16)
    m_scr[N_PAIRS + 3] = conv[:, LANES:].astype(BF16)

    mixed = jnp.concatenate([m_scr[j] for j in range(ROW_SLABS)], axis=1)
    x1_ref[...] = x + g1 * jnp.dot(mixed, w_out_ref[...], preferred_element_type=F32)


def _route_kernel(xp_ref, xs_ref, mod_ref, n2g_ref, wr_hi_ref, wr_lo_ref, br_ref,
                  h_ref, meta_ref, count_ref, run_scr, hi_scr, lo_scr):
    i = pl.program_id(0)
    n = ROUTE_TM

    @pl.when(i == 0)
    def _():
        run_scr[...] = jnp.zeros_like(run_scr)

    def norm_rows(x_ref):
        gain, scale, shift = n2g_ref[...], 1.0 + mod_ref[4:5, :], mod_ref[3:4, :]
        for r0 in range(0, n, ROUTE_CHUNK):
            h2 = _rms(x_ref[r0:r0 + ROUTE_CHUNK, :], gain) * scale + shift
            _store_slabs(h_ref, h2, ROUTE_CHUNK, r0 * ROW_SLABS)
            hi = h2.astype(BF16)
            hi_scr[r0:r0 + ROUTE_CHUNK, :] = hi
            lo_scr[r0:r0 + ROUTE_CHUNK, :] = (h2 - hi.astype(F32)).astype(BF16)

    pl.when(i < CTX_ROUTE_BLOCKS)(lambda: norm_rows(xp_ref))
    pl.when(i >= CTX_ROUTE_BLOCKS)(lambda: norm_rows(xs_ref))

    bucket = _router_bucket(hi_scr[...], lo_scr[...], wr_hi_ref, wr_lo_ref, br_ref, n)
    onehot = lax.broadcasted_iota(jnp.int32, (BUCKET_ROWS, n), 0) == bucket
    upper = (lax.broadcasted_iota(jnp.int32, (n, n), 0) <= lax.broadcasted_iota(jnp.int32, (n, n), 1))
    incl = jnp.dot(onehot.astype(BF16), upper.astype(BF16), preferred_element_type=F32)
    run = run_scr[...]
    rank = jnp.sum(jnp.where(onehot, incl + run, 0.0), axis=0, keepdims=True) - 1.0
    run = run + jnp.sum(onehot.astype(F32), axis=1, keepdims=True)
    run_scr[...] = run
    meta_ref[...] = jnp.concatenate(
        [bucket, rank.astype(jnp.int32), jnp.zeros((META_ROWS - 2, n), jnp.int32)], axis=0)
    count_ref[...] = jnp.broadcast_to(run, count_ref.shape).astype(jnp.int32)


def _route(x1p, x1s, mod_l, n2g, wr_hi, wr_lo, br):
    per_seq = DEC_SEQ // ROUTE_TM
    last_ctx = CTX_ROUTE_BLOCKS - 1
    return pl.pallas_call(
        _route_kernel,
        grid=(N_TOKENS // ROUTE_TM,),
        in_specs=[
            pl.BlockSpec((ROUTE_TM, D_MODEL), lambda i: (jnp.minimum(i, last_ctx), 0)),
            pl.BlockSpec((ROUTE_TM, D_MODEL), lambda i: (jnp.maximum(i - CTX_ROUTE_BLOCKS, 0), 0)),
            pl.BlockSpec((None, 6, D_MODEL),
                         lambda i: (jnp.where(i < CTX_ROUTE_BLOCKS, 0, (i - CTX_ROUTE_BLOCKS) // per_seq + 1), 0, 0)),
            pl.BlockSpec((1, D_MODEL), lambda i: (0, 0)),
            pl.BlockSpec((D_MODEL, LANES), lambda i: (0, 0)),
            pl.BlockSpec((D_MODEL, LANES), lambda i: (0, 0)),
            pl.BlockSpec((N_EXPERTS, 1), lambda i: (0, 0)),
        ],
        out_specs=[pl.BlockSpec((ROUTE_TM * ROW_SLABS, LANES), lambda i: (i, 0)),
                   pl.BlockSpec((META_ROWS, ROUTE_TM), lambda i: (0, i)),
                   pl.BlockSpec((BUCKET_ROWS, LANES), lambda i: (0, 0))],
        out_shape=[jax.ShapeDtypeStruct((N_TOKENS * ROW_SLABS, LANES), F32),
                   jax.ShapeDtypeStruct((META_ROWS, N_TOKENS), jnp.int32),
                   jax.ShapeDtypeStruct((BUCKET_ROWS, LANES), jnp.int32)],
        scratch_shapes=[pltpu.VMEM((BUCKET_ROWS, 1), F32),
                        pltpu.VMEM((ROUTE_TM, D_MODEL), BF16),
                        pltpu.VMEM((ROUTE_TM, D_MODEL), BF16)],
        compiler_params=pltpu.CompilerParams(dimension_semantics=("arbitrary",)),
        name="norm_route",
    )(x1p, x1s, mod_l, n2g, wr_hi, wr_lo, br)


def _const_spec(shape):
    nd = len(shape)
    return pl.BlockSpec(shape, lambda b: (0,) * nd, pipeline_mode=pl.Buffered(1))


def _stream_specs(latent, single_buffer_rows=False):
    n = DEC_SEQ if latent else SEQ
    first = BATCH * SEQ // n if latent else 0
    own = pl.BlockSpec((n, D_MODEL), lambda b: (b, 0))
    rows = pl.BlockSpec((n * ROW_SLABS, LANES), lambda b: (first + b, 0),
                        pipeline_mode=pl.Buffered(1) if single_buffer_rows else None)
    mod = pl.BlockSpec((None, 6, D_MODEL), (lambda b: (b + 1, 0, 0)) if latent else (lambda b: (0, 0, 0)))
    return n, own, rows, mod


def _mixer(x, mod_l, weights, *, latent, prev=None, kv_old=None, cache=None, bias=None, layer=0):
    n, own_spec, rows_spec, mod_spec = _stream_specs(latent)
    tok = x.shape[0]
    nb = tok // n
    in_specs = [own_spec, mod_spec]
    args = [x, mod_l]
    if prev is not None:
        in_specs += [rows_spec, mod_spec]
        args += list(prev)
        if not latent:
            in_specs += [pl.BlockSpec((None, layer, n, D_ATTN), lambda b: (b, 0, 0, 0))] * 2
            args += list(kv_old)
    if latent:
        cache_spec = pl.BlockSpec((None, None, PAST_LEN, D_ATTN), lambda b: (b, layer, 0, 0),
                                  pipeline_mode=pl.Buffered(1))
        in_specs += [cache_spec, cache_spec, _const_spec(bias.shape)]
        args += [cache[0], cache[1], bias]
    in_specs += [_const_spec(w.shape) for w in weights]
    args += list(weights)
    out_shape = [jax.ShapeDtypeStruct((tok, D_MODEL), F32)]
    out_specs = [own_spec]
    if not latent:
        out_shape += [jax.ShapeDtypeStruct((nb, layer + 1, n, D_ATTN), F32)] * 2
        out_specs += [pl.BlockSpec((None, layer + 1, n, D_ATTN), lambda b: (b, 0, 0, 0))] * 2
    scratch = [pltpu.VMEM((3 * N_PAIRS, n, LANES), BF16),
               pltpu.VMEM((3, n, D_POOL), F32),
               pltpu.VMEM((ROW_SLABS, n, LANES), BF16)]
    if latent:
        scratch += [pltpu.VMEM((N_PAIRS, PAST_LEN, LANES), BF16),
                    pltpu.VMEM((N_PAIRS, PAST_LEN, LANES), BF16)]
    return pl.pallas_call(
        functools.partial(_mixer_kernel, latent=latent, has_prev=prev is not None, n=n),
        grid=(nb,),
        in_specs=in_specs,
        out_specs=out_specs,
        out_shape=out_shape,
        scratch_shapes=scratch,
        compiler_params=pltpu.CompilerParams(dimension_semantics=("arbitrary",),
                                             vmem_limit_bytes=VMEM_LIMIT),
        name="mixer_latent" if latent else "mixer_context",
    )(*args)


def _bucket_tables():
    lo, hi = [], []
    for g in range(N_GROUPS):
        for a in range(GROUP_SIZE):
            for b in range(a + 1, GROUP_SIZE):
                lo.append(g * GROUP_SIZE + a)
                hi.append(g * GROUP_SIZE + b)
    return np.asarray(lo, np.int32), np.asarray(hi, np.int32)


def _route_plan(bucket, rank, count):
    tiles_b = (count + MOE_TM - 1) // MOE_TM
    tile_end = jnp.cumsum(tiles_b)
    tile_start = tile_end - tiles_b
    pos = tile_start[bucket] * MOE_TM + rank
    n_used = tile_end[-1]
    fill_lo = jnp.concatenate([tile_start * MOE_TM + count, (n_used * MOE_TM)[None]])
    fill_hi = jnp.concatenate([tile_end * MOE_TM, jnp.full((1,), MOE_TILES * MOE_TM, jnp.int32)])
    tile = jnp.arange(MOE_TILES, dtype=jnp.int32)
    tile_b = jnp.sum((tile[:, None] >= tile_end[None, :]).astype(jnp.int32), axis=1)
    last_b = jnp.sum((n_used - 1 >= tile_end).astype(jnp.int32))
    tile_b = jnp.minimum(tile_b, last_b)
    n_valid = jnp.clip(count[tile_b] - (tile - tile_start[tile_b]) * MOE_TM, 0, MOE_TM)
    lo_tab, hi_tab = _bucket_tables()
    i32 = lambda a: a.astype(jnp.int32)
    return (i32(pos), i32(fill_lo), i32(fill_hi)), (
        jnp.asarray(lo_tab)[tile_b], jnp.asarray(hi_tab)[tile_b], i32(n_valid), i32(n_used).reshape(1))


def _permute_kernel(pos_ref, lo_ref, hi_ref, h_hbm, hs_hbm, tok_ref, buf, lsem, ssem, fsem):
    i = pl.program_id(0)
    last = pl.num_programs(0) - 1
    half = i % 2
    blk = PERM_TM * ROW_SLABS

    def load(block, h):
        return pltpu.make_async_copy(h_hbm.at[pl.ds(block * blk, blk), :], buf.at[pl.ds(h * blk, blk), :], lsem.at[h])

    def wait_rows(sem, rows):
        n = rows * ROW_SLABS
        pltpu.make_async_copy(hs_hbm.at[pl.ds(0, n), :], hs_hbm.at[pl.ds(0, n), :], sem).wait()

    @pl.when(i == 0)
    def _():
        load(0, 0).start()

    load(i, half).wait()

    @pl.when(i >= 1)
    def _():
        wait_rows(ssem.at[1 - half], PERM_TM)

    @pl.when(i < last)
    def _():
        load(i + 1, 1 - half).start()

    base = i * PERM_TM
    for r in range(PERM_TM):
        p = pos_ref[base + r]
        pltpu.make_async_copy(_row_slabs(buf, half * PERM_TM + r), _row_slabs(hs_hbm, p),
                              ssem.at[half]).start(priority=r % 2)
        tok_ref[p] = base + r

    @pl.when(i == 0)
    def _():
        def fill(first, rows):
            n = rows * ROW_SLABS
            pltpu.make_async_copy(buf.at[pl.ds(0, n), :],
                                  hs_hbm.at[pl.ds(pl.multiple_of(first * ROW_SLABS, ROW_SLABS), n), :], fsem).start()

        def zero_tok(p, carry):
            tok_ref[p] = 0
            return carry

        for b in range(N_BUCKETS + 1):
            lax.fori_loop(lo_ref[b], hi_ref[b], zero_tok, 0)
        for b in range(N_BUCKETS):
            first = lo_ref[b]
            length = hi_ref[b] - first
            run = MOE_TM // 2
            while run >= 1:
                take = (length & run) != 0
                pl.when(take)(functools.partial(fill, first, run))
                first = first + jnp.where(take, run, 0)
                run //= 2

        def fill_tile(t, carry):
            fill(t * MOE_TM, MOE_TM)
            return carry

        lax.fori_loop(lo_ref[N_BUCKETS] // MOE_TM, MOE_TILES, fill_tile, 0)
        wait_rows(fsem, MOE_TILES * MOE_TM - N_TOKENS)

    @pl.when(i == last)
    def _():
        wait_rows(ssem.at[half], PERM_TM)


def _permute(h, pos, fill_lo, fill_hi):
    grid_spec = pltpu.PrefetchScalarGridSpec(
        num_scalar_prefetch=3,
        grid=(N_TOKENS // PERM_TM,),
        in_specs=[pl.BlockSpec(memory_space=pl.ANY)],
        out_specs=[pl.BlockSpec(memory_space=pl.ANY), pl.BlockSpec(memory_space=pltpu.SMEM)],
        scratch_shapes=[pltpu.VMEM((2 * PERM_TM * ROW_SLABS, LANES), F32),
                        pltpu.SemaphoreType.DMA((2,)),
                        pltpu.SemaphoreType.DMA((2,)),
                        pltpu.SemaphoreType.DMA(())],
    )
    return pl.pallas_call(
        _permute_kernel,
        grid_spec=grid_spec,
        out_shape=[jax.ShapeDtypeStruct((MOE_TILES * MOE_TM * ROW_SLABS, LANES), F32),
                   jax.ShapeDtypeStruct((MOE_TILES * MOE_TM,), jnp.int32)],
        compiler_params=pltpu.CompilerParams(dimension_semantics=("arbitrary",)),
        name="permute_rows",
    )(pos, fill_lo, fill_hi, h)


def _row_slabs(ref, row):
    return ref.at[pl.ds(pl.multiple_of(row * ROW_SLABS, ROW_SLABS), ROW_SLABS), :]


def _moe_kernel(e0_ref, e1_ref, nv_ref, nu_ref, tok_ref, hs_ref, wrt_ref, wg0_ref, wu0_ref, wd0_ref,
                wg1_ref, wu1_ref, wd1_ref, y_hbm, xb_scr, stg, ssem):
    i = pl.program_id(0)
    n_used = nu_ref[0]
    half = i % 2
    chunks = [(k * MOE_TM // MOE_PHASES, (k + 1) * MOE_TM // MOE_PHASES) for k in range(MOE_PHASES)]

    def scatter_row(tile, r, priority):
        pltpu.make_async_copy(_row_slabs(stg, (tile % 2) * MOE_TM + r), _row_slabs(y_hbm, tok_ref[tile * MOE_TM + r]),
                              ssem.at[tile % 2]).start(priority=priority)

    def scatter_rows(tile, lo, hi):
        n_valid = nv_ref[tile]

        @pl.when(n_valid >= hi)
        def _():
            for r in range(lo, hi):
                scatter_row(tile, r, r % 2)

        @pl.when(n_valid < hi)
        def _():
            def one(r, carry):
                scatter_row(tile, r, 0)
                return carry

            lax.fori_loop(lo, jnp.maximum(n_valid, lo), one, 0)

    def wait_scatter(tile):
        n = nv_ref[tile] * ROW_SLABS
        pltpu.make_async_copy(y_hbm.at[pl.ds(0, n), :], y_hbm.at[pl.ds(0, n), :], ssem.at[tile % 2]).wait()

    def scatter_previous(k):
        @pl.when(i >= 1)
        def _():
            scatter_rows(jnp.maximum(i - 1, 0), *chunks[k])

    @pl.when(i < n_used)
    def _():
        w0 = wrt_ref[pl.ds(e0_ref[i], 1), :]
        w1 = wrt_ref[pl.ds(e1_ref[i], 1), :]
        l0 = jnp.zeros((MOE_TM, LANES), F32)
        l1 = jnp.zeros((MOE_TM, LANES), F32)
        for j in range(ROW_SLABS):
            xj = hs_ref[_slab(j, MOE_TM), :]
            xb_scr[:, j * LANES:(j + 1) * LANES] = xj.astype(BF16)
            l0 = l0 + xj * w0[:, j * LANES:(j + 1) * LANES]
            l1 = l1 + xj * w1[:, j * LANES:(j + 1) * LANES]
        a0 = jax.nn.sigmoid(jnp.sum(l0, axis=1, keepdims=True))
        a1 = jax.nn.sigmoid(jnp.sum(l1, axis=1, keepdims=True))
        den = a0 + a1
        acc = None
        phase = 0
        for gate, wg_ref, wu_ref, wd_ref in ((a0 / den, wg0_ref, wu0_ref, wd0_ref),
                                             (a1 / den, wg1_ref, wu1_ref, wd1_ref)):
            a = jnp.dot(xb_scr[...], wg_ref[...], preferred_element_type=F32)
            scatter_previous(phase)
            b = jnp.dot(xb_scr[...], wu_ref[...], preferred_element_type=F32)
            hid = (_silu(a) * b * gate).astype(BF16)
            scatter_previous(phase + 1)
            o = jnp.dot(hid, wd_ref[...], preferred_element_type=F32)
            acc = o if acc is None else acc + o
            scatter_previous(phase + 2)
            phase += 3

        @pl.when(i >= 2)
        def _():
            wait_scatter(jnp.maximum(i - 2, 0))

        @pl.when(half == 0)
        def _():
            _store_slabs(stg, acc, MOE_TM, 0)

        @pl.when(half == 1)
        def _():
            _store_slabs(stg, acc, MOE_TM, MOE_TM * ROW_SLABS)

        @pl.when(i == n_used - 1)
        def _():
            for lo, hi in chunks:
                scatter_rows(i, lo, hi)
            wait_scatter(i)

            @pl.when(i >= 1)
            def _():
                wait_scatter(jnp.maximum(i - 1, 0))


def _moe(hs, row_tok, tiles, wrt, wg, wu, wd, layer):
    e0, e1, n_valid, n_used = tiles

    def w_spec(which, shape):
        return pl.BlockSpec((None, None) + shape,
                            lambda i, e0, e1, nv, nu, tok: (layer, (e0, e1)[which][i], 0, 0))

    up_shape, down_shape = (D_MODEL, D_FF), (D_FF, D_MODEL)
    grid_spec = pltpu.PrefetchScalarGridSpec(
        num_scalar_prefetch=5,
        grid=(MOE_TILES,),
        in_specs=[pl.BlockSpec((MOE_TM * ROW_SLABS, LANES),
                               lambda i, e0, e1, nv, nu, tok: (jnp.minimum(i, nu[0] - 1), 0)),
                  pl.BlockSpec((N_EXPERTS, D_MODEL), lambda i, *_: (0, 0)),
                  w_spec(0, up_shape), w_spec(0, up_shape), w_spec(0, down_shape),
                  w_spec(1, up_shape), w_spec(1, up_shape), w_spec(1, down_shape)],
        out_specs=pl.BlockSpec(memory_space=pl.ANY),
        scratch_shapes=[pltpu.VMEM((MOE_TM, D_MODEL), BF16),
                        pltpu.VMEM((2 * MOE_TM * ROW_SLABS, LANES), F32),
                        pltpu.SemaphoreType.DMA((2,))],
    )
    return pl.pallas_call(
        _moe_kernel,
        grid_spec=grid_spec,
        out_shape=jax.ShapeDtypeStruct((N_TOKENS * ROW_SLABS, LANES), F32),
        compiler_params=pltpu.CompilerParams(dimension_semantics=("arbitrary",),
                                             vmem_limit_bytes=VMEM_LIMIT),
        name="moe_routed",
    )(e0, e1, n_valid, n_used, row_tok, hs, wrt, wg, wu, wd, wg, wu, wd)


def _final_kernel(x1_ref, y_ref, mod_ref, fg_ref, o_ref):
    x2 = x1_ref[...] + mod_ref[5:6, :] * _rows_to_2d(y_ref, x1_ref.shape[0])
    o_ref[...] = _rms(x2, fg_ref[...])


def _final(x1, y, mod_l, fg, *, latent):
    n, own_spec, rows_spec, mod_spec = _stream_specs(latent)
    return pl.pallas_call(
        _final_kernel,
        grid=(x1.shape[0] // n,),
        in_specs=[own_spec, rows_spec, mod_spec, pl.BlockSpec((1, D_MODEL), lambda b: (0, 0))],
        out_specs=own_spec,
        out_shape=jax.ShapeDtypeStruct(x1.shape, F32),
        name="final_norm",
    )(x1, y, mod_l, fg)


def kernel(x_prompt, x_sample, cache_k, cache_v, c, c_ctx, w_ada, b_ada, norm1_g, norm2_g, w_in, w_out, rpb, pool_w, pool_scale, conv_w, conv_b, conv_ln_g, conv_ln_b, conv_pw_w, conv_pw_b, w_router, b_router, moe_w_gate, moe_w_up, moe_w_down, final_g):
    xp = x_prompt.reshape(BATCH * SEQ, D_MODEL)
    xs = x_sample.reshape(DEC_BATCH * DEC_SEQ, D_MODEL)
    cond = jnp.concatenate([c_ctx[None, :], c, jnp.zeros((N_MOD_ROWS - 1 - DEC_BATCH, D_MODEL), F32)], axis=0)
    mod = _ada_modulation(cond, w_ada, b_ada).reshape(DEPTH, N_MOD_ROWS, 6, D_MODEL)
    bias = _rpb_tiles(rpb)
    ck = cache_k.reshape(DEC_BATCH, DEPTH, PAST_LEN, D_ATTN)
    cv = cache_v.reshape(DEC_BATCH, DEPTH, PAST_LEN, D_ATTN)
    wrt = w_router.T
    wr = jnp.pad(w_router, ((0, 0), (0, LANES - N_EXPERTS)))
    wr_hi = wr.astype(BF16)
    wr_lo = (wr - wr_hi.astype(F32)).astype(BF16)
    br = b_router.reshape(N_EXPERTS, 1)
    fg = final_g.reshape(1, D_MODEL)
    wg, wu, wd = moe_w_gate.astype(BF16), moe_w_up.astype(BF16), moe_w_down.astype(BF16)
    new_k = new_v = None
    prev = None
    for l in range(DEPTH):
        pool_bd = jnp.zeros((D_POOL, D_POOL), F32)
        gd = D_POOL // 4
        for g in range(4):
            pool_bd = pool_bd.at[g * gd:(g + 1) * gd, g * gd:(g + 1) * gd].set(pool_w[l, g])
        weights = [norm1_g[l].reshape(1, -1),
                   w_in[l].astype(BF16), w_out[l].astype(BF16),
                   pool_bd.astype(BF16), pool_scale[l].reshape(1, -1),
                   conv_w[l], conv_b[l].reshape(1, -1),
                   conv_ln_g[l].reshape(1, -1), conv_ln_b[l].reshape(1, -1),
                   conv_pw_w[l].astype(BF16), conv_pw_b[l].reshape(1, -1)]

        xp, new_k, new_v = _mixer(xp, mod[l], weights, latent=False, prev=prev, kv_old=(new_k, new_v), layer=l)
        xs, = _mixer(xs, mod[l], weights, latent=True, prev=prev, cache=(ck, cv), bias=bias[l], layer=l)

        h, meta, count = _route(xp, xs, mod[l], norm2_g[l].reshape(1, -1), wr_hi, wr_lo, br)
        perm, tiles = _route_plan(meta[0], meta[1], count[:N_BUCKETS, 0])
        hs, row_tok = _permute(h, *perm)
        y = _moe(hs, row_tok, tiles, wrt, wg, wu, wd, l)
        prev = (y, mod[l])

    y_prompt = _final(xp, y, mod[DEPTH - 1], fg, latent=False).reshape(BATCH, SEQ, D_MODEL)
    y_sample = _final(xs, y, mod[DEPTH - 1], fg, latent=True).reshape(DEC_BATCH, DEC_SEQ, D_MODEL)
    new_cache_k = new_k.reshape(BATCH, DEPTH, SEQ, N_HEADS, HEAD_DIM)
    new_cache_v = new_v.reshape(BATCH, DEPTH, SEQ, N_HEADS, HEAD_DIM)
    return (y_prompt, y_sample, new_cache_k, new_cache_v)
```

```python
import functools

import numpy as np
import jax
import jax.numpy as jnp
from jax import lax
from jax.experimental import pallas as pl
from jax.experimental.pallas import tpu as pltpu

D_MODEL = 1024
BATCH = 32
SEQ = 256
DEPTH = 2
DEC_BATCH = 8
DEC_SEQ = 1024
PAST_LEN = 512
GRID_W = 64
GRID_ROWS = DEC_SEQ // GRID_W
D_ATTN = 512
HEAD_DIM = 64
N_HEADS = 8
N_PAIRS = N_HEADS // 2
WIN_ROWS = 8
WIN_COLS = 16
D_POOL = 256
D_CONV = 256
CONV_WIDTH = 31
D_IN = 3 * D_ATTN + D_POOL + 2 * D_CONV
N_EXPERTS = 16
N_GROUPS = 4
GROUP_SIZE = N_EXPERTS // N_GROUPS
D_FF = 512
EPS = 1e-6
N_MOD_ROWS = 16
N_DR = 2 * WIN_ROWS - 1
N_DC = 2 * WIN_COLS - 1
N_BIAS_TILES = N_DR - 1
LAT_GROUP_ROWS = 4
SEQ_PAD = 16
LANES = 128
SUBLANES = 8
ROW_SLABS = D_MODEL // LANES
N_TOKENS = BATCH * SEQ + DEC_BATCH * DEC_SEQ
PAIRS_PER_GROUP = GROUP_SIZE * (GROUP_SIZE - 1) // 2
N_BUCKETS = N_GROUPS * PAIRS_PER_GROUP
BUCKET_ROWS = 32
META_ROWS = 8
ROUTE_TM = 512
ROUTE_CHUNK = 32
CTX_ROUTE_BLOCKS = BATCH * SEQ // ROUTE_TM
MOE_TM = 256
MOE_TILES = N_TOKENS // MOE_TM + N_BUCKETS
PERM_TM = ROUTE_TM
DMA_UNROLL = 8
VMEM_LIMIT = 62 * 1024 * 1024

F32 = jnp.float32
BF16 = jnp.bfloat16
HIGHEST = lax.Precision.HIGHEST
NT_DIMS = (((1,), (1,)), ((), ()))


def _rms(x, g):
    return x * lax.rsqrt(jnp.mean(x * x, axis=-1, keepdims=True) + EPS) * g


def _silu(x):
    return x * jax.nn.sigmoid(x)


def _ada_kernel(cond_ref, w_ref, b_ref, o_ref):
    o_ref[...] = jnp.dot(_silu(cond_ref[...]), w_ref[...], preferred_element_type=F32,
                         precision=HIGHEST) + b_ref[...]


def _ada_modulation(cond, w_ada, b_ada):
    n_col = 6 * D_MODEL // D_MODEL
    return pl.pallas_call(
        _ada_kernel,
        grid=(DEPTH, n_col),
        in_specs=[
            pl.BlockSpec((N_MOD_ROWS, D_MODEL), lambda l, n: (0, 0)),
            pl.BlockSpec((None, D_MODEL, D_MODEL), lambda l, n: (l, 0, n)),
            pl.BlockSpec((None, 1, D_MODEL), lambda l, n: (l, 0, n)),
        ],
        out_specs=pl.BlockSpec((None, N_MOD_ROWS, D_MODEL), lambda l, n: (l, 0, n)),
        out_shape=jax.ShapeDtypeStruct((DEPTH, N_MOD_ROWS, 6 * D_MODEL), F32),
        name="ada_modulation",
    )(cond, w_ada, b_ada.reshape(DEPTH, 1, 6 * D_MODEL))


def _rpb_kernel(rpb_ref, o_ref):
    base = (pl.program_id(0) * N_HEADS + pl.program_id(1)) * (N_DR * N_DC)
    cq = lax.broadcasted_iota(jnp.int32, (GRID_W, LANES), 0)
    lane = lax.broadcasted_iota(jnp.int32, (GRID_W, LANES), 1)
    ck = lane & (GRID_W - 1)
    right = lane >= GRID_W
    dcm = jnp.clip(ck - cq, -(WIN_COLS - 1), WIN_COLS - 1) + (WIN_COLS - 1)
    cs = jnp.clip(cq - WIN_COLS // 2, 0, GRID_W - WIN_COLS)
    valid = (ck >= cs) & (ck < cs + WIN_COLS)
    for d in range(N_BIAS_TILES):
        acc = jnp.zeros((GRID_W, LANES), F32)
        for dc in range(N_DC):
            v = jnp.where(right, rpb_ref[base + (d + 1) * N_DC + dc], rpb_ref[base + d * N_DC + dc])
            acc = jnp.where(dcm == dc, v, acc)
        o_ref[d] = jnp.where(valid, acc, -jnp.inf)


def _rpb_tiles(rpb):
    return pl.pallas_call(
        _rpb_kernel,
        grid=(DEPTH, N_HEADS),
        in_specs=[pl.BlockSpec(memory_space=pltpu.SMEM)],
        out_specs=pl.BlockSpec((None, None, N_BIAS_TILES, GRID_W, LANES), lambda l, h: (l, h, 0, 0, 0)),
        out_shape=jax.ShapeDtypeStruct((DEPTH, N_HEADS, N_BIAS_TILES, GRID_W, LANES), F32),
        name="rpb_tiles",
    )(rpb.reshape(-1))


def _shift_rows(a, s):
    return a if s == 0 else pltpu.roll(a, s % a.shape[0], 0)


def _pad_rows(a):
    return jnp.concatenate([a, jnp.zeros((SEQ_PAD, a.shape[1]), a.dtype)], axis=0)


def _softmax_parts(s):
    m = jnp.max(s, axis=-1, keepdims=True)
    e = jnp.exp(s - m)
    return e, jnp.sum(e, axis=-1, keepdims=True)


def _context_attention(qkv_scr, m_scr, n):
    left = lax.broadcasted_iota(jnp.int32, (n, LANES), 1) < HEAD_DIM
    for p in range(N_PAIRS):
        q = qkv_scr[p]
        k = qkv_scr[N_PAIRS + p]
        v = qkv_scr[2 * N_PAIRS + p]
        zero = jnp.zeros_like(q)
        outs = []
        for qh in (jnp.where(left, q, zero), jnp.where(left, zero, q)):
            s = lax.dot_general(qh, k, NT_DIMS, preferred_element_type=F32)
            e, l = _softmax_parts(s)
            outs.append(jnp.dot(e.astype(BF16), v, preferred_element_type=F32) / l)
        m_scr[p] = jnp.where(left, outs[0], outs[1]).astype(BF16)


def _window_start(qr):
    return min(max(qr - WIN_ROWS // 2, 0), GRID_ROWS - WIN_ROWS)


def _latent_groups():
    groups = []
    for r0 in range(0, GRID_ROWS, LAT_GROUP_ROWS):
        k0 = _window_start(r0)
        k1 = _window_start(r0 + LAT_GROUP_ROWS - 1) + WIN_ROWS
        nk = k1 - k0 + (k1 - k0) % 2
        k0 = min(k0, GRID_ROWS - nk)
        groups.append((r0, LAT_GROUP_ROWS, k0, nk))
    return groups


def _group_bias(bias_ref, p, r0, nq, k0, nk):
    left = lax.broadcasted_iota(jnp.int32, (GRID_W, LANES), 1) < HEAD_DIM
    neg = jnp.full((GRID_W, LANES), -jnp.inf, F32)
    blocks = []
    for hh in range(2):
        for qr in range(r0, r0 + nq):
            rs = _window_start(qr)
            pieces = []
            for kr in range(k0, k0 + nk, 2):
                in0 = rs <= kr < rs + WIN_ROWS
                in1 = rs <= kr + 1 < rs + WIN_ROWS
                d = kr - qr + (WIN_ROWS - 1)
                if in0 and in1:
                    pieces.append(bias_ref[2 * p + hh, d])
                elif in0:
                    pieces.append(jnp.where(left, bias_ref[2 * p + hh, d], neg))
                elif in1:
                    pieces.append(jnp.where(left, neg, bias_ref[2 * p + hh, d]))
                else:
                    pieces.append(neg)
            blocks.append(jnp.concatenate(pieces, axis=1))
    return jnp.concatenate(blocks, axis=0)


def _latent_attention(qkv_scr, kc_scr, vc_scr, bias_ref, m_scr):
    def body(p, carry):
        for r0, nq, k0, nk in _latent_groups():
            rows = nq * GRID_W
            left = lax.broadcasted_iota(jnp.int32, (rows, LANES), 1) < HEAD_DIM
            q = qkv_scr[p, r0 * GRID_W:r0 * GRID_W + rows, :]
            zero = jnp.zeros_like(q)
            qm = jnp.concatenate([jnp.where(left, q, zero), jnp.where(left, zero, q)], axis=0)
            kl = qkv_scr[N_PAIRS + p, k0 * GRID_W:(k0 + nk) * GRID_W, :]
            vl = qkv_scr[2 * N_PAIRS + p, k0 * GRID_W:(k0 + nk) * GRID_W, :]
            s_loc = (lax.dot_general(qm, kl, NT_DIMS, preferred_element_type=F32)
                     + _group_bias(bias_ref, p, r0, nq, k0, nk))
            s_ctx = lax.dot_general(qm, kc_scr[p], NT_DIMS, preferred_element_type=F32)
            m = jnp.maximum(jnp.max(s_loc, axis=-1, keepdims=True), jnp.max(s_ctx, axis=-1, keepdims=True))
            e_loc = jnp.exp(s_loc - m)
            e_ctx = jnp.exp(s_ctx - m)
            l = jnp.sum(e_loc, axis=-1, keepdims=True) + jnp.sum(e_ctx, axis=-1, keepdims=True)
            o = (jnp.dot(e_loc.astype(BF16), vl, preferred_element_type=F32)
                 + jnp.dot(e_ctx.astype(BF16), vc_scr[p], preferred_element_type=F32)) / l
            m_scr[p, r0 * GRID_W:r0 * GRID_W + rows, :] = jnp.where(left, o[:rows], o[rows:]).astype(BF16)
        return carry

    lax.fori_loop(0, N_PAIRS, body, 0)


def _router_bucket(h_hi, h_lo, wr_hi_ref, wr_lo_ref, br_ref, n):
    logits = (jnp.dot(h_hi, wr_hi_ref[...], preferred_element_type=F32)
              + (jnp.dot(h_lo, wr_hi_ref[...], preferred_element_type=F32)
                 + jnp.dot(h_hi, wr_lo_ref[...], preferred_element_type=F32)))
    sel = jax.nn.sigmoid(logits.T[:N_EXPERTS]) + br_ref[...]
    rows = [sel[e:e + 1, :] for e in range(N_EXPERTS)]
    gscore = []
    for g in range(N_GROUPS):
        r4 = rows[g * GROUP_SIZE:(g + 1) * GROUP_SIZE]
        best = None
        for i in range(GROUP_SIZE):
            for j in range(i + 1, GROUP_SIZE):
                s = r4[i] + r4[j]
                best = s if best is None else jnp.maximum(best, s)
        gscore.append(best)
    gbest = jnp.zeros((1, n), jnp.int32)
    gval = gscore[0]
    for g in range(1, N_GROUPS):
        upd = gscore[g] > gval
        gbest = jnp.where(upd, g, gbest)
        gval = jnp.where(upd, gscore[g], gval)
    picked = []
    for e in range(N_EXPERTS):
        g = e // GROUP_SIZE
        rank = jnp.zeros((1, n), jnp.int32)
        for j in range(g * GROUP_SIZE, (g + 1) * GROUP_SIZE):
            if j == e:
                continue
            ahead = (rows[j] > rows[e]) | (rows[j] == rows[e]) if j < e else rows[j] > rows[e]
            rank = rank + ahead.astype(jnp.int32)
        picked.append((rank < 2) & (gbest == g))
    e_lo = jnp.zeros((1, n), jnp.int32)
    e_hi = jnp.zeros((1, n), jnp.int32)
    for e in range(N_EXPERTS):
        e_hi = jnp.where(picked[e], e, e_hi)
        e_lo = jnp.where(picked[N_EXPERTS - 1 - e], N_EXPERTS - 1 - e, e_lo)
    a = e_lo - GROUP_SIZE * gbest
    b = e_hi - GROUP_SIZE * gbest
    return gbest * PAIRS_PER_GROUP + jnp.right_shift(a * (2 * GROUP_SIZE - 1 - a), 1) + (b - a - 1)


def _slab(j, n):
    return pl.ds(j, n, stride=ROW_SLABS)


def _rows_to_2d(ref, n):
    return jnp.concatenate([ref[_slab(j, n), :] for j in range(ROW_SLABS)], axis=1)


def _store_slabs(ref, val, n, first_row=0):
    for j in range(ROW_SLABS):
        ref[_slab(first_row + j, n), :] = val[:, j * LANES:(j + 1) * LANES]


def _mixer_kernel(*refs, latent, has_prev, n):
    it = iter(refs)
    x_ref, mod_ref = next(it), next(it)
    if has_prev:
        y_ref, modp_ref = next(it), next(it)
        if not latent:
            kold_ref, vold_ref = next(it), next(it)
    if latent:
        ck_ref, cv_ref, bias_ref = next(it), next(it), next(it)
    (n1g_ref, w_in_ref, w_out_ref, poolw_ref, pscale_ref, convw_ref, convb_ref,
     lng_ref, lnb_ref, pww_ref, pwb_ref) = [next(it) for _ in range(11)]
    x1_ref = next(it)
    if not latent:
        knew_ref, vnew_ref = next(it), next(it)
    qkv_scr, u_scr, m_scr = next(it), next(it), next(it)
    if latent:
        kc_scr, vc_scr = next(it), next(it)

    x = x_ref[...]
    if has_prev:
        x = x + modp_ref[5:6, :] * _rows_to_2d(y_ref, n)
    layer = 0
    if has_prev and not latent:
        layer = kold_ref.shape[0]
        knew_ref[0:layer] = kold_ref[...]
        vnew_ref[0:layer] = vold_ref[...]
    sh1, sc1, g1 = mod_ref[0:1, :], mod_ref[1:2, :], mod_ref[2:3, :]

    hb = (_rms(x, n1g_ref[...]) * (1.0 + sc1) + sh1).astype(BF16)
    cw = 2 * LANES
    for j in range(D_IN // cw):
        zj = jnp.dot(hb, w_in_ref[:, j * cw:(j + 1) * cw], preferred_element_type=F32)
        if j < 6:
            part = j // 2
            zs = zj * (HEAD_DIM ** -0.5) if part == 0 else zj
            c = part * N_PAIRS + 2 * (j % 2)
            qkv_scr[c] = zs[:, :LANES].astype(BF16)
            qkv_scr[c + 1] = zs[:, LANES:].astype(BF16)
            if not latent and part == 1:
                knew_ref[layer, :, (j % 2) * cw:(j % 2 + 1) * cw] = zj
            if not latent and part == 2:
                vnew_ref[layer, :, (j % 2) * cw:(j % 2 + 1) * cw] = zj
        else:
            u_scr[j - 6] = zj

    if latent:
        for p in range(N_PAIRS):
            kc_scr[p] = ck_ref[:, p * LANES:(p + 1) * LANES].astype(BF16)
            vc_scr[p] = cv_ref[:, p * LANES:(p + 1) * LANES].astype(BF16)
        _latent_attention(qkv_scr, kc_scr, vc_scr, bias_ref, m_scr)
    else:
        _context_attention(qkv_scr, m_scr, n)

    row = lax.broadcasted_iota(jnp.int32, (n, D_POOL), 0)
    grp = lax.broadcasted_iota(jnp.int32, (n, D_POOL), 1) // (D_POOL // 4)
    up = u_scr[0]
    t2 = _pad_rows(up)
    t2 = t2 + _shift_rows(t2, 1)
    t4 = t2 + _shift_rows(t2, 2)
    t8 = t4 + _shift_rows(t4, 4)
    t16 = t8 + _shift_rows(t8, 8)
    s2, s4, s8, s16 = t2[:n], _shift_rows(t4, -1)[:n], _shift_rows(t8, -3)[:n], _shift_rows(t16, -7)[:n]
    wsum = jnp.where(grp == 0, s2, jnp.where(grp == 1, s4, jnp.where(grp == 2, s8, s16)))
    half = jnp.left_shift(1, grp)
    lo = jnp.maximum(row - half, 0)
    hi = jnp.minimum(row + half - 1, n - 1)
    diff = wsum / (hi - lo + 1).astype(F32) - up
    pooled = jnp.dot(diff.astype(BF16), poolw_ref[...], preferred_element_type=F32) * pscale_ref[...]
    m_scr[N_PAIRS] = pooled[:, :LANES].astype(BF16)
    m_scr[N_PAIRS + 1] = pooled[:, LANES:].astype(BF16)

    y = _pad_rows(u_scr[1] * jax.nn.sigmoid(u_scr[2]))
    acc = jnp.zeros(y.shape, F32) + convb_ref[...]
    y_rot = [_shift_rows(y, b) for b in range(SUBLANES)]
    for k in range(CONV_WIDTH):
        s = CONV_WIDTH // 2 - k
        acc = acc + convw_ref[k:k + 1, :] * _shift_rows(y_rot[s % SUBLANES], s - s % SUBLANES)
    acc = acc[:n]
    mu = jnp.mean(acc, axis=-1, keepdims=True)
    cen = acc - mu
    var = jnp.mean(cen * cen, axis=-1, keepdims=True)
    yn = _silu(cen * lax.rsqrt(var + EPS) * lng_ref[...] + lnb_ref[...])
    conv = jnp.dot(yn.astype(BF16), pww_ref[...], preferred_element_type=F32) + pwb_ref[...]
    m_scr[N_PAIRS + 2] = conv[:, :LANES].astype(BF16)
    m_scr[N_PAIRS + 3] = conv[:, LANES:].astype(BF16)

    mixed = jnp.concatenate([m_scr[j] for j in range(ROW_SLABS)], axis=1)
    x1_ref[...] = x + g1 * jnp.dot(mixed, w_out_ref[...], preferred_element_type=F32)


def _route_kernel(xp_ref, xs_ref, mod_ref, n2g_ref, wr_hi_ref, wr_lo_ref, br_ref,
                  h_ref, meta_ref, count_ref, block_count_ref, run_scr, hi_scr, lo_scr):
    i = pl.program_id(0)
    n = ROUTE_TM

    @pl.when(i == 0)
    def _():
        run_scr[...] = jnp.zeros_like(run_scr)

    def norm_rows(x_ref):
        gain, scale, shift = n2g_ref[...], 1.0 + mod_ref[4:5, :], mod_ref[3:4, :]
        for r0 in range(0, n, ROUTE_CHUNK):
            h2 = _rms(x_ref[r0:r0 + ROUTE_CHUNK, :], gain) * scale + shift
            _store_slabs(h_ref, h2, ROUTE_CHUNK, r0 * ROW_SLABS)
            hi = h2.astype(BF16)
            hi_scr[r0:r0 + ROUTE_CHUNK, :] = hi
            lo_scr[r0:r0 + ROUTE_CHUNK, :] = (h2 - hi.astype(F32)).astype(BF16)

    pl.when(i < CTX_ROUTE_BLOCKS)(lambda: norm_rows(xp_ref))
    pl.when(i >= CTX_ROUTE_BLOCKS)(lambda: norm_rows(xs_ref))

    bucket = _router_bucket(hi_scr[...], lo_scr[...], wr_hi_ref, wr_lo_ref, br_ref, n)
    onehot = lax.broadcasted_iota(jnp.int32, (BUCKET_ROWS, n), 0) == bucket
    upper = (lax.broadcasted_iota(jnp.int32, (n, n), 0) <= lax.broadcasted_iota(jnp.int32, (n, n), 1))
    incl = jnp.dot(onehot.astype(BF16), upper.astype(BF16), preferred_element_type=F32)
    run = run_scr[...]
    rank = jnp.sum(jnp.where(onehot, incl + run, 0.0), axis=0, keepdims=True) - 1.0
    local_rank = jnp.sum(jnp.where(onehot, incl, 0.0), axis=0, keepdims=True) - 1.0
    in_block = jnp.sum(onehot.astype(F32), axis=1, keepdims=True)
    run = run + in_block
    run_scr[...] = run
    meta_ref[...] = jnp.concatenate(
        [bucket, rank.astype(jnp.int32), local_rank.astype(jnp.int32),
         jnp.zeros((META_ROWS - 3, n), jnp.int32)], axis=0)
    count_ref[...] = jnp.broadcast_to(run, count_ref.shape).astype(jnp.int32)
    block_count_ref[...] = jnp.broadcast_to(in_block, block_count_ref.shape).astype(jnp.int32)


def _route(x1p, x1s, mod_l, n2g, wr_hi, wr_lo, br):
    per_seq = DEC_SEQ // ROUTE_TM
    last_ctx = CTX_ROUTE_BLOCKS - 1
    return pl.pallas_call(
        _route_kernel,
        grid=(N_TOKENS // ROUTE_TM,),
        in_specs=[
            pl.BlockSpec((ROUTE_TM, D_MODEL), lambda i: (jnp.minimum(i, last_ctx), 0)),
            pl.BlockSpec((ROUTE_TM, D_MODEL), lambda i: (jnp.maximum(i - CTX_ROUTE_BLOCKS, 0), 0)),
            pl.BlockSpec((None, 6, D_MODEL),
                         lambda i: (jnp.where(i < CTX_ROUTE_BLOCKS, 0, (i - CTX_ROUTE_BLOCKS) // per_seq + 1), 0, 0)),
            pl.BlockSpec((1, D_MODEL), lambda i: (0, 0)),
            pl.BlockSpec((D_MODEL, LANES), lambda i: (0, 0)),
            pl.BlockSpec((D_MODEL, LANES), lambda i: (0, 0)),
            pl.BlockSpec((N_EXPERTS, 1), lambda i: (0, 0)),
        ],
        out_specs=[pl.BlockSpec((ROUTE_TM * ROW_SLABS, LANES), lambda i: (i, 0)),
                   pl.BlockSpec((META_ROWS, ROUTE_TM), lambda i: (0, i)),
                   pl.BlockSpec((BUCKET_ROWS, LANES), lambda i: (0, 0)),
                   pl.BlockSpec((BUCKET_ROWS, LANES), lambda i: (i, 0))],
        out_shape=[jax.ShapeDtypeStruct((N_TOKENS * ROW_SLABS, LANES), F32),
                   jax.ShapeDtypeStruct((META_ROWS, N_TOKENS), jnp.int32),
                   jax.ShapeDtypeStruct((BUCKET_ROWS, LANES), jnp.int32),
                   jax.ShapeDtypeStruct((N_TOKENS // ROUTE_TM * BUCKET_ROWS, LANES), jnp.int32)],
        scratch_shapes=[pltpu.VMEM((BUCKET_ROWS, 1), F32),
                        pltpu.VMEM((ROUTE_TM, D_MODEL), BF16),
                        pltpu.VMEM((ROUTE_TM, D_MODEL), BF16)],
        compiler_params=pltpu.CompilerParams(dimension_semantics=("arbitrary",)),
        name="norm_route",
    )(x1p, x1s, mod_l, n2g, wr_hi, wr_lo, br)


def _const_spec(shape):
    nd = len(shape)
    return pl.BlockSpec(shape, lambda b: (0,) * nd, pipeline_mode=pl.Buffered(1))


def _stream_specs(latent, single_buffer_rows=False):
    n = DEC_SEQ if latent else SEQ
    first = BATCH * SEQ // n if latent else 0
    own = pl.BlockSpec((n, D_MODEL), lambda b: (b, 0))
    rows = pl.BlockSpec((n * ROW_SLABS, LANES), lambda b: (first + b, 0),
                        pipeline_mode=pl.Buffered(1) if single_buffer_rows else None)
    mod = pl.BlockSpec((None, 6, D_MODEL), (lambda b: (b + 1, 0, 0)) if latent else (lambda b: (0, 0, 0)))
    return n, own, rows, mod


def _mixer(x, mod_l, weights, *, latent, prev=None, kv_old=None, cache=None, bias=None, layer=0):
    n, own_spec, rows_spec, mod_spec = _stream_specs(latent)
    tok = x.shape[0]
    nb = tok // n
    in_specs = [own_spec, mod_spec]
    args = [x, mod_l]
    if prev is not None:
        in_specs += [rows_spec, mod_spec]
        args += list(prev)
        if not latent:
            in_specs += [pl.BlockSpec((None, layer, n, D_ATTN), lambda b: (b, 0, 0, 0))] * 2
            args += list(kv_old)
    if latent:
        cache_spec = pl.BlockSpec((None, None, PAST_LEN, D_ATTN), lambda b: (b, layer, 0, 0),
                                  pipeline_mode=pl.Buffered(1))
        in_specs += [cache_spec, cache_spec, _const_spec(bias.shape)]
        args += [cache[0], cache[1], bias]
    in_specs += [_const_spec(w.shape) for w in weights]
    args += list(weights)
    out_shape = [jax.ShapeDtypeStruct((tok, D_MODEL), F32)]
    out_specs = [own_spec]
    if not latent:
        out_shape += [jax.ShapeDtypeStruct((nb, layer + 1, n, D_ATTN), F32)] * 2
        out_specs += [pl.BlockSpec((None, layer + 1, n, D_ATTN), lambda b: (b, 0, 0, 0))] * 2
    scratch = [pltpu.VMEM((3 * N_PAIRS, n, LANES), BF16),
               pltpu.VMEM((3, n, D_POOL), F32),
               pltpu.VMEM((ROW_SLABS, n, LANES), BF16)]
    if latent:
        scratch += [pltpu.VMEM((N_PAIRS, PAST_LEN, LANES), BF16),
                    pltpu.VMEM((N_PAIRS, PAST_LEN, LANES), BF16)]
    return pl.pallas_call(
        functools.partial(_mixer_kernel, latent=latent, has_prev=prev is not None, n=n),
        grid=(nb,),
        in_specs=in_specs,
        out_specs=out_specs,
        out_shape=out_shape,
        scratch_shapes=scratch,
        compiler_params=pltpu.CompilerParams(dimension_semantics=("arbitrary",),
                                             vmem_limit_bytes=VMEM_LIMIT),
        name="mixer_latent" if latent else "mixer_context",
    )(*args)


def _bucket_tables():
    lo, hi = [], []
    for g in range(N_GROUPS):
        for a in range(GROUP_SIZE):
            for b in range(a + 1, GROUP_SIZE):
                lo.append(g * GROUP_SIZE + a)
                hi.append(g * GROUP_SIZE + b)
    return np.asarray(lo, np.int32), np.asarray(hi, np.int32)


def _route_plan(bucket, rank, local_rank, count, block_count):
    tiles_b = (count + MOE_TM - 1) // MOE_TM
    tile_end = jnp.cumsum(tiles_b)
    tile_start = tile_end - tiles_b
    pos = tile_start[bucket] * MOE_TM + rank
    block = jnp.arange(N_TOKENS, dtype=jnp.int32) // PERM_TM
    run_src = jnp.cumsum(block_count, axis=1) - block_count
    run_dst = tile_start[None, :] * MOE_TM + jnp.cumsum(block_count, axis=0) - block_count
    local_pos = run_src[block, bucket] + local_rank
    n_used = tile_end[-1]
    fill_lo = jnp.concatenate([tile_start * MOE_TM + count, (n_used * MOE_TM)[None]])
    fill_hi = jnp.concatenate([tile_end * MOE_TM, jnp.full((1,), MOE_TILES * MOE_TM, jnp.int32)])
    tile = jnp.arange(MOE_TILES, dtype=jnp.int32)
    tile_b = jnp.sum((tile[:, None] >= tile_end[None, :]).astype(jnp.int32), axis=1)
    last_b = jnp.sum((n_used - 1 >= tile_end).astype(jnp.int32))
    tile_b = jnp.minimum(tile_b, last_b)
    n_valid = jnp.clip(count[tile_b] - (tile - tile_start[tile_b]) * MOE_TM, 0, MOE_TM)
    lo_tab, hi_tab = _bucket_tables()
    i32 = lambda a: a.astype(jnp.int32)
    return (i32(local_pos), i32(pos), i32(block_count.reshape(-1)), i32(run_src.reshape(-1)),
            i32(run_dst.reshape(-1)), i32(fill_lo), i32(fill_hi)), (
        jnp.asarray(lo_tab)[tile_b], jnp.asarray(hi_tab)[tile_b], i32(n_valid), i32(n_used).reshape(1))


def _permute_kernel(lpos_ref, pos_ref, rlen_ref, rsrc_ref, rdst_ref, lo_ref, hi_ref,
                    h_hbm, hs_hbm, tok_ref, buf, srt, lsem, ssem, fsem):
    i = pl.program_id(0)
    last = pl.num_programs(0) - 1
    half = i % 2
    blk = PERM_TM * ROW_SLABS

    def load(block, h):
        return pltpu.make_async_copy(h_hbm.at[pl.ds(block * blk, blk), :], buf.at[pl.ds(h * blk, blk), :], lsem.at[h])

    def wait_rows(sem, rows):
        n = rows * ROW_SLABS
        pltpu.make_async_copy(hs_hbm.at[pl.ds(0, n), :], hs_hbm.at[pl.ds(0, n), :], sem).wait()

    def rows_at(ref, first, rows):
        return ref.at[pl.ds(pl.multiple_of(first * ROW_SLABS, ROW_SLABS), rows * ROW_SLABS), :]

    @pl.when(i == 0)
    def _():
        load(0, 0).start()

    load(i, half).wait()

    @pl.when(i < last)
    def _():
        load(i + 1, 1 - half).start()

    @pl.when(i >= 2)
    def _():
        wait_rows(ssem.at[half], PERM_TM)

    base = i * PERM_TM
    for r in range(PERM_TM):
        srt[pl.ds(pl.multiple_of((half * PERM_TM + lpos_ref[base + r]) * ROW_SLABS, ROW_SLABS), ROW_SLABS), :] = (
            buf[pl.ds(pl.multiple_of((half * PERM_TM + r) * ROW_SLABS, ROW_SLABS), ROW_SLABS), :])
        tok_ref[pos_ref[base + r]] = base + r

    for b in range(N_BUCKETS):
        length = rlen_ref[i * N_BUCKETS + b]
        src = half * PERM_TM + rsrc_ref[i * N_BUCKETS + b]
        dst = rdst_ref[i * N_BUCKETS + b]
        run = PERM_TM
        while run >= 1:
            take = (length & run) != 0

            @pl.when(take)
            def _(src=src, dst=dst, run=run):
                pltpu.make_async_copy(rows_at(srt, src, run), rows_at(hs_hbm, dst, run),
                                      ssem.at[half]).start(priority=b % 2)

            step = jnp.where(take, run, 0)
            src, dst = src + step, dst + step
            run //= 2

    @pl.when(i == 0)
    def _():
        def fill(first, rows):
            n = rows * ROW_SLABS
            pltpu.make_async_copy(buf.at[pl.ds(0, n), :],
                                  hs_hbm.at[pl.ds(pl.multiple_of(first * ROW_SLABS, ROW_SLABS), n), :], fsem).start()

        def zero_tok(p, carry):
            tok_ref[p] = 0
            return carry

        for b in range(N_BUCKETS + 1):
            lax.fori_loop(lo_ref[b], hi_ref[b], zero_tok, 0)
        for b in range(N_BUCKETS):
            first = lo_ref[b]
            length = hi_ref[b] - first
            run = MOE_TM // 2
            while run >= 1:
                take = (length & run) != 0
                pl.when(take)(functools.partial(fill, first, run))
                first = first + jnp.where(take, run, 0)
                run //= 2

        def fill_tile(t, carry):
            fill(t * MOE_TM, MOE_TM)
            return carry

        lax.fori_loop(lo_ref[N_BUCKETS] // MOE_TM, MOE_TILES, fill_tile, 0)
        wait_rows(fsem, MOE_TILES * MOE_TM - N_TOKENS)

    @pl.when(i == last)
    def _():
        wait_rows(ssem.at[half], PERM_TM)

        @pl.when(i >= 1)
        def _():
            wait_rows(ssem.at[1 - half], PERM_TM)


def _permute(h, perm):
    grid_spec = pltpu.PrefetchScalarGridSpec(
        num_scalar_prefetch=len(perm),
        grid=(N_TOKENS // PERM_TM,),
        in_specs=[pl.BlockSpec(memory_space=pl.ANY)],
        out_specs=[pl.BlockSpec(memory_space=pl.ANY), pl.BlockSpec(memory_space=pltpu.SMEM)],
        scratch_shapes=[pltpu.VMEM((2 * PERM_TM * ROW_SLABS, LANES), F32),
                        pltpu.VMEM((2 * PERM_TM * ROW_SLABS, LANES), F32),
                        pltpu.SemaphoreType.DMA((2,)),
                        pltpu.SemaphoreType.DMA((2,)),
                        pltpu.SemaphoreType.DMA(())],
    )
    return pl.pallas_call(
        _permute_kernel,
        grid_spec=grid_spec,
        out_shape=[jax.ShapeDtypeStruct((MOE_TILES * MOE_TM * ROW_SLABS, LANES), F32),
                   jax.ShapeDtypeStruct((MOE_TILES * MOE_TM,), jnp.int32)],
        compiler_params=pltpu.CompilerParams(dimension_semantics=("arbitrary",)),
        name="permute_rows",
    )(*perm, h)


def _row_slabs(ref, row):
    return ref.at[pl.ds(pl.multiple_of(row * ROW_SLABS, ROW_SLABS), ROW_SLABS), :]


def _moe_kernel(e0_ref, e1_ref, nv_ref, nu_ref, tok_ref, hs_ref, wrt_ref, wg0_ref, wu0_ref, wd0_ref,
                wg1_ref, wu1_ref, wd1_ref, y_hbm, xb_scr, stg, ssem):
    i = pl.program_id(0)
    n_used = nu_ref[0]

    def wait_scatter(rows):
        n = rows * ROW_SLABS
        pltpu.make_async_copy(y_hbm.at[pl.ds(0, n), :], y_hbm.at[pl.ds(0, n), :], ssem).wait()

    @pl.when(i < n_used)
    def _():
        w0 = wrt_ref[pl.ds(e0_ref[i], 1), :]
        w1 = wrt_ref[pl.ds(e1_ref[i], 1), :]
        l0 = jnp.zeros((MOE_TM, LANES), F32)
        l1 = jnp.zeros((MOE_TM, LANES), F32)
        for j in range(ROW_SLABS):
            xj = hs_ref[_slab(j, MOE_TM), :]
            xb_scr[:, j * LANES:(j + 1) * LANES] = xj.astype(BF16)
            l0 = l0 + xj * w0[:, j * LANES:(j + 1) * LANES]
            l1 = l1 + xj * w1[:, j * LANES:(j + 1) * LANES]
        a0 = jax.nn.sigmoid(jnp.sum(l0, axis=1, keepdims=True))
        a1 = jax.nn.sigmoid(jnp.sum(l1, axis=1, keepdims=True))
        den = a0 + a1
        acc = None
        for gate, wg_ref, wu_ref, wd_ref in ((a0 / den, wg0_ref, wu0_ref, wd0_ref),
                                             (a1 / den, wg1_ref, wu1_ref, wd1_ref)):
            a = jnp.dot(xb_scr[...], wg_ref[...], preferred_element_type=F32)
            b = jnp.dot(xb_scr[...], wu_ref[...], preferred_element_type=F32)
            hid = (_silu(a) * b * gate).astype(BF16)
            o = jnp.dot(hid, wd_ref[...], preferred_element_type=F32)
            acc = o if acc is None else acc + o

        @pl.when(i >= 1)
        def _():
            wait_scatter(nv_ref[jnp.maximum(i - 1, 0)])

        _store_slabs(stg, acc, MOE_TM)
        base = i * MOE_TM

        def scatter_row(r, priority):
            pltpu.make_async_copy(_row_slabs(stg, r), _row_slabs(y_hbm, tok_ref[base + r]),
                                  ssem).start(priority=priority)

        def scatter_rows(c, carry):
            for u in range(DMA_UNROLL):
                scatter_row(c * DMA_UNROLL + u, u % 2)
            return carry

        def scatter_one(r, carry):
            scatter_row(r, 0)
            return carry

        n_full = nv_ref[i] // DMA_UNROLL
        lax.fori_loop(0, n_full, scatter_rows, 0)
        lax.fori_loop(n_full * DMA_UNROLL, nv_ref[i], scatter_one, 0)

        @pl.when(i == n_used - 1)
        def _():
            wait_scatter(nv_ref[i])


def _moe(hs, row_tok, tiles, wrt, wg, wu, wd, layer):
    e0, e1, n_valid, n_used = tiles

    def w_spec(which, shape):
        return pl.BlockSpec((None, None) + shape,
                            lambda i, e0, e1, nv, nu, tok: (layer, (e0, e1)[which][i], 0, 0))

    up_shape, down_shape = (D_MODEL, D_FF), (D_FF, D_MODEL)
    grid_spec = pltpu.PrefetchScalarGridSpec(
        num_scalar_prefetch=5,
        grid=(MOE_TILES,),
        in_specs=[pl.BlockSpec((MOE_TM * ROW_SLABS, LANES),
                               lambda i, e0, e1, nv, nu, tok: (jnp.minimum(i, nu[0] - 1), 0)),
                  pl.BlockSpec((N_EXPERTS, D_MODEL), lambda i, *_: (0, 0)),
                  w_spec(0, up_shape), w_spec(0, up_shape), w_spec(0, down_shape),
                  w_spec(1, up_shape), w_spec(1, up_shape), w_spec(1, down_shape)],
        out_specs=pl.BlockSpec(memory_space=pl.ANY),
        scratch_shapes=[pltpu.VMEM((MOE_TM, D_MODEL), BF16),
                        pltpu.VMEM((MOE_TM * ROW_SLABS, LANES), F32),
                        pltpu.SemaphoreType.DMA(())],
    )
    return pl.pallas_call(
        _moe_kernel,
        grid_spec=grid_spec,
        out_shape=jax.ShapeDtypeStruct((N_TOKENS * ROW_SLABS, LANES), F32),
        compiler_params=pltpu.CompilerParams(dimension_semantics=("arbitrary",),
                                             vmem_limit_bytes=VMEM_LIMIT),
        name="moe_routed",
    )(e0, e1, n_valid, n_used, row_tok, hs, wrt, wg, wu, wd, wg, wu, wd)


def _final_kernel(x1_ref, y_ref, mod_ref, fg_ref, o_ref):
    x2 = x1_ref[...] + mod_ref[5:6, :] * _rows_to_2d(y_ref, x1_ref.shape[0])
    o_ref[...] = _rms(x2, fg_ref[...])


def _final(x1, y, mod_l, fg, *, latent):
    n, own_spec, rows_spec, mod_spec = _stream_specs(latent)
    return pl.pallas_call(
        _final_kernel,
        grid=(x1.shape[0] // n,),
        in_specs=[own_spec, rows_spec, mod_spec, pl.BlockSpec((1, D_MODEL), lambda b: (0, 0))],
        out_specs=own_spec,
        out_shape=jax.ShapeDtypeStruct(x1.shape, F32),
        name="final_norm",
    )(x1, y, mod_l, fg)


def kernel(x_prompt, x_sample, cache_k, cache_v, c, c_ctx, w_ada, b_ada, norm1_g, norm2_g, w_in, w_out, rpb, pool_w, pool_scale, conv_w, conv_b, conv_ln_g, conv_ln_b, conv_pw_w, conv_pw_b, w_router, b_router, moe_w_gate, moe_w_up, moe_w_down, final_g):
    xp = x_prompt.reshape(BATCH * SEQ, D_MODEL)
    xs = x_sample.reshape(DEC_BATCH * DEC_SEQ, D_MODEL)
    cond = jnp.concatenate([c_ctx[None, :], c, jnp.zeros((N_MOD_ROWS - 1 - DEC_BATCH, D_MODEL), F32)], axis=0)
    mod = _ada_modulation(cond, w_ada, b_ada).reshape(DEPTH, N_MOD_ROWS, 6, D_MODEL)
    bias = _rpb_tiles(rpb)
    ck = cache_k.reshape(DEC_BATCH, DEPTH, PAST_LEN, D_ATTN)
    cv = cache_v.reshape(DEC_BATCH, DEPTH, PAST_LEN, D_ATTN)
    wrt = w_router.T
    wr = jnp.pad(w_router, ((0, 0), (0, LANES - N_EXPERTS)))
    wr_hi = wr.astype(BF16)
    wr_lo = (wr - wr_hi.astype(F32)).astype(BF16)
    br = b_router.reshape(N_EXPERTS, 1)
    fg = final_g.reshape(1, D_MODEL)
    wg, wu, wd = moe_w_gate.astype(BF16), moe_w_up.astype(BF16), moe_w_down.astype(BF16)
    new_k = new_v = None
    prev = None
    for l in range(DEPTH):
        pool_bd = jnp.zeros((D_POOL, D_POOL), F32)
        gd = D_POOL // 4
        for g in range(4):
            pool_bd = pool_bd.at[g * gd:(g + 1) * gd, g * gd:(g + 1) * gd].set(pool_w[l, g])
        weights = [norm1_g[l].reshape(1, -1),
                   w_in[l].astype(BF16), w_out[l].astype(BF16),
                   pool_bd.astype(BF16), pool_scale[l].reshape(1, -1),
                   conv_w[l], conv_b[l].reshape(1, -1),
                   conv_ln_g[l].reshape(1, -1), conv_ln_b[l].reshape(1, -1),
                   conv_pw_w[l].astype(BF16), conv_pw_b[l].reshape(1, -1)]

        xp, new_k, new_v = _mixer(xp, mod[l], weights, latent=False, prev=prev, kv_old=(new_k, new_v), layer=l)
        xs, = _mixer(xs, mod[l], weights, latent=True, prev=prev, cache=(ck, cv), bias=bias[l], layer=l)

        h, meta, count, block_count = _route(xp, xs, mod[l], norm2_g[l].reshape(1, -1), wr_hi, wr_lo, br)
        block_count = block_count.reshape(N_TOKENS // ROUTE_TM, BUCKET_ROWS, LANES)[:, :N_BUCKETS, 0]
        perm, tiles = _route_plan(meta[0], meta[1], meta[2], count[:N_BUCKETS, 0], block_count)
        hs, row_tok = _permute(h, perm)
        y = _moe(hs, row_tok, tiles, wrt, wg, wu, wd, l)
        prev = (y, mod[l])

    y_prompt = _final(xp, y, mod[DEPTH - 1], fg, latent=False).reshape(BATCH, SEQ, D_MODEL)
    y_sample = _final(xs, y, mod[DEPTH - 1], fg, latent=True).reshape(DEC_BATCH, DEC_SEQ, D_MODEL)
    new_cache_k = new_k.reshape(BATCH, DEPTH, SEQ, N_HEADS, HEAD_DIM)
    new_cache_v = new_v.reshape(BATCH, DEPTH, SEQ, N_HEADS, HEAD_DIM)
    return (y_prompt, y_sample, new_cache_k, new_cache_v)
```

```python
import functools

import numpy as np
import jax
import jax.numpy as jnp
from jax import lax
from jax.experimental import pallas as pl
from jax.experimental.pallas import tpu as pltpu

D_MODEL = 1024
BATCH = 32
SEQ = 256
DEPTH = 2
DEC_BATCH = 8
DEC_SEQ = 1024
PAST_LEN = 512
GRID_W = 64
GRID_ROWS = DEC_SEQ // GRID_W
D_ATTN = 512
HEAD_DIM = 64
N_HEADS = 8
N_PAIRS = N_HEADS // 2
WIN_ROWS = 8
WIN_COLS = 16
D_POOL = 256
D_CONV = 256
CONV_WIDTH = 31
D_IN = 3 * D_ATTN + D_POOL + 2 * D_CONV
N_EXPERTS = 16
N_GROUPS = 4
GROUP_SIZE = N_EXPERTS // N_GROUPS
D_FF = 512
EPS = 1e-6
N_MOD_ROWS = 16
N_DR = 2 * WIN_ROWS - 1
N_DC = 2 * WIN_COLS - 1
N_BIAS_TILES = N_DR - 1
LAT_GROUP_ROWS = 4
SEQ_PAD = 16
LANES = 128
SUBLANES = 8
ROW_SLABS = D_MODEL // LANES
N_TOKENS = BATCH * SEQ + DEC_BATCH * DEC_SEQ
PAIRS_PER_GROUP = GROUP_SIZE * (GROUP_SIZE - 1) // 2
N_BUCKETS = N_GROUPS * PAIRS_PER_GROUP
BUCKET_ROWS = 32
META_ROWS = 8
ROUTE_TM = 512
ROUTE_CHUNK = 32
CTX_ROUTE_BLOCKS = BATCH * SEQ // ROUTE_TM
MOE_TM = 256
MOE_TILES = N_TOKENS // MOE_TM + N_BUCKETS
PERM_TM = 512
DMA_UNROLL = 8
VMEM_LIMIT = 62 * 1024 * 1024

F32 = jnp.float32
BF16 = jnp.bfloat16
HIGHEST = lax.Precision.HIGHEST
NT_DIMS = (((1,), (1,)), ((), ()))


def _rms(x, g):
    return x * lax.rsqrt(jnp.mean(x * x, axis=-1, keepdims=True) + EPS) * g


def _silu(x):
    return x * jax.nn.sigmoid(x)


def _ada_kernel(cond_ref, w_ref, b_ref, o_ref):
    o_ref[...] = jnp.dot(_silu(cond_ref[...]), w_ref[...], preferred_element_type=F32,
                         precision=HIGHEST) + b_ref[...]


def _ada_modulation(cond, w_ada, b_ada):
    n_col = 6 * D_MODEL // D_MODEL
    return pl.pallas_call(
        _ada_kernel,
        grid=(DEPTH, n_col),
        in_specs=[
            pl.BlockSpec((N_MOD_ROWS, D_MODEL), lambda l, n: (0, 0)),
            pl.BlockSpec((None, D_MODEL, D_MODEL), lambda l, n: (l, 0, n)),
            pl.BlockSpec((None, 1, D_MODEL), lambda l, n: (l, 0, n)),
        ],
        out_specs=pl.BlockSpec((None, N_MOD_ROWS, D_MODEL), lambda l, n: (l, 0, n)),
        out_shape=jax.ShapeDtypeStruct((DEPTH, N_MOD_ROWS, 6 * D_MODEL), F32),
        name="ada_modulation",
    )(cond, w_ada, b_ada.reshape(DEPTH, 1, 6 * D_MODEL))


def _rpb_kernel(rpb_ref, o_ref):
    base = (pl.program_id(0) * N_HEADS + pl.program_id(1)) * (N_DR * N_DC)
    cq = lax.broadcasted_iota(jnp.int32, (GRID_W, LANES), 0)
    lane = lax.broadcasted_iota(jnp.int32, (GRID_W, LANES), 1)
    ck = lane & (GRID_W - 1)
    right = lane >= GRID_W
    dcm = jnp.clip(ck - cq, -(WIN_COLS - 1), WIN_COLS - 1) + (WIN_COLS - 1)
    cs = jnp.clip(cq - WIN_COLS // 2, 0, GRID_W - WIN_COLS)
    valid = (ck >= cs) & (ck < cs + WIN_COLS)
    for d in range(N_BIAS_TILES):
        acc = jnp.zeros((GRID_W, LANES), F32)
        for dc in range(N_DC):
            v = jnp.where(right, rpb_ref[base + (d + 1) * N_DC + dc], rpb_ref[base + d * N_DC + dc])
            acc = jnp.where(dcm == dc, v, acc)
        o_ref[d] = jnp.where(valid, acc, -jnp.inf)


def _rpb_tiles(rpb):
    return pl.pallas_call(
        _rpb_kernel,
        grid=(DEPTH, N_HEADS),
        in_specs=[pl.BlockSpec(memory_space=pltpu.SMEM)],
        out_specs=pl.BlockSpec((None, None, N_BIAS_TILES, GRID_W, LANES), lambda l, h: (l, h, 0, 0, 0)),
        out_shape=jax.ShapeDtypeStruct((DEPTH, N_HEADS, N_BIAS_TILES, GRID_W, LANES), F32),
        name="rpb_tiles",
    )(rpb.reshape(-1))


def _shift_rows(a, s):
    return a if s == 0 else pltpu.roll(a, s % a.shape[0], 0)


def _pad_rows(a):
    return jnp.concatenate([a, jnp.zeros((SEQ_PAD, a.shape[1]), a.dtype)], axis=0)


def _softmax_parts(s):
    m = jnp.max(s, axis=-1, keepdims=True)
    e = jnp.exp(s - m)
    return e, jnp.sum(e, axis=-1, keepdims=True)


def _context_attention(qkv_scr, m_scr, n):
    left = lax.broadcasted_iota(jnp.int32, (n, LANES), 1) < HEAD_DIM
    for p in range(N_PAIRS):
        q = qkv_scr[p]
        k = qkv_scr[N_PAIRS + p]
        v = qkv_scr[2 * N_PAIRS + p]
        zero = jnp.zeros_like(q)
        outs = []
        for qh in (jnp.where(left, q, zero), jnp.where(left, zero, q)):
            s = lax.dot_general(qh, k, NT_DIMS, preferred_element_type=F32)
            e, l = _softmax_parts(s)
            outs.append(jnp.dot(e.astype(BF16), v, preferred_element_type=F32) / l)
        m_scr[p] = jnp.where(left, outs[0], outs[1]).astype(BF16)


def _window_start(qr):
    return min(max(qr - WIN_ROWS // 2, 0), GRID_ROWS - WIN_ROWS)


def _latent_groups():
    groups = []
    for r0 in range(0, GRID_ROWS, LAT_GROUP_ROWS):
        k0 = _window_start(r0)
        k1 = _window_start(r0 + LAT_GROUP_ROWS - 1) + WIN_ROWS
        nk = k1 - k0 + (k1 - k0) % 2
        k0 = min(k0, GRID_ROWS - nk)
        groups.append((r0, LAT_GROUP_ROWS, k0, nk))
    return groups


def _group_bias(bias_ref, p, r0, nq, k0, nk):
    left = lax.broadcasted_iota(jnp.int32, (GRID_W, LANES), 1) < HEAD_DIM
    neg = jnp.full((GRID_W, LANES), -jnp.inf, F32)
    blocks = []
    for hh in range(2):
        for qr in range(r0, r0 + nq):
            rs = _window_start(qr)
            pieces = []
            for kr in range(k0, k0 + nk, 2):
                in0 = rs <= kr < rs + WIN_ROWS
                in1 = rs <= kr + 1 < rs + WIN_ROWS
                d = kr - qr + (WIN_ROWS - 1)
                if in0 and in1:
                    pieces.append(bias_ref[2 * p + hh, d])
                elif in0:
                    pieces.append(jnp.where(left, bias_ref[2 * p + hh, d], neg))
                elif in1:
                    pieces.append(jnp.where(left, neg, bias_ref[2 * p + hh, d]))
                else:
                    pieces.append(neg)
            blocks.append(jnp.concatenate(pieces, axis=1))
    return jnp.concatenate(blocks, axis=0)


def _latent_attention(qkv_scr, kc_scr, vc_scr, bias_ref, m_scr):
    def body(p, carry):
        for r0, nq, k0, nk in _latent_groups():
            rows = nq * GRID_W
            left = lax.broadcasted_iota(jnp.int32, (rows, LANES), 1) < HEAD_DIM
            q = qkv_scr[p, r0 * GRID_W:r0 * GRID_W + rows, :]
            zero = jnp.zeros_like(q)
            qm = jnp.concatenate([jnp.where(left, q, zero), jnp.where(left, zero, q)], axis=0)
            kl = qkv_scr[N_PAIRS + p, k0 * GRID_W:(k0 + nk) * GRID_W, :]
            vl = qkv_scr[2 * N_PAIRS + p, k0 * GRID_W:(k0 + nk) * GRID_W, :]
            s_loc = (lax.dot_general(qm, kl, NT_DIMS, preferred_element_type=F32)
                     + _group_bias(bias_ref, p, r0, nq, k0, nk))
            s_ctx = lax.dot_general(qm, kc_scr[p], NT_DIMS, preferred_element_type=F32)
            m = jnp.maximum(jnp.max(s_loc, axis=-1, keepdims=True), jnp.max(s_ctx, axis=-1, keepdims=True))
            e_loc = jnp.exp(s_loc - m)
            e_ctx = jnp.exp(s_ctx - m)
            l = jnp.sum(e_loc, axis=-1, keepdims=True) + jnp.sum(e_ctx, axis=-1, keepdims=True)
            o = (jnp.dot(e_loc.astype(BF16), vl, preferred_element_type=F32)
                 + jnp.dot(e_ctx.astype(BF16), vc_scr[p], preferred_element_type=F32)) / l
            m_scr[p, r0 * GRID_W:r0 * GRID_W + rows, :] = jnp.where(left, o[:rows], o[rows:]).astype(BF16)
        return carry

    lax.fori_loop(0, N_PAIRS, body, 0)


def _router_bucket(h_hi, h_lo, wr_hi_ref, wr_lo_ref, br_ref, n):
    logits = (jnp.dot(h_hi, wr_hi_ref[...], preferred_element_type=F32)
              + (jnp.dot(h_lo, wr_hi_ref[...], preferred_element_type=F32)
                 + jnp.dot(h_hi, wr_lo_ref[...], preferred_element_type=F32)))
    sel = jax.nn.sigmoid(logits.T[:N_EXPERTS]) + br_ref[...]
    rows = [sel[e:e + 1, :] for e in range(N_EXPERTS)]
    gscore = []
    for g in range(N_GROUPS):
        r4 = rows[g * GROUP_SIZE:(g + 1) * GROUP_SIZE]
        best = None
        for i in range(GROUP_SIZE):
            for j in range(i + 1, GROUP_SIZE):
                s = r4[i] + r4[j]
                best = s if best is None else jnp.maximum(best, s)
        gscore.append(best)
    gbest = jnp.zeros((1, n), jnp.int32)
    gval = gscore[0]
    for g in range(1, N_GROUPS):
        upd = gscore[g] > gval
        gbest = jnp.where(upd, g, gbest)
        gval = jnp.where(upd, gscore[g], gval)
    picked = []
    for e in range(N_EXPERTS):
        g = e // GROUP_SIZE
        rank = jnp.zeros((1, n), jnp.int32)
        for j in range(g * GROUP_SIZE, (g + 1) * GROUP_SIZE):
            if j == e:
                continue
            ahead = (rows[j] > rows[e]) | (rows[j] == rows[e]) if j < e else rows[j] > rows[e]
            rank = rank + ahead.astype(jnp.int32)
        picked.append((rank < 2) & (gbest == g))
    e_lo = jnp.zeros((1, n), jnp.int32)
    e_hi = jnp.zeros((1, n), jnp.int32)
    for e in range(N_EXPERTS):
        e_hi = jnp.where(picked[e], e, e_hi)
        e_lo = jnp.where(picked[N_EXPERTS - 1 - e], N_EXPERTS - 1 - e, e_lo)
    a = e_lo - GROUP_SIZE * gbest
    b = e_hi - GROUP_SIZE * gbest
    return gbest * PAIRS_PER_GROUP + jnp.right_shift(a * (2 * GROUP_SIZE - 1 - a), 1) + (b - a - 1)


def _slab(j, n):
    return pl.ds(j, n, stride=ROW_SLABS)


def _rows_to_2d(ref, n):
    return jnp.concatenate([ref[_slab(j, n), :] for j in range(ROW_SLABS)], axis=1)


def _store_slabs(ref, val, n, first_row=0):
    for j in range(ROW_SLABS):
        ref[_slab(first_row + j, n), :] = val[:, j * LANES:(j + 1) * LANES]


def _mixer_kernel(*refs, latent, has_prev, n):
    it = iter(refs)
    x_ref, mod_ref = next(it), next(it)
    if has_prev:
        y_ref, modp_ref = next(it), next(it)
        if not latent:
            kold_ref, vold_ref = next(it), next(it)
    if latent:
        ck_ref, cv_ref, bias_ref = next(it), next(it), next(it)
    (n1g_ref, w_in_ref, w_out_ref, poolw_ref, pscale_ref, convw_ref, convb_ref,
     lng_ref, lnb_ref, pww_ref, pwb_ref) = [next(it) for _ in range(11)]
    x1_ref = next(it)
    if not latent:
        knew_ref, vnew_ref = next(it), next(it)
    qkv_scr, u_scr, m_scr = next(it), next(it), next(it)
    if latent:
        kc_scr, vc_scr = next(it), next(it)

    x = x_ref[...]
    if has_prev:
        x = x + modp_ref[5:6, :] * _rows_to_2d(y_ref, n)
    layer = 0
    if has_prev and not latent:
        layer = kold_ref.shape[0]
        knew_ref[0:layer] = kold_ref[...]
        vnew_ref[0:layer] = vold_ref[...]
    sh1, sc1, g1 = mod_ref[0:1, :], mod_ref[1:2, :], mod_ref[2:3, :]

    hb = (_rms(x, n1g_ref[...]) * (1.0 + sc1) + sh1).astype(BF16)
    cw = 2 * LANES
    for j in range(D_IN // cw):
        zj = jnp.dot(hb, w_in_ref[:, j * cw:(j + 1) * cw], preferred_element_type=F32)
        if j < 6:
            part = j // 2
            zs = zj * (HEAD_DIM ** -0.5) if part == 0 else zj
            c = part * N_PAIRS + 2 * (j % 2)
            qkv_scr[c] = zs[:, :LANES].astype(BF16)
            qkv_scr[c + 1] = zs[:, LANES:].astype(BF16)
            if not latent and part == 1:
                knew_ref[layer, :, (j % 2) * cw:(j % 2 + 1) * cw] = zj
            if not latent and part == 2:
                vnew_ref[layer, :, (j % 2) * cw:(j % 2 + 1) * cw] = zj
        else:
            u_scr[j - 6] = zj

    if latent:
        for p in range(N_PAIRS):
            kc_scr[p] = ck_ref[:, p * LANES:(p + 1) * LANES].astype(BF16)
            vc_scr[p] = cv_ref[:, p * LANES:(p + 1) * LANES].astype(BF16)
        _latent_attention(qkv_scr, kc_scr, vc_scr, bias_ref, m_scr)
    else:
        _context_attention(qkv_scr, m_scr, n)

    row = lax.broadcasted_iota(jnp.int32, (n, D_POOL), 0)
    grp = lax.broadcasted_iota(jnp.int32, (n, D_POOL), 1) // (D_POOL // 4)
    up = u_scr[0]
    t2 = _pad_rows(up)
    t2 = t2 + _shift_rows(t2, 1)
    t4 = t2 + _shift_rows(t2, 2)
    t8 = t4 + _shift_rows(t4, 4)
    t16 = t8 + _shift_rows(t8, 8)
    s2, s4, s8, s16 = t2[:n], _shift_rows(t4, -1)[:n], _shift_rows(t8, -3)[:n], _shift_rows(t16, -7)[:n]
    wsum = jnp.where(grp == 0, s2, jnp.where(grp == 1, s4, jnp.where(grp == 2, s8, s16)))
    half = jnp.left_shift(1, grp)
    lo = jnp.maximum(row - half, 0)
    hi = jnp.minimum(row + half - 1, n - 1)
    diff = wsum / (hi - lo + 1).astype(F32) - up
    pooled = jnp.dot(diff.astype(BF16), poolw_ref[...], preferred_element_type=F32) * pscale_ref[...]
    m_scr[N_PAIRS] = pooled[:, :LANES].astype(BF16)
    m_scr[N_PAIRS + 1] = pooled[:, LANES:].astype(BF16)

    y = _pad_rows(u_scr[1] * jax.nn.sigmoid(u_scr[2]))
    acc = jnp.zeros(y.shape, F32) + convb_ref[...]
    y_rot = [_shift_rows(y, b) for b in range(SUBLANES)]
    for k in range(CONV_WIDTH):
        s = CONV_WIDTH // 2 - k
        acc = acc + convw_ref[k:k + 1, :] * _shift_rows(y_rot[s % SUBLANES], s - s % SUBLANES)
    acc = acc[:n]
    mu = jnp.mean(acc, axis=-1, keepdims=True)
    cen = acc - mu
    var = jnp.mean(cen * cen, axis=-1, keepdims=True)
    yn = _silu(cen * lax.rsqrt(var + EPS) * lng_ref[...] + lnb_ref[...])
    conv = jnp.dot(yn.astype(BF16), pww_ref[...], preferred_element_type=F32) + pwb_ref[...]
    m_scr[N_PAIRS + 2] = conv[:, :LANES].astype(BF16)
    m_scr[N_PAIRS + 3] = conv[:, LANES:].astype(BF16)

    mixed = jnp.concatenate([m_scr[j] for j in range(ROW_SLABS)], axis=1)
    x1_ref[...] = x + g1 * jnp.dot(mixed, w_out_ref[...], preferred_element_type=F32)


def _route_kernel(xp_ref, xs_ref, mod_ref, n2g_ref, wr_hi_ref, wr_lo_ref, br_ref,
                  h_ref, meta_ref, count_ref, run_scr, hi_scr, lo_scr):
    i = pl.program_id(0)
    n = ROUTE_TM

    @pl.when(i == 0)
    def _():
        run_scr[...] = jnp.zeros_like(run_scr)

    def norm_rows(x_ref):
        gain, scale, shift = n2g_ref[...], 1.0 + mod_ref[4:5, :], mod_ref[3:4, :]
        for r0 in range(0, n, ROUTE_CHUNK):
            h2 = _rms(x_ref[r0:r0 + ROUTE_CHUNK, :], gain) * scale + shift
            _store_slabs(h_ref, h2, ROUTE_CHUNK, r0 * ROW_SLABS)
            hi = h2.astype(BF16)
            hi_scr[r0:r0 + ROUTE_CHUNK, :] = hi
            lo_scr[r0:r0 + ROUTE_CHUNK, :] = (h2 - hi.astype(F32)).astype(BF16)

    pl.when(i < CTX_ROUTE_BLOCKS)(lambda: norm_rows(xp_ref))
    pl.when(i >= CTX_ROUTE_BLOCKS)(lambda: norm_rows(xs_ref))

    bucket = _router_bucket(hi_scr[...], lo_scr[...], wr_hi_ref, wr_lo_ref, br_ref, n)
    onehot = lax.broadcasted_iota(jnp.int32, (BUCKET_ROWS, n), 0) == bucket
    upper = (lax.broadcasted_iota(jnp.int32, (n, n), 0) <= lax.broadcasted_iota(jnp.int32, (n, n), 1))
    incl = jnp.dot(onehot.astype(BF16), upper.astype(BF16), preferred_element_type=F32)
    run = run_scr[...]
    rank = jnp.sum(jnp.where(onehot, incl + run, 0.0), axis=0, keepdims=True) - 1.0
    run = run + jnp.sum(onehot.astype(F32), axis=1, keepdims=True)
    run_scr[...] = run
    meta_ref[...] = jnp.concatenate(
        [bucket, rank.astype(jnp.int32), jnp.zeros((META_ROWS - 2, n), jnp.int32)], axis=0)
    count_ref[...] = jnp.broadcast_to(run, count_ref.shape).astype(jnp.int32)


def _route(x1p, x1s, mod_l, n2g, wr_hi, wr_lo, br):
    per_seq = DEC_SEQ // ROUTE_TM
    last_ctx = CTX_ROUTE_BLOCKS - 1
    return pl.pallas_call(
        _route_kernel,
        grid=(N_TOKENS // ROUTE_TM,),
        in_specs=[
            pl.BlockSpec((ROUTE_TM, D_MODEL), lambda i: (jnp.minimum(i, last_ctx), 0)),
            pl.BlockSpec((ROUTE_TM, D_MODEL), lambda i: (jnp.maximum(i - CTX_ROUTE_BLOCKS, 0), 0)),
            pl.BlockSpec((None, 6, D_MODEL),
                         lambda i: (jnp.where(i < CTX_ROUTE_BLOCKS, 0, (i - CTX_ROUTE_BLOCKS) // per_seq + 1), 0, 0)),
            pl.BlockSpec((1, D_MODEL), lambda i: (0, 0)),
            pl.BlockSpec((D_MODEL, LANES), lambda i: (0, 0)),
            pl.BlockSpec((D_MODEL, LANES), lambda i: (0, 0)),
            pl.BlockSpec((N_EXPERTS, 1), lambda i: (0, 0)),
        ],
        out_specs=[pl.BlockSpec((ROUTE_TM * ROW_SLABS, LANES), lambda i: (i, 0)),
                   pl.BlockSpec((META_ROWS, ROUTE_TM), lambda i: (0, i)),
                   pl.BlockSpec((BUCKET_ROWS, LANES), lambda i: (0, 0))],
        out_shape=[jax.ShapeDtypeStruct((N_TOKENS * ROW_SLABS, LANES), F32),
                   jax.ShapeDtypeStruct((META_ROWS, N_TOKENS), jnp.int32),
                   jax.ShapeDtypeStruct((BUCKET_ROWS, LANES), jnp.int32)],
        scratch_shapes=[pltpu.VMEM((BUCKET_ROWS, 1), F32),
                        pltpu.VMEM((ROUTE_TM, D_MODEL), BF16),
                        pltpu.VMEM((ROUTE_TM, D_MODEL), BF16)],
        compiler_params=pltpu.CompilerParams(dimension_semantics=("arbitrary",)),
        name="norm_route",
    )(x1p, x1s, mod_l, n2g, wr_hi, wr_lo, br)


def _const_spec(shape):
    nd = len(shape)
    return pl.BlockSpec(shape, lambda b: (0,) * nd, pipeline_mode=pl.Buffered(1))


def _stream_specs(latent):
    n = DEC_SEQ if latent else SEQ
    first = BATCH * SEQ // n if latent else 0
    own = pl.BlockSpec((n, D_MODEL), lambda b: (b, 0))
    rows = pl.BlockSpec((n * ROW_SLABS, LANES), lambda b: (first + b, 0))
    mod = pl.BlockSpec((None, 6, D_MODEL), (lambda b: (b + 1, 0, 0)) if latent else (lambda b: (0, 0, 0)))
    return n, own, rows, mod


def _mixer(x, mod_l, weights, *, latent, prev=None, kv_old=None, cache=None, bias=None, layer=0):
    n, own_spec, rows_spec, mod_spec = _stream_specs(latent)
    tok = x.shape[0]
    nb = tok // n
    in_specs = [own_spec, mod_spec]
    args = [x, mod_l]
    if prev is not None:
        in_specs += [rows_spec, mod_spec]
        args += list(prev)
        if not latent:
            in_specs += [pl.BlockSpec((None, layer, n, D_ATTN), lambda b: (b, 0, 0, 0))] * 2
            args += list(kv_old)
    if latent:
        cache_spec = pl.BlockSpec((None, None, PAST_LEN, D_ATTN), lambda b: (b, layer, 0, 0),
                                  pipeline_mode=pl.Buffered(1))
        in_specs += [cache_spec, cache_spec, _const_spec(bias.shape)]
        args += [cache[0], cache[1], bias]
    in_specs += [_const_spec(w.shape) for w in weights]
    args += list(weights)
    out_shape = [jax.ShapeDtypeStruct((tok, D_MODEL), F32)]
    out_specs = [own_spec]
    if not latent:
        out_shape += [jax.ShapeDtypeStruct((nb, layer + 1, n, D_ATTN), F32)] * 2
        out_specs += [pl.BlockSpec((None, layer + 1, n, D_ATTN), lambda b: (b, 0, 0, 0))] * 2
    scratch = [pltpu.VMEM((3 * N_PAIRS, n, LANES), BF16),
               pltpu.VMEM((3, n, D_POOL), F32),
               pltpu.VMEM((ROW_SLABS, n, LANES), BF16)]
    if latent:
        scratch += [pltpu.VMEM((N_PAIRS, PAST_LEN, LANES), BF16),
                    pltpu.VMEM((N_PAIRS, PAST_LEN, LANES), BF16)]
    return pl.pallas_call(
        functools.partial(_mixer_kernel, latent=latent, has_prev=prev is not None, n=n),
        grid=(nb,),
        in_specs=in_specs,
        out_specs=out_specs,
        out_shape=out_shape,
        scratch_shapes=scratch,
        compiler_params=pltpu.CompilerParams(dimension_semantics=("arbitrary",),
                                             vmem_limit_bytes=VMEM_LIMIT),
        name="mixer_latent" if latent else "mixer_context",
    )(*args)


def _bucket_tables():
    lo, hi = [], []
    for g in range(N_GROUPS):
        for a in range(GROUP_SIZE):
            for b in range(a + 1, GROUP_SIZE):
                lo.append(g * GROUP_SIZE + a)
                hi.append(g * GROUP_SIZE + b)
    return np.asarray(lo, np.int32), np.asarray(hi, np.int32)


def _route_plan(bucket, rank, count):
    tiles_b = (count + MOE_TM - 1) // MOE_TM
    tile_end = jnp.cumsum(tiles_b)
    tile_start = tile_end - tiles_b
    pos = tile_start[bucket] * MOE_TM + rank
    n_used = tile_end[-1]
    fill_lo = jnp.concatenate([tile_start * MOE_TM + count, (n_used * MOE_TM)[None]])
    fill_hi = jnp.concatenate([tile_end * MOE_TM, jnp.full((1,), MOE_TILES * MOE_TM, jnp.int32)])
    tile = jnp.arange(MOE_TILES, dtype=jnp.int32)
    tile_b = jnp.sum((tile[:, None] >= tile_end[None, :]).astype(jnp.int32), axis=1)
    last_b = jnp.sum((n_used - 1 >= tile_end).astype(jnp.int32))
    tile_b = jnp.minimum(tile_b, last_b)
    n_valid = jnp.clip(count[tile_b] - (tile - tile_start[tile_b]) * MOE_TM, 0, MOE_TM)
    lo_tab, hi_tab = _bucket_tables()
    i32 = lambda a: a.astype(jnp.int32)
    return (i32(pos), i32(fill_lo), i32(fill_hi)), (
        jnp.asarray(lo_tab)[tile_b], jnp.asarray(hi_tab)[tile_b], i32(n_valid), i32(n_used).reshape(1))


def _permute_kernel(pos_ref, lo_ref, hi_ref, h_hbm, hs_hbm, tok_ref, buf, lsem, ssem, fsem):
    i = pl.program_id(0)
    last = pl.num_programs(0) - 1
    half = i % 2
    blk = PERM_TM * ROW_SLABS

    def load(block, h):
        return pltpu.make_async_copy(h_hbm.at[pl.ds(block * blk, blk), :], buf.at[pl.ds(h * blk, blk), :], lsem.at[h])

    def wait_rows(sem, rows):
        n = rows * ROW_SLABS
        pltpu.make_async_copy(hs_hbm.at[pl.ds(0, n), :], hs_hbm.at[pl.ds(0, n), :], sem).wait()

    @pl.when(i == 0)
    def _():
        load(0, 0).start()

    load(i, half).wait()

    @pl.when(i >= 1)
    def _():
        wait_rows(ssem.at[1 - half], PERM_TM)

    @pl.when(i < last)
    def _():
        load(i + 1, 1 - half).start()

    base = i * PERM_TM
    for r in range(PERM_TM):
        p = pos_ref[base + r]
        pltpu.make_async_copy(_row_slabs(buf, half * PERM_TM + r), _row_slabs(hs_hbm, p),
                              ssem.at[half]).start(priority=r % 2)
        tok_ref[p] = base + r

    @pl.when(i == 0)
    def _():
        def fill(first, rows):
            n = rows * ROW_SLABS
            pltpu.make_async_copy(buf.at[pl.ds(0, n), :],
                                  hs_hbm.at[pl.ds(pl.multiple_of(first * ROW_SLABS, ROW_SLABS), n), :], fsem).start()

        def zero_tok(p, carry):
            tok_ref[p] = 0
            return carry

        for b in range(N_BUCKETS + 1):
            lax.fori_loop(lo_ref[b], hi_ref[b], zero_tok, 0)
        for b in range(N_BUCKETS):
            first = lo_ref[b]
            length = hi_ref[b] - first
            run = MOE_TM // 2
            while run >= 1:
                take = (length & run) != 0
                pl.when(take)(functools.partial(fill, first, run))
                first = first + jnp.where(take, run, 0)
                run //= 2

        def fill_tile(t, carry):
            fill(t * MOE_TM, MOE_TM)
            return carry

        lax.fori_loop(lo_ref[N_BUCKETS] // MOE_TM, MOE_TILES, fill_tile, 0)
        wait_rows(fsem, MOE_TILES * MOE_TM - N_TOKENS)

    @pl.when(i == last)
    def _():
        wait_rows(ssem.at[half], PERM_TM)


def _permute(h, pos, fill_lo, fill_hi):
    grid_spec = pltpu.PrefetchScalarGridSpec(
        num_scalar_prefetch=3,
        grid=(N_TOKENS // PERM_TM,),
        in_specs=[pl.BlockSpec(memory_space=pl.ANY)],
        out_specs=[pl.BlockSpec(memory_space=pl.ANY), pl.BlockSpec(memory_space=pltpu.SMEM)],
        scratch_shapes=[pltpu.VMEM((2 * PERM_TM * ROW_SLABS, LANES), F32),
                        pltpu.SemaphoreType.DMA((2,)),
                        pltpu.SemaphoreType.DMA((2,)),
                        pltpu.SemaphoreType.DMA(())],
    )
    return pl.pallas_call(
        _permute_kernel,
        grid_spec=grid_spec,
        out_shape=[jax.ShapeDtypeStruct((MOE_TILES * MOE_TM * ROW_SLABS, LANES), F32),
                   jax.ShapeDtypeStruct((MOE_TILES * MOE_TM,), jnp.int32)],
        compiler_params=pltpu.CompilerParams(dimension_semantics=("arbitrary",)),
        name="permute_rows",
    )(pos, fill_lo, fill_hi, h)


def _row_slabs(ref, row):
    return ref.at[pl.ds(pl.multiple_of(row * ROW_SLABS, ROW_SLABS), ROW_SLABS), :]


def _moe_kernel(e0_ref, e1_ref, nv_ref, nu_ref, tok_ref, hs_ref, wrt_ref, wg0_ref, wu0_ref, wd0_ref,
                wg1_ref, wu1_ref, wd1_ref, y_hbm, xb_scr, stg, ssem):
    i = pl.program_id(0)
    n_used = nu_ref[0]

    def wait_scatter(rows):
        n = rows * ROW_SLABS
        pltpu.make_async_copy(y_hbm.at[pl.ds(0, n), :], y_hbm.at[pl.ds(0, n), :], ssem).wait()

    @pl.when(i < n_used)
    def _():
        w0 = wrt_ref[pl.ds(e0_ref[i], 1), :]
        w1 = wrt_ref[pl.ds(e1_ref[i], 1), :]
        l0 = jnp.zeros((MOE_TM, LANES), F32)
        l1 = jnp.zeros((MOE_TM, LANES), F32)
        for j in range(ROW_SLABS):
            xj = hs_ref[_slab(j, MOE_TM), :]
            xb_scr[:, j * LANES:(j + 1) * LANES] = xj.astype(BF16)
            l0 = l0 + xj * w0[:, j * LANES:(j + 1) * LANES]
            l1 = l1 + xj * w1[:, j * LANES:(j + 1) * LANES]
        a0 = jax.nn.sigmoid(jnp.sum(l0, axis=1, keepdims=True))
        a1 = jax.nn.sigmoid(jnp.sum(l1, axis=1, keepdims=True))
        den = a0 + a1
        acc = None
        for gate, wg_ref, wu_ref, wd_ref in ((a0 / den, wg0_ref, wu0_ref, wd0_ref),
                                             (a1 / den, wg1_ref, wu1_ref, wd1_ref)):
            a = jnp.dot(xb_scr[...], wg_ref[...], preferred_element_type=F32)
            b = jnp.dot(xb_scr[...], wu_ref[...], preferred_element_type=F32)
            hid = (_silu(a) * b * gate).astype(BF16)
            o = jnp.dot(hid, wd_ref[...], preferred_element_type=F32)
            acc = o if acc is None else acc + o

        @pl.when(i >= 1)
        def _():
            wait_scatter(nv_ref[jnp.maximum(i - 1, 0)])

        _store_slabs(stg, acc, MOE_TM)
        base = i * MOE_TM

        def scatter_row(r, priority):
            pltpu.make_async_copy(_row_slabs(stg, r), _row_slabs(y_hbm, tok_ref[base + r]),
                                  ssem).start(priority=priority)

        def scatter_rows(c, carry):
            for u in range(DMA_UNROLL):
                scatter_row(c * DMA_UNROLL + u, u % 2)
            return carry

        def scatter_one(r, carry):
            scatter_row(r, 0)
            return carry

        n_full = nv_ref[i] // DMA_UNROLL
        lax.fori_loop(0, n_full, scatter_rows, 0)
        lax.fori_loop(n_full * DMA_UNROLL, nv_ref[i], scatter_one, 0)

        @pl.when(i == n_used - 1)
        def _():
            wait_scatter(nv_ref[i])


def _moe(hs, row_tok, tiles, wrt, wg, wu, wd, layer):
    e0, e1, n_valid, n_used = tiles

    def w_spec(which, shape):
        return pl.BlockSpec((None, None) + shape,
                            lambda i, e0, e1, nv, nu, tok: (layer, (e0, e1)[which][i], 0, 0))

    up_shape, down_shape = (D_MODEL, D_FF), (D_FF, D_MODEL)
    grid_spec = pltpu.PrefetchScalarGridSpec(
        num_scalar_prefetch=5,
        grid=(MOE_TILES,),
        in_specs=[pl.BlockSpec((MOE_TM * ROW_SLABS, LANES),
                               lambda i, e0, e1, nv, nu, tok: (jnp.minimum(i, nu[0] - 1), 0)),
                  pl.BlockSpec((N_EXPERTS, D_MODEL), lambda i, *_: (0, 0)),
                  w_spec(0, up_shape), w_spec(0, up_shape), w_spec(0, down_shape),
                  w_spec(1, up_shape), w_spec(1, up_shape), w_spec(1, down_shape)],
        out_specs=pl.BlockSpec(memory_space=pl.ANY),
        scratch_shapes=[pltpu.VMEM((MOE_TM, D_MODEL), BF16),
                        pltpu.VMEM((MOE_TM * ROW_SLABS, LANES), F32),
                        pltpu.SemaphoreType.DMA(())],
    )
    return pl.pallas_call(
        _moe_kernel,
        grid_spec=grid_spec,
        out_shape=jax.ShapeDtypeStruct((N_TOKENS * ROW_SLABS, LANES), F32),
        compiler_params=pltpu.CompilerParams(dimension_semantics=("arbitrary",),
                                             vmem_limit_bytes=VMEM_LIMIT),
        name="moe_routed",
    )(e0, e1, n_valid, n_used, row_tok, hs, wrt, wg, wu, wd, wg, wu, wd)


def _final_kernel(x1_ref, y_ref, mod_ref, fg_ref, o_ref):
    x2 = x1_ref[...] + mod_ref[5:6, :] * _rows_to_2d(y_ref, x1_ref.shape[0])
    o_ref[...] = _rms(x2, fg_ref[...])


def _final(x1, y, mod_l, fg, *, latent):
    n, own_spec, rows_spec, mod_spec = _stream_specs(latent)
    return pl.pallas_call(
        _final_kernel,
        grid=(x1.shape[0] // n,),
        in_specs=[own_spec, rows_spec, mod_spec, pl.BlockSpec((1, D_MODEL), lambda b: (0, 0))],
        out_specs=own_spec,
        out_shape=jax.ShapeDtypeStruct(x1.shape, F32),
        name="final_norm",
    )(x1, y, mod_l, fg)


def kernel(x_prompt, x_sample, cache_k, cache_v, c, c_ctx, w_ada, b_ada, norm1_g, norm2_g, w_in, w_out, rpb, pool_w, pool_scale, conv_w, conv_b, conv_ln_g, conv_ln_b, conv_pw_w, conv_pw_b, w_router, b_router, moe_w_gate, moe_w_up, moe_w_down, final_g):
    xp = x_prompt.reshape(BATCH * SEQ, D_MODEL)
    xs = x_sample.reshape(DEC_BATCH * DEC_SEQ, D_MODEL)
    cond = jnp.concatenate([c_ctx[None, :], c, jnp.zeros((N_MOD_ROWS - 1 - DEC_BATCH, D_MODEL), F32)], axis=0)
    mod = _ada_modulation(cond, w_ada, b_ada).reshape(DEPTH, N_MOD_ROWS, 6, D_MODEL)
    bias = _rpb_tiles(rpb)
    ck = cache_k.reshape(DEC_BATCH, DEPTH, PAST_LEN, D_ATTN)
    cv = cache_v.reshape(DEC_BATCH, DEPTH, PAST_LEN, D_ATTN)
    wrt = w_router.T
    wr = jnp.pad(w_router, ((0, 0), (0, LANES - N_EXPERTS)))
    wr_hi = wr.astype(BF16)
    wr_lo = (wr - wr_hi.astype(F32)).astype(BF16)
    br = b_router.reshape(N_EXPERTS, 1)
    fg = final_g.reshape(1, D_MODEL)
    wg, wu, wd = moe_w_gate.astype(BF16), moe_w_up.astype(BF16), moe_w_down.astype(BF16)
    new_k = new_v = None
    prev = None
    for l in range(DEPTH):
        pool_bd = jnp.zeros((D_POOL, D_POOL), F32)
        gd = D_POOL // 4
        for g in range(4):
            pool_bd = pool_bd.at[g * gd:(g + 1) * gd, g * gd:(g + 1) * gd].set(pool_w[l, g])
        weights = [norm1_g[l].reshape(1, -1),
                   w_in[l].astype(BF16), w_out[l].astype(BF16),
                   pool_bd.astype(BF16), pool_scale[l].reshape(1, -1),
                   conv_w[l], conv_b[l].reshape(1, -1),
                   conv_ln_g[l].reshape(1, -1), conv_ln_b[l].reshape(1, -1),
                   conv_pw_w[l].astype(BF16), conv_pw_b[l].reshape(1, -1)]

        xp, new_k, new_v = _mixer(xp, mod[l], weights, latent=False, prev=prev, kv_old=(new_k, new_v), layer=l)
        xs, = _mixer(xs, mod[l], weights, latent=True, prev=prev, cache=(ck, cv), bias=bias[l], layer=l)

        h, meta, count = _route(xp, xs, mod[l], norm2_g[l].reshape(1, -1), wr_hi, wr_lo, br)
        perm, tiles = _route_plan(meta[0], meta[1], count[:N_BUCKETS, 0])
        hs, row_tok = _permute(h, *perm)
        y = _moe(hs, row_tok, tiles, wrt, wg, wu, wd, l)
        prev = (y, mod[l])

    y_prompt = _final(xp, y, mod[DEPTH - 1], fg, latent=False).reshape(BATCH, SEQ, D_MODEL)
    y_sample = _final(xs, y, mod[DEPTH - 1], fg, latent=True).reshape(DEC_BATCH, DEC_SEQ, D_MODEL)
    new_cache_k = new_k.reshape(BATCH, DEPTH, SEQ, N_HEADS, HEAD_DIM)
    new_cache_v = new_v.reshape(BATCH, DEPTH, SEQ, N_HEADS, HEAD_DIM)
    return (y_prompt, y_sample, new_cache_k, new_cache_v)
```

```python
import functools

import numpy as np
import jax
import jax.numpy as jnp
from jax import lax
from jax.experimental import pallas as pl
from jax.experimental.pallas import tpu as pltpu

D_MODEL = 1024
BATCH = 32
SEQ = 256
DEPTH = 2
DEC_BATCH = 8
DEC_SEQ = 1024
PAST_LEN = 512
GRID_W = 64
GRID_ROWS = DEC_SEQ // GRID_W
D_ATTN = 512
HEAD_DIM = 64
N_HEADS = 8
N_PAIRS = N_HEADS // 2
WIN_ROWS = 8
WIN_COLS = 16
D_POOL = 256
D_CONV = 256
CONV_WIDTH = 31
D_IN = 3 * D_ATTN + D_POOL + 2 * D_CONV
N_EXPERTS = 16
N_GROUPS = 4
GROUP_SIZE = N_EXPERTS // N_GROUPS
D_FF = 512
EPS = 1e-6
N_MOD_ROWS = 16
N_DR = 2 * WIN_ROWS - 1
N_DC = 2 * WIN_COLS - 1
N_BIAS_TILES = N_DR - 1
LAT_GROUP_ROWS = 4
SEQ_PAD = 16
LANES = 128
SUBLANES = 8
ROW_SLABS = D_MODEL // LANES
N_TOKENS = BATCH * SEQ + DEC_BATCH * DEC_SEQ
PAIRS_PER_GROUP = GROUP_SIZE * (GROUP_SIZE - 1) // 2
N_BUCKETS = N_GROUPS * PAIRS_PER_GROUP
BUCKET_ROWS = 32
META_ROWS = 8
ROUTE_TM = 512
ROUTE_CHUNK = 32
CTX_ROUTE_BLOCKS = BATCH * SEQ // ROUTE_TM
MOE_TM = 256
MOE_TILES = N_TOKENS // MOE_TM + N_BUCKETS
PERM_TM = ROUTE_TM
PERM_GROUP = 8
COUNT_SPLIT = 16
DMA_UNROLL = 8
VMEM_LIMIT = 62 * 1024 * 1024

F32 = jnp.float32
BF16 = jnp.bfloat16
HIGHEST = lax.Precision.HIGHEST
NT_DIMS = (((1,), (1,)), ((), ()))


def _rms(x, g):
    return x * lax.rsqrt(jnp.mean(x * x, axis=-1, keepdims=True) + EPS) * g


def _silu(x):
    return x * jax.nn.sigmoid(x)


def _ada_kernel(cond_ref, w_ref, b_ref, o_ref):
    o_ref[...] = jnp.dot(_silu(cond_ref[...]), w_ref[...], preferred_element_type=F32,
                         precision=HIGHEST) + b_ref[...]


def _ada_modulation(cond, w_ada, b_ada):
    n_col = 6 * D_MODEL // D_MODEL
    return pl.pallas_call(
        _ada_kernel,
        grid=(DEPTH, n_col),
        in_specs=[
            pl.BlockSpec((N_MOD_ROWS, D_MODEL), lambda l, n: (0, 0)),
            pl.BlockSpec((None, D_MODEL, D_MODEL), lambda l, n: (l, 0, n)),
            pl.BlockSpec((None, 1, D_MODEL), lambda l, n: (l, 0, n)),
        ],
        out_specs=pl.BlockSpec((None, N_MOD_ROWS, D_MODEL), lambda l, n: (l, 0, n)),
        out_shape=jax.ShapeDtypeStruct((DEPTH, N_MOD_ROWS, 6 * D_MODEL), F32),
        name="ada_modulation",
    )(cond, w_ada, b_ada.reshape(DEPTH, 1, 6 * D_MODEL))


def _rpb_kernel(rpb_ref, o_ref):
    base = (pl.program_id(0) * N_HEADS + pl.program_id(1)) * (N_DR * N_DC)
    cq = lax.broadcasted_iota(jnp.int32, (GRID_W, LANES), 0)
    lane = lax.broadcasted_iota(jnp.int32, (GRID_W, LANES), 1)
    ck = lane & (GRID_W - 1)
    right = lane >= GRID_W
    dcm = jnp.clip(ck - cq, -(WIN_COLS - 1), WIN_COLS - 1) + (WIN_COLS - 1)
    cs = jnp.clip(cq - WIN_COLS // 2, 0, GRID_W - WIN_COLS)
    valid = (ck >= cs) & (ck < cs + WIN_COLS)
    for d in range(N_BIAS_TILES):
        acc = jnp.zeros((GRID_W, LANES), F32)
        for dc in range(N_DC):
            v = jnp.where(right, rpb_ref[base + (d + 1) * N_DC + dc], rpb_ref[base + d * N_DC + dc])
            acc = jnp.where(dcm == dc, v, acc)
        o_ref[d] = jnp.where(valid, acc, -jnp.inf)


def _rpb_tiles(rpb):
    return pl.pallas_call(
        _rpb_kernel,
        grid=(DEPTH, N_HEADS),
        in_specs=[pl.BlockSpec(memory_space=pltpu.SMEM)],
        out_specs=pl.BlockSpec((None, None, N_BIAS_TILES, GRID_W, LANES), lambda l, h: (l, h, 0, 0, 0)),
        out_shape=jax.ShapeDtypeStruct((DEPTH, N_HEADS, N_BIAS_TILES, GRID_W, LANES), F32),
        name="rpb_tiles",
    )(rpb.reshape(-1))


def _shift_rows(a, s):
    return a if s == 0 else pltpu.roll(a, s % a.shape[0], 0)


def _pad_rows(a):
    return jnp.concatenate([a, jnp.zeros((SEQ_PAD, a.shape[1]), a.dtype)], axis=0)


def _softmax_parts(s):
    m = jnp.max(s, axis=-1, keepdims=True)
    e = jnp.exp(s - m)
    return e, jnp.sum(e, axis=-1, keepdims=True)


def _context_attention(qkv_scr, m_scr, n):
    left = lax.broadcasted_iota(jnp.int32, (n, LANES), 1) < HEAD_DIM
    for p in range(N_PAIRS):
        q = qkv_scr[p]
        k = qkv_scr[N_PAIRS + p]
        v = qkv_scr[2 * N_PAIRS + p]
        zero = jnp.zeros_like(q)
        outs = []
        for qh in (jnp.where(left, q, zero), jnp.where(left, zero, q)):
            s = lax.dot_general(qh, k, NT_DIMS, preferred_element_type=F32)
            e, l = _softmax_parts(s)
            outs.append(jnp.dot(e.astype(BF16), v, preferred_element_type=F32) / l)
        m_scr[p] = jnp.where(left, outs[0], outs[1]).astype(BF16)


def _window_start(qr):
    return min(max(qr - WIN_ROWS // 2, 0), GRID_ROWS - WIN_ROWS)


def _latent_groups():
    groups = []
    for r0 in range(0, GRID_ROWS, LAT_GROUP_ROWS):
        k0 = _window_start(r0)
        k1 = _window_start(r0 + LAT_GROUP_ROWS - 1) + WIN_ROWS
        nk = k1 - k0 + (k1 - k0) % 2
        k0 = min(k0, GRID_ROWS - nk)
        groups.append((r0, LAT_GROUP_ROWS, k0, nk))
    return groups


def _group_bias(bias_ref, p, r0, nq, k0, nk):
    left = lax.broadcasted_iota(jnp.int32, (GRID_W, LANES), 1) < HEAD_DIM
    neg = jnp.full((GRID_W, LANES), -jnp.inf, F32)
    blocks = []
    for hh in range(2):
        for qr in range(r0, r0 + nq):
            rs = _window_start(qr)
            pieces = []
            for kr in range(k0, k0 + nk, 2):
                in0 = rs <= kr < rs + WIN_ROWS
                in1 = rs <= kr + 1 < rs + WIN_ROWS
                d = kr - qr + (WIN_ROWS - 1)
                if in0 and in1:
                    pieces.append(bias_ref[2 * p + hh, d])
                elif in0:
                    pieces.append(jnp.where(left, bias_ref[2 * p + hh, d], neg))
                elif in1:
                    pieces.append(jnp.where(left, neg, bias_ref[2 * p + hh, d]))
                else:
                    pieces.append(neg)
            blocks.append(jnp.concatenate(pieces, axis=1))
    return jnp.concatenate(blocks, axis=0)


def _latent_attention(qkv_scr, kc_scr, vc_scr, bias_ref, m_scr):
    def body(p, carry):
        for r0, nq, k0, nk in _latent_groups():
            rows = nq * GRID_W
            left = lax.broadcasted_iota(jnp.int32, (rows, LANES), 1) < HEAD_DIM
            q = qkv_scr[p, r0 * GRID_W:r0 * GRID_W + rows, :]
            zero = jnp.zeros_like(q)
            qm = jnp.concatenate([jnp.where(left, q, zero), jnp.where(left, zero, q)], axis=0)
            kl = qkv_scr[N_PAIRS + p, k0 * GRID_W:(k0 + nk) * GRID_W, :]
            vl = qkv_scr[2 * N_PAIRS + p, k0 * GRID_W:(k0 + nk) * GRID_W, :]
            s_loc = (lax.dot_general(qm, kl, NT_DIMS, preferred_element_type=F32)
                     + _group_bias(bias_ref, p, r0, nq, k0, nk))
            s_ctx = lax.dot_general(qm, kc_scr[p], NT_DIMS, preferred_element_type=F32)
            m = jnp.maximum(jnp.max(s_loc, axis=-1, keepdims=True), jnp.max(s_ctx, axis=-1, keepdims=True))
            e_loc = jnp.exp(s_loc - m)
            e_ctx = jnp.exp(s_ctx - m)
            l = jnp.sum(e_loc, axis=-1, keepdims=True) + jnp.sum(e_ctx, axis=-1, keepdims=True)
            o = (jnp.dot(e_loc.astype(BF16), vl, preferred_element_type=F32)
                 + jnp.dot(e_ctx.astype(BF16), vc_scr[p], preferred_element_type=F32)) / l
            m_scr[p, r0 * GRID_W:r0 * GRID_W + rows, :] = jnp.where(left, o[:rows], o[rows:]).astype(BF16)
        return carry

    lax.fori_loop(0, N_PAIRS, body, 0)


def _router_bucket(h_hi, h_lo, wr_hi_ref, wr_lo_ref, br_ref, n):
    logits = (jnp.dot(h_hi, wr_hi_ref[...], preferred_element_type=F32)
              + (jnp.dot(h_lo, wr_hi_ref[...], preferred_element_type=F32)
                 + jnp.dot(h_hi, wr_lo_ref[...], preferred_element_type=F32)))
    sel = jax.nn.sigmoid(logits.T[:N_EXPERTS]) + br_ref[...]
    rows = [sel[e:e + 1, :] for e in range(N_EXPERTS)]
    gscore = []
    for g in range(N_GROUPS):
        r4 = rows[g * GROUP_SIZE:(g + 1) * GROUP_SIZE]
        best = None
        for i in range(GROUP_SIZE):
            for j in range(i + 1, GROUP_SIZE):
                s = r4[i] + r4[j]
                best = s if best is None else jnp.maximum(best, s)
        gscore.append(best)
    gbest = jnp.zeros((1, n), jnp.int32)
    gval = gscore[0]
    for g in range(1, N_GROUPS):
        upd = gscore[g] > gval
        gbest = jnp.where(upd, g, gbest)
        gval = jnp.where(upd, gscore[g], gval)
    picked = []
    for e in range(N_EXPERTS):
        g = e // GROUP_SIZE
        rank = jnp.zeros((1, n), jnp.int32)
        for j in range(g * GROUP_SIZE, (g + 1) * GROUP_SIZE):
            if j == e:
                continue
            ahead = (rows[j] > rows[e]) | (rows[j] == rows[e]) if j < e else rows[j] > rows[e]
            rank = rank + ahead.astype(jnp.int32)
        picked.append((rank < 2) & (gbest == g))
    e_lo = jnp.zeros((1, n), jnp.int32)
    e_hi = jnp.zeros((1, n), jnp.int32)
    for e in range(N_EXPERTS):
        e_hi = jnp.where(picked[e], e, e_hi)
        e_lo = jnp.where(picked[N_EXPERTS - 1 - e], N_EXPERTS - 1 - e, e_lo)
    a = e_lo - GROUP_SIZE * gbest
    b = e_hi - GROUP_SIZE * gbest
    return gbest * PAIRS_PER_GROUP + jnp.right_shift(a * (2 * GROUP_SIZE - 1 - a), 1) + (b - a - 1)


def _slab(j, n):
    return pl.ds(j, n, stride=ROW_SLABS)


def _rows_to_2d(ref, n):
    return jnp.concatenate([ref[_slab(j, n), :] for j in range(ROW_SLABS)], axis=1)


def _store_slabs(ref, val, n, first_row=0):
    for j in range(ROW_SLABS):
        ref[_slab(first_row + j, n), :] = val[:, j * LANES:(j + 1) * LANES]


def _mixer_kernel(*refs, latent, has_prev, n):
    it = iter(refs)
    x_ref, mod_ref = next(it), next(it)
    if has_prev:
        y_ref, modp_ref = next(it), next(it)
        if not latent:
            kold_ref, vold_ref = next(it), next(it)
    if latent:
        ck_ref, cv_ref, bias_ref = next(it), next(it), next(it)
    (n1g_ref, w_in_ref, w_out_ref, poolw_ref, pscale_ref, convw_ref, convb_ref,
     lng_ref, lnb_ref, pww_ref, pwb_ref) = [next(it) for _ in range(11)]
    x1_ref = next(it)
    if not latent:
        knew_ref, vnew_ref = next(it), next(it)
    qkv_scr, u_scr, m_scr = next(it), next(it), next(it)
    if latent:
        kc_scr, vc_scr = next(it), next(it)

    x = x_ref[...]
    if has_prev:
        x = x + modp_ref[5:6, :] * _rows_to_2d(y_ref, n)
    layer = 0
    if has_prev and not latent:
        layer = kold_ref.shape[0]
        knew_ref[0:layer] = kold_ref[...]
        vnew_ref[0:layer] = vold_ref[...]
    sh1, sc1, g1 = mod_ref[0:1, :], mod_ref[1:2, :], mod_ref[2:3, :]

    hb = (_rms(x, n1g_ref[...]) * (1.0 + sc1) + sh1).astype(BF16)
    cw = 2 * LANES
    for j in range(D_IN // cw):
        zj = jnp.dot(hb, w_in_ref[:, j * cw:(j + 1) * cw], preferred_element_type=F32)
        if j < 6:
            part = j // 2
            zs = zj * (HEAD_DIM ** -0.5) if part == 0 else zj
            c = part * N_PAIRS + 2 * (j % 2)
            qkv_scr[c] = zs[:, :LANES].astype(BF16)
            qkv_scr[c + 1] = zs[:, LANES:].astype(BF16)
            if not latent and part == 1:
                knew_ref[layer, :, (j % 2) * cw:(j % 2 + 1) * cw] = zj
            if not latent and part == 2:
                vnew_ref[layer, :, (j % 2) * cw:(j % 2 + 1) * cw] = zj
        else:
            u_scr[j - 6] = zj

    if latent:
        for p in range(N_PAIRS):
            kc_scr[p] = ck_ref[:, p * LANES:(p + 1) * LANES].astype(BF16)
            vc_scr[p] = cv_ref[:, p * LANES:(p + 1) * LANES].astype(BF16)
        _latent_attention(qkv_scr, kc_scr, vc_scr, bias_ref, m_scr)
    else:
        _context_attention(qkv_scr, m_scr, n)

    row = lax.broadcasted_iota(jnp.int32, (n, D_POOL), 0)
    grp = lax.broadcasted_iota(jnp.int32, (n, D_POOL), 1) // (D_POOL // 4)
    up = u_scr[0]
    t2 = _pad_rows(up)
    t2 = t2 + _shift_rows(t2, 1)
    t4 = t2 + _shift_rows(t2, 2)
    t8 = t4 + _shift_rows(t4, 4)
    t16 = t8 + _shift_rows(t8, 8)
    s2, s4, s8, s16 = t2[:n], _shift_rows(t4, -1)[:n], _shift_rows(t8, -3)[:n], _shift_rows(t16, -7)[:n]
    wsum = jnp.where(grp == 0, s2, jnp.where(grp == 1, s4, jnp.where(grp == 2, s8, s16)))
    half = jnp.left_shift(1, grp)
    lo = jnp.maximum(row - half, 0)
    hi = jnp.minimum(row + half - 1, n - 1)
    diff = wsum / (hi - lo + 1).astype(F32) - up
    pooled = jnp.dot(diff.astype(BF16), poolw_ref[...], preferred_element_type=F32) * pscale_ref[...]
    m_scr[N_PAIRS] = pooled[:, :LANES].astype(BF16)
    m_scr[N_PAIRS + 1] = pooled[:, LANES:].astype(BF16)

    y = _pad_rows(u_scr[1] * jax.nn.sigmoid(u_scr[2]))
    acc = jnp.zeros(y.shape, F32) + convb_ref[...]
    y_rot = [_shift_rows(y, b) for b in range(SUBLANES)]
    for k in range(CONV_WIDTH):
        s = CONV_WIDTH // 2 - k
        acc = acc + convw_ref[k:k + 1, :] * _shift_rows(y_rot[s % SUBLANES], s - s % SUBLANES)
    acc = acc[:n]
    mu = jnp.mean(acc, axis=-1, keepdims=True)
    cen = acc - mu
    var = jnp.mean(cen * cen, axis=-1, keepdims=True)
    yn = _silu(cen * lax.rsqrt(var + EPS) * lng_ref[...] + lnb_ref[...])
    conv = jnp.dot(yn.astype(BF16), pww_ref[...], preferred_element_type=F32) + pwb_ref[...]
    m_scr[N_PAIRS + 2] = conv[:, :LANES].astype(BF16)
    m_scr[N_PAIRS + 3] = conv[:, LANES:].astype(BF16)

    mixed = jnp.concatenate([m_scr[j] for j in range(ROW_SLABS)], axis=1)
    x1_ref[...] = x + g1 * jnp.dot(mixed, w_out_ref[...], preferred_element_type=F32)


def _route_kernel(xp_ref, xs_ref, mod_ref, n2g_ref, wr_hi_ref, wr_lo_ref, br_ref,
                  h_ref, meta_ref, count_ref, block_count_ref, run_scr, hi_scr, lo_scr):
    i = pl.program_id(0)
    n = ROUTE_TM

    @pl.when(i == 0)
    def _():
        run_scr[...] = jnp.zeros_like(run_scr)

    def norm_rows(x_ref):
        gain, scale, shift = n2g_ref[...], 1.0 + mod_ref[4:5, :], mod_ref[3:4, :]
        for r0 in range(0, n, ROUTE_CHUNK):
            h2 = _rms(x_ref[r0:r0 + ROUTE_CHUNK, :], gain) * scale + shift
            _store_slabs(h_ref, h2, ROUTE_CHUNK, r0 * ROW_SLABS)
            hi = h2.astype(BF16)
            hi_scr[r0:r0 + ROUTE_CHUNK, :] = hi
            lo_scr[r0:r0 + ROUTE_CHUNK, :] = (h2 - hi.astype(F32)).astype(BF16)

    pl.when(i < CTX_ROUTE_BLOCKS)(lambda: norm_rows(xp_ref))
    pl.when(i >= CTX_ROUTE_BLOCKS)(lambda: norm_rows(xs_ref))

    bucket = _router_bucket(hi_scr[...], lo_scr[...], wr_hi_ref, wr_lo_ref, br_ref, n)
    onehot = lax.broadcasted_iota(jnp.int32, (BUCKET_ROWS, n), 0) == bucket
    upper = (lax.broadcasted_iota(jnp.int32, (n, n), 0) <= lax.broadcasted_iota(jnp.int32, (n, n), 1))
    incl = jnp.dot(onehot.astype(BF16), upper.astype(BF16), preferred_element_type=F32)
    run = run_scr[...]
    rank = jnp.sum(jnp.where(onehot, incl + run, 0.0), axis=0, keepdims=True) - 1.0
    in_block = jnp.sum(onehot.astype(F32), axis=1, keepdims=True)
    before = (lax.broadcasted_iota(jnp.int32, (BUCKET_ROWS, BUCKET_ROWS), 0)
              > lax.broadcasted_iota(jnp.int32, (BUCKET_ROWS, BUCKET_ROWS), 1)).astype(BF16)
    coarse = jnp.floor(in_block * (1.0 / COUNT_SPLIT))
    parts = jnp.concatenate([jnp.broadcast_to(coarse, (BUCKET_ROWS, LANES // 2)),
                             jnp.broadcast_to(in_block - COUNT_SPLIT * coarse, (BUCKET_ROWS, LANES // 2))], axis=1)
    totals = jnp.dot(before, parts.astype(BF16), preferred_element_type=F32)
    first = COUNT_SPLIT * totals[:, :1] + totals[:, LANES // 2:LANES // 2 + 1]
    local_pos = jnp.sum(jnp.where(onehot, incl - 1.0 + first, 0.0), axis=0, keepdims=True)
    run = run + in_block
    run_scr[...] = run
    meta_ref[...] = jnp.concatenate(
        [bucket, rank.astype(jnp.int32), local_pos.astype(jnp.int32),
         jnp.zeros((META_ROWS - 3, n), jnp.int32)], axis=0)
    count_ref[...] = jnp.broadcast_to(run, count_ref.shape).astype(jnp.int32)
    block_count_ref[...] = jnp.broadcast_to(in_block, block_count_ref.shape).astype(jnp.int32)


def _route(x1p, x1s, mod_l, n2g, wr_hi, wr_lo, br):
    per_seq = DEC_SEQ // ROUTE_TM
    last_ctx = CTX_ROUTE_BLOCKS - 1
    return pl.pallas_call(
        _route_kernel,
        grid=(N_TOKENS // ROUTE_TM,),
        in_specs=[
            pl.BlockSpec((ROUTE_TM, D_MODEL), lambda i: (jnp.minimum(i, last_ctx), 0)),
            pl.BlockSpec((ROUTE_TM, D_MODEL), lambda i: (jnp.maximum(i - CTX_ROUTE_BLOCKS, 0), 0)),
            pl.BlockSpec((None, 6, D_MODEL),
                         lambda i: (jnp.where(i < CTX_ROUTE_BLOCKS, 0, (i - CTX_ROUTE_BLOCKS) // per_seq + 1), 0, 0)),
            pl.BlockSpec((1, D_MODEL), lambda i: (0, 0)),
            pl.BlockSpec((D_MODEL, LANES), lambda i: (0, 0)),
            pl.BlockSpec((D_MODEL, LANES), lambda i: (0, 0)),
            pl.BlockSpec((N_EXPERTS, 1), lambda i: (0, 0)),
        ],
        out_specs=[pl.BlockSpec((ROUTE_TM * ROW_SLABS, LANES), lambda i: (i, 0)),
                   pl.BlockSpec((META_ROWS, ROUTE_TM), lambda i: (0, i)),
                   pl.BlockSpec((BUCKET_ROWS, LANES), lambda i: (0, 0)),
                   pl.BlockSpec((BUCKET_ROWS, LANES), lambda i: (i, 0))],
        out_shape=[jax.ShapeDtypeStruct((N_TOKENS * ROW_SLABS, LANES), F32),
                   jax.ShapeDtypeStruct((META_ROWS, N_TOKENS), jnp.int32),
                   jax.ShapeDtypeStruct((BUCKET_ROWS, LANES), jnp.int32),
                   jax.ShapeDtypeStruct((N_TOKENS // ROUTE_TM * BUCKET_ROWS, LANES), jnp.int32)],
        scratch_shapes=[pltpu.VMEM((BUCKET_ROWS, 1), F32),
                        pltpu.VMEM((ROUTE_TM, D_MODEL), BF16),
                        pltpu.VMEM((ROUTE_TM, D_MODEL), BF16)],
        compiler_params=pltpu.CompilerParams(dimension_semantics=("arbitrary",)),
        name="norm_route",
    )(x1p, x1s, mod_l, n2g, wr_hi, wr_lo, br)


def _const_spec(shape):
    nd = len(shape)
    return pl.BlockSpec(shape, lambda b: (0,) * nd, pipeline_mode=pl.Buffered(1))


def _stream_specs(latent):
    n = DEC_SEQ if latent else SEQ
    first = BATCH * SEQ // n if latent else 0
    own = pl.BlockSpec((n, D_MODEL), lambda b: (b, 0))
    rows = pl.BlockSpec((n * ROW_SLABS, LANES), lambda b: (first + b, 0))
    mod = pl.BlockSpec((None, 6, D_MODEL), (lambda b: (b + 1, 0, 0)) if latent else (lambda b: (0, 0, 0)))
    return n, own, rows, mod


def _mixer(x, mod_l, weights, *, latent, prev=None, kv_old=None, cache=None, bias=None, layer=0):
    n, own_spec, rows_spec, mod_spec = _stream_specs(latent)
    tok = x.shape[0]
    nb = tok // n
    in_specs = [own_spec, mod_spec]
    args = [x, mod_l]
    if prev is not None:
        in_specs += [rows_spec, mod_spec]
        args += list(prev)
        if not latent:
            in_specs += [pl.BlockSpec((None, layer, n, D_ATTN), lambda b: (b, 0, 0, 0))] * 2
            args += list(kv_old)
    if latent:
        cache_spec = pl.BlockSpec((None, None, PAST_LEN, D_ATTN), lambda b: (b, layer, 0, 0),
                                  pipeline_mode=pl.Buffered(1))
        in_specs += [cache_spec, cache_spec, _const_spec(bias.shape)]
        args += [cache[0], cache[1], bias]
    in_specs += [_const_spec(w.shape) for w in weights]
    args += list(weights)
    out_shape = [jax.ShapeDtypeStruct((tok, D_MODEL), F32)]
    out_specs = [own_spec]
    if not latent:
        out_shape += [jax.ShapeDtypeStruct((nb, layer + 1, n, D_ATTN), F32)] * 2
        out_specs += [pl.BlockSpec((None, layer + 1, n, D_ATTN), lambda b: (b, 0, 0, 0))] * 2
    scratch = [pltpu.VMEM((3 * N_PAIRS, n, LANES), BF16),
               pltpu.VMEM((3, n, D_POOL), F32),
               pltpu.VMEM((ROW_SLABS, n, LANES), BF16)]
    if latent:
        scratch += [pltpu.VMEM((N_PAIRS, PAST_LEN, LANES), BF16),
                    pltpu.VMEM((N_PAIRS, PAST_LEN, LANES), BF16)]
    return pl.pallas_call(
        functools.partial(_mixer_kernel, latent=latent, has_prev=prev is not None, n=n),
        grid=(nb,),
        in_specs=in_specs,
        out_specs=out_specs,
        out_shape=out_shape,
        scratch_shapes=scratch,
        compiler_params=pltpu.CompilerParams(dimension_semantics=("arbitrary",),
                                             vmem_limit_bytes=VMEM_LIMIT),
        name="mixer_latent" if latent else "mixer_context",
    )(*args)


def _bucket_tables():
    lo, hi = [], []
    for g in range(N_GROUPS):
        for a in range(GROUP_SIZE):
            for b in range(a + 1, GROUP_SIZE):
                lo.append(g * GROUP_SIZE + a)
                hi.append(g * GROUP_SIZE + b)
    return np.asarray(lo, np.int32), np.asarray(hi, np.int32)


def _route_plan(bucket, rank, local_pos, count, block_count):
    tiles_b = (count + MOE_TM - 1) // MOE_TM
    tile_end = jnp.cumsum(tiles_b)
    tile_start = tile_end - tiles_b
    pos = tile_start[bucket] * MOE_TM + rank
    run_src = jnp.cumsum(block_count, axis=1) - block_count
    run_dst = tile_start[None, :] * MOE_TM + jnp.cumsum(block_count, axis=0) - block_count
    n_used = tile_end[-1]
    fill_lo = jnp.concatenate([tile_start * MOE_TM + count, (n_used * MOE_TM)[None]])
    fill_hi = jnp.concatenate([tile_end * MOE_TM, jnp.full((1,), MOE_TILES * MOE_TM, jnp.int32)])
    tile = jnp.arange(MOE_TILES, dtype=jnp.int32)
    tile_b = jnp.sum((tile[:, None] >= tile_end[None, :]).astype(jnp.int32), axis=1)
    last_b = jnp.sum((n_used - 1 >= tile_end).astype(jnp.int32))
    tile_b = jnp.minimum(tile_b, last_b)
    n_valid = jnp.clip(count[tile_b] - (tile - tile_start[tile_b]) * MOE_TM, 0, MOE_TM)
    lo_tab, hi_tab = _bucket_tables()
    i32 = lambda a: a.astype(jnp.int32)
    return (i32(local_pos), i32(pos), i32(block_count.reshape(-1)), i32(run_src.reshape(-1)),
            i32(run_dst.reshape(-1)), i32(fill_lo), i32(fill_hi)), (
        jnp.asarray(lo_tab)[tile_b], jnp.asarray(hi_tab)[tile_b], i32(n_valid), i32(n_used).reshape(1))


def _permute_kernel(lpos_ref, pos_ref, rlen_ref, rsrc_ref, rdst_ref, lo_ref, hi_ref,
                    h_hbm, hs_hbm, tok_ref, buf, srt, lsem, ssem, fsem):
    i = pl.program_id(0)
    last = pl.num_programs(0) - 1
    half = i % 2
    blk = PERM_TM * ROW_SLABS

    def load(block, h):
        return pltpu.make_async_copy(h_hbm.at[pl.ds(block * blk, blk), :], buf.at[pl.ds(h * blk, blk), :], lsem.at[h])

    def wait_rows(sem, rows):
        n = rows * ROW_SLABS
        pltpu.make_async_copy(hs_hbm.at[pl.ds(0, n), :], hs_hbm.at[pl.ds(0, n), :], sem).wait()

    def rows_at(ref, first, rows):
        return ref.at[pl.ds(pl.multiple_of(first * ROW_SLABS, ROW_SLABS), rows * ROW_SLABS), :]

    @pl.when(i == 0)
    def _():
        load(0, 0).start()

    load(i, half).wait()

    @pl.when(i < last)
    def _():
        load(i + 1, 1 - half).start()

    @pl.when(i >= 2)
    def _():
        wait_rows(ssem.at[half], PERM_TM)

    base = i * PERM_TM
    for g0 in range(0, PERM_TM, PERM_GROUP):
        rows = range(g0, g0 + PERM_GROUP)
        local = [lpos_ref[base + r] for r in rows]
        sorted_row = [pos_ref[base + r] for r in rows]
        for r, l in zip(rows, local):
            srt[pl.ds(pl.multiple_of((half * PERM_TM + l) * ROW_SLABS, ROW_SLABS), ROW_SLABS), :] = (
                buf[pl.ds(pl.multiple_of((half * PERM_TM + r) * ROW_SLABS, ROW_SLABS), ROW_SLABS), :])
        for r, p in zip(rows, sorted_row):
            tok_ref[p] = base + r

    for b in range(N_BUCKETS):
        length = rlen_ref[i * N_BUCKETS + b]
        src = half * PERM_TM + rsrc_ref[i * N_BUCKETS + b]
        dst = rdst_ref[i * N_BUCKETS + b]
        run = PERM_TM
        while run >= 1:
            take = (length & run) != 0

            @pl.when(take)
            def _(src=src, dst=dst, run=run):
                pltpu.make_async_copy(rows_at(srt, src, run), rows_at(hs_hbm, dst, run),
                                      ssem.at[half]).start(priority=b % 2)

            step = jnp.where(take, run, 0)
            src, dst = src + step, dst + step
            run //= 2

    @pl.when(i == 0)
    def _():
        def fill(first, rows):
            n = rows * ROW_SLABS
            pltpu.make_async_copy(buf.at[pl.ds(0, n), :],
                                  hs_hbm.at[pl.ds(pl.multiple_of(first * ROW_SLABS, ROW_SLABS), n), :], fsem).start()

        def zero_tok(p, carry):
            tok_ref[p] = 0
            return carry

        for b in range(N_BUCKETS + 1):
            lax.fori_loop(lo_ref[b], hi_ref[b], zero_tok, 0)
        for b in range(N_BUCKETS):
            first = lo_ref[b]
            length = hi_ref[b] - first
            run = MOE_TM // 2
            while run >= 1:
                take = (length & run) != 0
                pl.when(take)(functools.partial(fill, first, run))
                first = first + jnp.where(take, run, 0)
                run //= 2

        def fill_tile(t, carry):
            fill(t * MOE_TM, MOE_TM)
            return carry

        lax.fori_loop(lo_ref[N_BUCKETS] // MOE_TM, MOE_TILES, fill_tile, 0)
        wait_rows(fsem, MOE_TILES * MOE_TM - N_TOKENS)

    @pl.when(i == last)
    def _():
        wait_rows(ssem.at[half], PERM_TM)

        @pl.when(i >= 1)
        def _():
            wait_rows(ssem.at[1 - half], PERM_TM)


def _permute(h, perm):
    grid_spec = pltpu.PrefetchScalarGridSpec(
        num_scalar_prefetch=len(perm),
        grid=(N_TOKENS // PERM_TM,),
        in_specs=[pl.BlockSpec(memory_space=pl.ANY)],
        out_specs=[pl.BlockSpec(memory_space=pl.ANY), pl.BlockSpec(memory_space=pltpu.SMEM)],
        scratch_shapes=[pltpu.VMEM((2 * PERM_TM * ROW_SLABS, LANES), F32),
                        pltpu.VMEM((2 * PERM_TM * ROW_SLABS, LANES), F32),
                        pltpu.SemaphoreType.DMA((2,)),
                        pltpu.SemaphoreType.DMA((2,)),
                        pltpu.SemaphoreType.DMA(())],
    )
    return pl.pallas_call(
        _permute_kernel,
        grid_spec=grid_spec,
        out_shape=[jax.ShapeDtypeStruct((MOE_TILES * MOE_TM * ROW_SLABS, LANES), F32),
                   jax.ShapeDtypeStruct((MOE_TILES * MOE_TM,), jnp.int32)],
        compiler_params=pltpu.CompilerParams(dimension_semantics=("arbitrary",)),
        name="permute_rows",
    )(*perm, h)


def _row_slabs(ref, row):
    return ref.at[pl.ds(pl.multiple_of(row * ROW_SLABS, ROW_SLABS), ROW_SLABS), :]


def _moe_kernel(e0_ref, e1_ref, nv_ref, nu_ref, tok_ref, hs_ref, wrt_ref, wg0_ref, wu0_ref, wd0_ref,
                wg1_ref, wu1_ref, wd1_ref, y_hbm, xb_scr, stg, ssem):
    i = pl.program_id(0)
    n_used = nu_ref[0]

    def wait_scatter(rows):
        n = rows * ROW_SLABS
        pltpu.make_async_copy(y_hbm.at[pl.ds(0, n), :], y_hbm.at[pl.ds(0, n), :], ssem).wait()

    @pl.when(i < n_used)
    def _():
        w0 = wrt_ref[pl.ds(e0_ref[i], 1), :]
        w1 = wrt_ref[pl.ds(e1_ref[i], 1), :]
        l0 = jnp.zeros((MOE_TM, LANES), F32)
        l1 = jnp.zeros((MOE_TM, LANES), F32)
        for j in range(ROW_SLABS):
            xj = hs_ref[_slab(j, MOE_TM), :]
            xb_scr[:, j * LANES:(j + 1) * LANES] = xj.astype(BF16)
            l0 = l0 + xj * w0[:, j * LANES:(j + 1) * LANES]
            l1 = l1 + xj * w1[:, j * LANES:(j + 1) * LANES]
        a0 = jax.nn.sigmoid(jnp.sum(l0, axis=1, keepdims=True))
        a1 = jax.nn.sigmoid(jnp.sum(l1, axis=1, keepdims=True))
        den = a0 + a1
        acc = None
        for gate, wg_ref, wu_ref, wd_ref in ((a0 / den, wg0_ref, wu0_ref, wd0_ref),
                                             (a1 / den, wg1_ref, wu1_ref, wd1_ref)):
            a = jnp.dot(xb_scr[...], wg_ref[...], preferred_element_type=F32)
            b = jnp.dot(xb_scr[...], wu_ref[...], preferred_element_type=F32)
            hid = (_silu(a) * b * gate).astype(BF16)
            o = jnp.dot(hid, wd_ref[...], preferred_element_type=F32)
            acc = o if acc is None else acc + o

        @pl.when(i >= 1)
        def _():
            wait_scatter(nv_ref[jnp.maximum(i - 1, 0)])

        _store_slabs(stg, acc, MOE_TM)
        base = i * MOE_TM

        def scatter_row(r, priority):
            pltpu.make_async_copy(_row_slabs(stg, r), _row_slabs(y_hbm, tok_ref[base + r]),
                                  ssem).start(priority=priority)

        def scatter_rows(c, carry):
            for u in range(DMA_UNROLL):
                scatter_row(c * DMA_UNROLL + u, u % 2)
            return carry

        def scatter_one(r, carry):
            scatter_row(r, 0)
            return carry

        n_full = nv_ref[i] // DMA_UNROLL
        lax.fori_loop(0, n_full, scatter_rows, 0)
        lax.fori_loop(n_full * DMA_UNROLL, nv_ref[i], scatter_one, 0)

        @pl.when(i == n_used - 1)
        def _():
            wait_scatter(nv_ref[i])


def _moe(hs, row_tok, tiles, wrt, wg, wu, wd, layer):
    e0, e1, n_valid, n_used = tiles

    def w_spec(which, shape):
        return pl.BlockSpec((None, None) + shape,
                            lambda i, e0, e1, nv, nu, tok: (layer, (e0, e1)[which][i], 0, 0))

    up_shape, down_shape = (D_MODEL, D_FF), (D_FF, D_MODEL)
    grid_spec = pltpu.PrefetchScalarGridSpec(
        num_scalar_prefetch=5,
        grid=(MOE_TILES,),
        in_specs=[pl.BlockSpec((MOE_TM * ROW_SLABS, LANES),
                               lambda i, e0, e1, nv, nu, tok: (jnp.minimum(i, nu[0] - 1), 0)),
                  pl.BlockSpec((N_EXPERTS, D_MODEL), lambda i, *_: (0, 0)),
                  w_spec(0, up_shape), w_spec(0, up_shape), w_spec(0, down_shape),
                  w_spec(1, up_shape), w_spec(1, up_shape), w_spec(1, down_shape)],
        out_specs=pl.BlockSpec(memory_space=pl.ANY),
        scratch_shapes=[pltpu.VMEM((MOE_TM, D_MODEL), BF16),
                        pltpu.VMEM((MOE_TM * ROW_SLABS, LANES), F32),
                        pltpu.SemaphoreType.DMA(())],
    )
    return pl.pallas_call(
        _moe_kernel,
        grid_spec=grid_spec,
        out_shape=jax.ShapeDtypeStruct((N_TOKENS * ROW_SLABS, LANES), F32),
        compiler_params=pltpu.CompilerParams(dimension_semantics=("arbitrary",),
                                             vmem_limit_bytes=VMEM_LIMIT),
        name="moe_routed",
    )(e0, e1, n_valid, n_used, row_tok, hs, wrt, wg, wu, wd, wg, wu, wd)


def _final_kernel(x1_ref, y_ref, mod_ref, fg_ref, o_ref):
    x2 = x1_ref[...] + mod_ref[5:6, :] * _rows_to_2d(y_ref, x1_ref.shape[0])
    o_ref[...] = _rms(x2, fg_ref[...])


def _final(x1, y, mod_l, fg, *, latent):
    n, own_spec, rows_spec, mod_spec = _stream_specs(latent)
    return pl.pallas_call(
        _final_kernel,
        grid=(x1.shape[0] // n,),
        in_specs=[own_spec, rows_spec, mod_spec, pl.BlockSpec((1, D_MODEL), lambda b: (0, 0))],
        out_specs=own_spec,
        out_shape=jax.ShapeDtypeStruct(x1.shape, F32),
        name="final_norm",
    )(x1, y, mod_l, fg)


def kernel(x_prompt, x_sample, cache_k, cache_v, c, c_ctx, w_ada, b_ada, norm1_g, norm2_g, w_in, w_out, rpb, pool_w, pool_scale, conv_w, conv_b, conv_ln_g, conv_ln_b, conv_pw_w, conv_pw_b, w_router, b_router, moe_w_gate, moe_w_up, moe_w_down, final_g):
    xp = x_prompt.reshape(BATCH * SEQ, D_MODEL)
    xs = x_sample.reshape(DEC_BATCH * DEC_SEQ, D_MODEL)
    cond = jnp.concatenate([c_ctx[None, :], c, jnp.zeros((N_MOD_ROWS - 1 - DEC_BATCH, D_MODEL), F32)], axis=0)
    mod = _ada_modulation(cond, w_ada, b_ada).reshape(DEPTH, N_MOD_ROWS, 6, D_MODEL)
    bias = _rpb_tiles(rpb)
    ck = cache_k.reshape(DEC_BATCH, DEPTH, PAST_LEN, D_ATTN)
    cv = cache_v.reshape(DEC_BATCH, DEPTH, PAST_LEN, D_ATTN)
    wrt = w_router.T
    wr = jnp.pad(w_router, ((0, 0), (0, LANES - N_EXPERTS)))
    wr_hi = wr.astype(BF16)
    wr_lo = (wr - wr_hi.astype(F32)).astype(BF16)
    br = b_router.reshape(N_EXPERTS, 1)
    fg = final_g.reshape(1, D_MODEL)
    wg, wu, wd = moe_w_gate.astype(BF16), moe_w_up.astype(BF16), moe_w_down.astype(BF16)
    new_k = new_v = None
    prev = None
    for l in range(DEPTH):
        pool_bd = jnp.zeros((D_POOL, D_POOL), F32)
        gd = D_POOL // 4
        for g in range(4):
            pool_bd = pool_bd.at[g * gd:(g + 1) * gd, g * gd:(g + 1) * gd].set(pool_w[l, g])
        weights = [norm1_g[l].reshape(1, -1),
                   w_in[l].astype(BF16), w_out[l].astype(BF16),
                   pool_bd.astype(BF16), pool_scale[l].reshape(1, -1),
                   conv_w[l], conv_b[l].reshape(1, -1),
                   conv_ln_g[l].reshape(1, -1), conv_ln_b[l].reshape(1, -1),
                   conv_pw_w[l].astype(BF16), conv_pw_b[l].reshape(1, -1)]

        xp, new_k, new_v = _mixer(xp, mod[l], weights, latent=False, prev=prev, kv_old=(new_k, new_v), layer=l)
        xs, = _mixer(xs, mod[l], weights, latent=True, prev=prev, cache=(ck, cv), bias=bias[l], layer=l)

        h, meta, count, block_count = _route(xp, xs, mod[l], norm2_g[l].reshape(1, -1), wr_hi, wr_lo, br)
        block_count = block_count.reshape(N_TOKENS // ROUTE_TM, BUCKET_ROWS, LANES)[:, :N_BUCKETS, 0]
        perm, tiles = _route_plan(meta[0], meta[1], meta[2], count[:N_BUCKETS, 0], block_count)
        hs, row_tok = _permute(h, perm)
        y = _moe(hs, row_tok, tiles, wrt, wg, wu, wd, l)
        prev = (y, mod[l])

    y_prompt = _final(xp, y, mod[DEPTH - 1], fg, latent=False).reshape(BATCH, SEQ, D_MODEL)
    y_sample = _final(xs, y, mod[DEPTH - 1], fg, latent=True).reshape(DEC_BATCH, DEC_SEQ, D_MODEL)
    new_cache_k = new_k.reshape(BATCH, DEPTH, SEQ, N_HEADS, HEAD_DIM)
    new_cache_v = new_v.reshape(BATCH, DEPTH, SEQ, N_HEADS, HEAD_DIM)
    return (y_prompt, y_sample, new_cache_k, new_cache_v)
```

```python
import functools

import numpy as np
import jax
import jax.numpy as jnp
from jax import lax
from jax.experimental import pallas as pl
from jax.experimental.pallas import tpu as pltpu

D_MODEL = 1024
BATCH = 32
SEQ = 256
DEPTH = 2
DEC_BATCH = 8
DEC_SEQ = 1024
PAST_LEN = 512
GRID_W = 64
GRID_ROWS = DEC_SEQ // GRID_W
D_ATTN = 512
HEAD_DIM = 64
N_HEADS = 8
N_PAIRS = N_HEADS // 2
WIN_ROWS = 8
WIN_COLS = 16
D_POOL = 256
D_CONV = 256
CONV_WIDTH = 31
D_IN = 3 * D_ATTN + D_POOL + 2 * D_CONV
N_EXPERTS = 16
N_GROUPS = 4
GROUP_SIZE = N_EXPERTS // N_GROUPS
D_FF = 512
EPS = 1e-6
N_MOD_ROWS = 16
N_DR = 2 * WIN_ROWS - 1
N_DC = 2 * WIN_COLS - 1
N_BIAS_TILES = N_DR - 1
LAT_GROUP_ROWS = 4
SEQ_PAD = 16
LANES = 128
SUBLANES = 8
ROW_SLABS = D_MODEL // LANES
N_TOKENS = BATCH * SEQ + DEC_BATCH * DEC_SEQ
PAIRS_PER_GROUP = GROUP_SIZE * (GROUP_SIZE - 1) // 2
N_BUCKETS = N_GROUPS * PAIRS_PER_GROUP
BUCKET_ROWS = 32
META_ROWS = 8
ROUTE_TM = 512
ROUTE_CHUNK = 32
CTX_ROUTE_BLOCKS = BATCH * SEQ // ROUTE_TM
MOE_TM = 256
MOE_TILES = N_TOKENS // MOE_TM + N_BUCKETS
PERM_TM = 512
DMA_UNROLL = 8
VMEM_LIMIT = 62 * 1024 * 1024

F32 = jnp.float32
BF16 = jnp.bfloat16
HIGHEST = lax.Precision.HIGHEST
NT_DIMS = (((1,), (1,)), ((), ()))


def _rms(x, g):
    return x * lax.rsqrt(jnp.mean(x * x, axis=-1, keepdims=True) + EPS) * g


def _silu(x):
    return x * jax.nn.sigmoid(x)


def _ada_kernel(cond_ref, w_ref, b_ref, o_ref):
    o_ref[...] = jnp.dot(_silu(cond_ref[...]), w_ref[...], preferred_element_type=F32,
                         precision=HIGHEST) + b_ref[...]


def _ada_modulation(cond, w_ada, b_ada):
    n_col = 6 * D_MODEL // D_MODEL
    return pl.pallas_call(
        _ada_kernel,
        grid=(DEPTH, n_col),
        in_specs=[
            pl.BlockSpec((N_MOD_ROWS, D_MODEL), lambda l, n: (0, 0)),
            pl.BlockSpec((None, D_MODEL, D_MODEL), lambda l, n: (l, 0, n)),
            pl.BlockSpec((None, 1, D_MODEL), lambda l, n: (l, 0, n)),
        ],
        out_specs=pl.BlockSpec((None, N_MOD_ROWS, D_MODEL), lambda l, n: (l, 0, n)),
        out_shape=jax.ShapeDtypeStruct((DEPTH, N_MOD_ROWS, 6 * D_MODEL), F32),
        name="ada_modulation",
    )(cond, w_ada, b_ada.reshape(DEPTH, 1, 6 * D_MODEL))


def _rpb_kernel(rpb_ref, o_ref):
    base = (pl.program_id(0) * N_HEADS + pl.program_id(1)) * (N_DR * N_DC)
    cq = lax.broadcasted_iota(jnp.int32, (GRID_W, LANES), 0)
    lane = lax.broadcasted_iota(jnp.int32, (GRID_W, LANES), 1)
    ck = lane & (GRID_W - 1)
    right = lane >= GRID_W
    dcm = jnp.clip(ck - cq, -(WIN_COLS - 1), WIN_COLS - 1) + (WIN_COLS - 1)
    cs = jnp.clip(cq - WIN_COLS // 2, 0, GRID_W - WIN_COLS)
    valid = (ck >= cs) & (ck < cs + WIN_COLS)
    for d in range(N_BIAS_TILES):
        acc = jnp.zeros((GRID_W, LANES), F32)
        for dc in range(N_DC):
            v = jnp.where(right, rpb_ref[base + (d + 1) * N_DC + dc], rpb_ref[base + d * N_DC + dc])
            acc = jnp.where(dcm == dc, v, acc)
        o_ref[d] = jnp.where(valid, acc, -jnp.inf)


def _rpb_tiles(rpb):
    return pl.pallas_call(
        _rpb_kernel,
        grid=(DEPTH, N_HEADS),
        in_specs=[pl.BlockSpec(memory_space=pltpu.SMEM)],
        out_specs=pl.BlockSpec((None, None, N_BIAS_TILES, GRID_W, LANES), lambda l, h: (l, h, 0, 0, 0)),
        out_shape=jax.ShapeDtypeStruct((DEPTH, N_HEADS, N_BIAS_TILES, GRID_W, LANES), F32),
        name="rpb_tiles",
    )(rpb.reshape(-1))


def _shift_rows(a, s):
    return a if s == 0 else pltpu.roll(a, s % a.shape[0], 0)


def _pad_rows(a):
    return jnp.concatenate([a, jnp.zeros((SEQ_PAD, a.shape[1]), a.dtype)], axis=0)


def _softmax_parts(s):
    m = jnp.max(s, axis=-1, keepdims=True)
    e = jnp.exp(s - m)
    return e, jnp.sum(e, axis=-1, keepdims=True)


def _context_attention(qkv_scr, m_scr, n):
    left = lax.broadcasted_iota(jnp.int32, (n, LANES), 1) < HEAD_DIM
    for p in range(N_PAIRS):
        q = qkv_scr[p]
        k = qkv_scr[N_PAIRS + p]
        v = qkv_scr[2 * N_PAIRS + p]
        zero = jnp.zeros_like(q)
        qm = jnp.concatenate([jnp.where(left, q, zero), jnp.where(left, zero, q)], axis=0)
        e, l = _softmax_parts(lax.dot_general(qm, k, NT_DIMS, preferred_element_type=F32))
        o = jnp.dot(e.astype(BF16), v, preferred_element_type=F32) / l
        m_scr[p] = jnp.where(left, o[:n], o[n:]).astype(BF16)


def _window_start(qr):
    return min(max(qr - WIN_ROWS // 2, 0), GRID_ROWS - WIN_ROWS)


def _latent_groups():
    groups = []
    for r0 in range(0, GRID_ROWS, LAT_GROUP_ROWS):
        k0 = _window_start(r0)
        k1 = _window_start(r0 + LAT_GROUP_ROWS - 1) + WIN_ROWS
        nk = k1 - k0 + (k1 - k0) % 2
        k0 = min(k0, GRID_ROWS - nk)
        groups.append((r0, LAT_GROUP_ROWS, k0, nk))
    return groups


def _group_bias(bias_ref, p, r0, nq, k0, nk):
    left = lax.broadcasted_iota(jnp.int32, (GRID_W, LANES), 1) < HEAD_DIM
    neg = jnp.full((GRID_W, LANES), -jnp.inf, F32)
    blocks = []
    for hh in range(2):
        for qr in range(r0, r0 + nq):
            rs = _window_start(qr)
            pieces = []
            for kr in range(k0, k0 + nk, 2):
                in0 = rs <= kr < rs + WIN_ROWS
                in1 = rs <= kr + 1 < rs + WIN_ROWS
                d = kr - qr + (WIN_ROWS - 1)
                if in0 and in1:
                    pieces.append(bias_ref[2 * p + hh, d])
                elif in0:
                    pieces.append(jnp.where(left, bias_ref[2 * p + hh, d], neg))
                elif in1:
                    pieces.append(jnp.where(left, neg, bias_ref[2 * p + hh, d]))
                else:
                    pieces.append(neg)
            blocks.append(jnp.concatenate(pieces, axis=1))
    return jnp.concatenate(blocks, axis=0)


def _latent_attention(qkv_scr, kc_scr, vc_scr, bias_ref, m_scr):
    def body(p, carry):
        for r0, nq, k0, nk in _latent_groups():
            rows = nq * GRID_W
            left = lax.broadcasted_iota(jnp.int32, (rows, LANES), 1) < HEAD_DIM
            q = qkv_scr[p, r0 * GRID_W:r0 * GRID_W + rows, :]
            zero = jnp.zeros_like(q)
            qm = jnp.concatenate([jnp.where(left, q, zero), jnp.where(left, zero, q)], axis=0)
            kl = qkv_scr[N_PAIRS + p, k0 * GRID_W:(k0 + nk) * GRID_W, :]
            vl = qkv_scr[2 * N_PAIRS + p, k0 * GRID_W:(k0 + nk) * GRID_W, :]
            s_loc = (lax.dot_general(qm, kl, NT_DIMS, preferred_element_type=F32)
                     + _group_bias(bias_ref, p, r0, nq, k0, nk))
            s_ctx = lax.dot_general(qm, kc_scr[p], NT_DIMS, preferred_element_type=F32)
            m = jnp.maximum(jnp.max(s_loc, axis=-1, keepdims=True), jnp.max(s_ctx, axis=-1, keepdims=True))
            e_loc = jnp.exp(s_loc - m)
            e_ctx = jnp.exp(s_ctx - m)
            l = jnp.sum(e_loc, axis=-1, keepdims=True) + jnp.sum(e_ctx, axis=-1, keepdims=True)
            o = (jnp.dot(e_loc.astype(BF16), vl, preferred_element_type=F32)
                 + jnp.dot(e_ctx.astype(BF16), vc_scr[p], preferred_element_type=F32)) / l
            m_scr[p, r0 * GRID_W:r0 * GRID_W + rows, :] = jnp.where(left, o[:rows], o[rows:]).astype(BF16)
        return carry

    lax.fori_loop(0, N_PAIRS, body, 0)


def _router_bucket(h_hi, h_lo, wr_hi_ref, wr_lo_ref, br_ref, n):
    logits = (jnp.dot(h_hi, wr_hi_ref[...], preferred_element_type=F32)
              + (jnp.dot(h_lo, wr_hi_ref[...], preferred_element_type=F32)
                 + jnp.dot(h_hi, wr_lo_ref[...], preferred_element_type=F32)))
    sel = jax.nn.sigmoid(logits.T[:N_EXPERTS]) + br_ref[...]
    rows = [sel[e:e + 1, :] for e in range(N_EXPERTS)]
    gscore = []
    for g in range(N_GROUPS):
        r4 = rows[g * GROUP_SIZE:(g + 1) * GROUP_SIZE]
        best = None
        for i in range(GROUP_SIZE):
            for j in range(i + 1, GROUP_SIZE):
                s = r4[i] + r4[j]
                best = s if best is None else jnp.maximum(best, s)
        gscore.append(best)
    gbest = jnp.zeros((1, n), jnp.int32)
    gval = gscore[0]
    for g in range(1, N_GROUPS):
        upd = gscore[g] > gval
        gbest = jnp.where(upd, g, gbest)
        gval = jnp.where(upd, gscore[g], gval)
    picked = []
    for e in range(N_EXPERTS):
        g = e // GROUP_SIZE
        rank = jnp.zeros((1, n), jnp.int32)
        for j in range(g * GROUP_SIZE, (g + 1) * GROUP_SIZE):
            if j == e:
                continue
            ahead = (rows[j] > rows[e]) | (rows[j] == rows[e]) if j < e else rows[j] > rows[e]
            rank = rank + ahead.astype(jnp.int32)
        picked.append((rank < 2) & (gbest == g))
    e_lo = jnp.zeros((1, n), jnp.int32)
    e_hi = jnp.zeros((1, n), jnp.int32)
    for e in range(N_EXPERTS):
        e_hi = jnp.where(picked[e], e, e_hi)
        e_lo = jnp.where(picked[N_EXPERTS - 1 - e], N_EXPERTS - 1 - e, e_lo)
    a = e_lo - GROUP_SIZE * gbest
    b = e_hi - GROUP_SIZE * gbest
    return gbest * PAIRS_PER_GROUP + jnp.right_shift(a * (2 * GROUP_SIZE - 1 - a), 1) + (b - a - 1)


def _slab(j, n):
    return pl.ds(j, n, stride=ROW_SLABS)


def _rows_to_2d(ref, n):
    return jnp.concatenate([ref[_slab(j, n), :] for j in range(ROW_SLABS)], axis=1)


def _store_slabs(ref, val, n, first_row=0):
    for j in range(ROW_SLABS):
        ref[_slab(first_row + j, n), :] = val[:, j * LANES:(j + 1) * LANES]


def _mixer_kernel(*refs, latent, has_prev, n):
    it = iter(refs)
    x_ref, mod_ref = next(it), next(it)
    if has_prev:
        y_ref, modp_ref = next(it), next(it)
        if not latent:
            kold_ref, vold_ref = next(it), next(it)
    if latent:
        ck_ref, cv_ref, bias_ref = next(it), next(it), next(it)
    (n1g_ref, w_in_ref, w_out_ref, poolw_ref, pscale_ref, convw_ref, convb_ref,
     lng_ref, lnb_ref, pww_ref, pwb_ref) = [next(it) for _ in range(11)]
    x1_ref = next(it)
    if not latent:
        knew_ref, vnew_ref = next(it), next(it)
    qkv_scr, u_scr, m_scr = next(it), next(it), next(it)
    if latent:
        kc_scr, vc_scr = next(it), next(it)

    x = x_ref[...]
    if has_prev:
        x = x + modp_ref[5:6, :] * _rows_to_2d(y_ref, n)
    layer = 0
    if has_prev and not latent:
        layer = kold_ref.shape[0]
        knew_ref[0:layer] = kold_ref[...]
        vnew_ref[0:layer] = vold_ref[...]
    sh1, sc1, g1 = mod_ref[0:1, :], mod_ref[1:2, :], mod_ref[2:3, :]

    hb = (_rms(x, n1g_ref[...]) * (1.0 + sc1) + sh1).astype(BF16)
    cw = 2 * LANES
    for j in range(D_IN // cw):
        zj = jnp.dot(hb, w_in_ref[:, j * cw:(j + 1) * cw], preferred_element_type=F32)
        if j < 6:
            part = j // 2
            zs = zj * (HEAD_DIM ** -0.5) if part == 0 else zj
            c = part * N_PAIRS + 2 * (j % 2)
            qkv_scr[c] = zs[:, :LANES].astype(BF16)
            qkv_scr[c + 1] = zs[:, LANES:].astype(BF16)
            if not latent and part == 1:
                knew_ref[layer, :, (j % 2) * cw:(j % 2 + 1) * cw] = zj
            if not latent and part == 2:
                vnew_ref[layer, :, (j % 2) * cw:(j % 2 + 1) * cw] = zj
        else:
            u_scr[j - 6] = zj

    if latent:
        for p in range(N_PAIRS):
            kc_scr[p] = ck_ref[:, p * LANES:(p + 1) * LANES].astype(BF16)
            vc_scr[p] = cv_ref[:, p * LANES:(p + 1) * LANES].astype(BF16)
        _latent_attention(qkv_scr, kc_scr, vc_scr, bias_ref, m_scr)
    else:
        _context_attention(qkv_scr, m_scr, n)

    row = lax.broadcasted_iota(jnp.int32, (n, D_POOL), 0)
    grp = lax.broadcasted_iota(jnp.int32, (n, D_POOL), 1) // (D_POOL // 4)
    up = u_scr[0]
    t2 = _pad_rows(up)
    t2 = t2 + _shift_rows(t2, 1)
    t4 = t2 + _shift_rows(t2, 2)
    t8 = t4 + _shift_rows(t4, 4)
    t16 = t8 + _shift_rows(t8, 8)
    s2, s4, s8, s16 = t2[:n], _shift_rows(t4, -1)[:n], _shift_rows(t8, -3)[:n], _shift_rows(t16, -7)[:n]
    wsum = jnp.where(grp == 0, s2, jnp.where(grp == 1, s4, jnp.where(grp == 2, s8, s16)))
    half = jnp.left_shift(1, grp)
    lo = jnp.maximum(row - half, 0)
    hi = jnp.minimum(row + half - 1, n - 1)
    diff = wsum / (hi - lo + 1).astype(F32) - up
    pooled = jnp.dot(diff.astype(BF16), poolw_ref[...], preferred_element_type=F32) * pscale_ref[...]
    m_scr[N_PAIRS] = pooled[:, :LANES].astype(BF16)
    m_scr[N_PAIRS + 1] = pooled[:, LANES:].astype(BF16)

    y = _pad_rows(u_scr[1] * jax.nn.sigmoid(u_scr[2]))
    acc = jnp.zeros(y.shape, F32) + convb_ref[...]
    y_rot = [_shift_rows(y, b) for b in range(SUBLANES)]
    for k in range(CONV_WIDTH):
        s = CONV_WIDTH // 2 - k
        acc = acc + convw_ref[k:k + 1, :] * _shift_rows(y_rot[s % SUBLANES], s - s % SUBLANES)
    acc = acc[:n]
    mu = jnp.mean(acc, axis=-1, keepdims=True)
    cen = acc - mu
    var = jnp.mean(cen * cen, axis=-1, keepdims=True)
    yn = _silu(cen * lax.rsqrt(var + EPS) * lng_ref[...] + lnb_ref[...])
    conv = jnp.dot(yn.astype(BF16), pww_ref[...], preferred_element_type=F32) + pwb_ref[...]
    m_scr[N_PAIRS + 2] = conv[:, :LANES].astype(BF16)
    m_scr[N_PAIRS + 3] = conv[:, LANES:].astype(BF16)

    mixed = jnp.concatenate([m_scr[j] for j in range(ROW_SLABS)], axis=1)
    x1_ref[...] = x + g1 * jnp.dot(mixed, w_out_ref[...], preferred_element_type=F32)


def _route_kernel(xp_ref, xs_ref, mod_ref, n2g_ref, wr_hi_ref, wr_lo_ref, br_ref,
                  h_ref, meta_ref, count_ref, run_scr, hi_scr, lo_scr):
    i = pl.program_id(0)
    n = ROUTE_TM

    @pl.when(i == 0)
    def _():
        run_scr[...] = jnp.zeros_like(run_scr)

    def norm_rows(x_ref):
        gain, scale, shift = n2g_ref[...], 1.0 + mod_ref[4:5, :], mod_ref[3:4, :]
        for r0 in range(0, n, ROUTE_CHUNK):
            h2 = _rms(x_ref[r0:r0 + ROUTE_CHUNK, :], gain) * scale + shift
            _store_slabs(h_ref, h2, ROUTE_CHUNK, r0 * ROW_SLABS)
            hi = h2.astype(BF16)
            hi_scr[r0:r0 + ROUTE_CHUNK, :] = hi
            lo_scr[r0:r0 + ROUTE_CHUNK, :] = (h2 - hi.astype(F32)).astype(BF16)

    pl.when(i < CTX_ROUTE_BLOCKS)(lambda: norm_rows(xp_ref))
    pl.when(i >= CTX_ROUTE_BLOCKS)(lambda: norm_rows(xs_ref))

    bucket = _router_bucket(hi_scr[...], lo_scr[...], wr_hi_ref, wr_lo_ref, br_ref, n)
    onehot = lax.broadcasted_iota(jnp.int32, (BUCKET_ROWS, n), 0) == bucket
    upper = (lax.broadcasted_iota(jnp.int32, (n, n), 0) <= lax.broadcasted_iota(jnp.int32, (n, n), 1))
    incl = jnp.dot(onehot.astype(BF16), upper.astype(BF16), preferred_element_type=F32)
    run = run_scr[...]
    rank = jnp.sum(jnp.where(onehot, incl + run, 0.0), axis=0, keepdims=True) - 1.0
    run = run + jnp.sum(onehot.astype(F32), axis=1, keepdims=True)
    run_scr[...] = run
    meta_ref[...] = jnp.concatenate(
        [bucket, rank.astype(jnp.int32), jnp.zeros((META_ROWS - 2, n), jnp.int32)], axis=0)
    count_ref[...] = jnp.broadcast_to(run, count_ref.shape).astype(jnp.int32)


def _route(x1p, x1s, mod_l, n2g, wr_hi, wr_lo, br):
    per_seq = DEC_SEQ // ROUTE_TM
    last_ctx = CTX_ROUTE_BLOCKS - 1
    return pl.pallas_call(
        _route_kernel,
        grid=(N_TOKENS // ROUTE_TM,),
        in_specs=[
            pl.BlockSpec((ROUTE_TM, D_MODEL), lambda i: (jnp.minimum(i, last_ctx), 0)),
            pl.BlockSpec((ROUTE_TM, D_MODEL), lambda i: (jnp.maximum(i - CTX_ROUTE_BLOCKS, 0), 0)),
            pl.BlockSpec((None, 6, D_MODEL),
                         lambda i: (jnp.where(i < CTX_ROUTE_BLOCKS, 0, (i - CTX_ROUTE_BLOCKS) // per_seq + 1), 0, 0)),
            pl.BlockSpec((1, D_MODEL), lambda i: (0, 0)),
            pl.BlockSpec((D_MODEL, LANES), lambda i: (0, 0)),
            pl.BlockSpec((D_MODEL, LANES), lambda i: (0, 0)),
            pl.BlockSpec((N_EXPERTS, 1), lambda i: (0, 0)),
        ],
        out_specs=[pl.BlockSpec((ROUTE_TM * ROW_SLABS, LANES), lambda i: (i, 0)),
                   pl.BlockSpec((META_ROWS, ROUTE_TM), lambda i: (0, i)),
                   pl.BlockSpec((BUCKET_ROWS, LANES), lambda i: (0, 0))],
        out_shape=[jax.ShapeDtypeStruct((N_TOKENS * ROW_SLABS, LANES), F32),
                   jax.ShapeDtypeStruct((META_ROWS, N_TOKENS), jnp.int32),
                   jax.ShapeDtypeStruct((BUCKET_ROWS, LANES), jnp.int32)],
        scratch_shapes=[pltpu.VMEM((BUCKET_ROWS, 1), F32),
                        pltpu.VMEM((ROUTE_TM, D_MODEL), BF16),
                        pltpu.VMEM((ROUTE_TM, D_MODEL), BF16)],
        compiler_params=pltpu.CompilerParams(dimension_semantics=("arbitrary",)),
        name="norm_route",
    )(x1p, x1s, mod_l, n2g, wr_hi, wr_lo, br)


def _const_spec(shape):
    nd = len(shape)
    return pl.BlockSpec(shape, lambda b: (0,) * nd, pipeline_mode=pl.Buffered(1))


def _stream_specs(latent):
    n = DEC_SEQ if latent else SEQ
    first = BATCH * SEQ // n if latent else 0
    own = pl.BlockSpec((n, D_MODEL), lambda b: (b, 0))
    rows = pl.BlockSpec((n * ROW_SLABS, LANES), lambda b: (first + b, 0))
    mod = pl.BlockSpec((None, 6, D_MODEL), (lambda b: (b + 1, 0, 0)) if latent else (lambda b: (0, 0, 0)))
    return n, own, rows, mod


def _mixer(x, mod_l, weights, *, latent, prev=None, kv_old=None, cache=None, bias=None, layer=0):
    n, own_spec, rows_spec, mod_spec = _stream_specs(latent)
    tok = x.shape[0]
    nb = tok // n
    in_specs = [own_spec, mod_spec]
    args = [x, mod_l]
    if prev is not None:
        in_specs += [rows_spec, mod_spec]
        args += list(prev)
        if not latent:
            in_specs += [pl.BlockSpec((None, layer, n, D_ATTN), lambda b: (b, 0, 0, 0))] * 2
            args += list(kv_old)
    if latent:
        cache_spec = pl.BlockSpec((None, None, PAST_LEN, D_ATTN), lambda b: (b, layer, 0, 0),
                                  pipeline_mode=pl.Buffered(1))
        in_specs += [cache_spec, cache_spec, _const_spec(bias.shape)]
        args += [cache[0], cache[1], bias]
    in_specs += [_const_spec(w.shape) for w in weights]
    args += list(weights)
    out_shape = [jax.ShapeDtypeStruct((tok, D_MODEL), F32)]
    out_specs = [own_spec]
    if not latent:
        out_shape += [jax.ShapeDtypeStruct((nb, layer + 1, n, D_ATTN), F32)] * 2
        out_specs += [pl.BlockSpec((None, layer + 1, n, D_ATTN), lambda b: (b, 0, 0, 0))] * 2
    scratch = [pltpu.VMEM((3 * N_PAIRS, n, LANES), BF16),
               pltpu.VMEM((3, n, D_POOL), F32),
               pltpu.VMEM((ROW_SLABS, n, LANES), BF16)]
    if latent:
        scratch += [pltpu.VMEM((N_PAIRS, PAST_LEN, LANES), BF16),
                    pltpu.VMEM((N_PAIRS, PAST_LEN, LANES), BF16)]
    return pl.pallas_call(
        functools.partial(_mixer_kernel, latent=latent, has_prev=prev is not None, n=n),
        grid=(nb,),
        in_specs=in_specs,
        out_specs=out_specs,
        out_shape=out_shape,
        scratch_shapes=scratch,
        compiler_params=pltpu.CompilerParams(dimension_semantics=("arbitrary",),
                                             vmem_limit_bytes=VMEM_LIMIT),
        name="mixer_latent" if latent else "mixer_context",
    )(*args)


def _bucket_tables():
    lo, hi = [], []
    for g in range(N_GROUPS):
        for a in range(GROUP_SIZE):
            for b in range(a + 1, GROUP_SIZE):
                lo.append(g * GROUP_SIZE + a)
                hi.append(g * GROUP_SIZE + b)
    return np.asarray(lo, np.int32), np.asarray(hi, np.int32)


def _route_plan(bucket, rank, count):
    tiles_b = (count + MOE_TM - 1) // MOE_TM
    tile_end = jnp.cumsum(tiles_b)
    tile_start = tile_end - tiles_b
    pos = tile_start[bucket] * MOE_TM + rank
    n_used = tile_end[-1]
    fill_lo = jnp.concatenate([tile_start * MOE_TM + count, (n_used * MOE_TM)[None]])
    fill_hi = jnp.concatenate([tile_end * MOE_TM, jnp.full((1,), MOE_TILES * MOE_TM, jnp.int32)])
    tile = jnp.arange(MOE_TILES, dtype=jnp.int32)
    tile_b = jnp.sum((tile[:, None] >= tile_end[None, :]).astype(jnp.int32), axis=1)
    last_b = jnp.sum((n_used - 1 >= tile_end).astype(jnp.int32))
    tile_b = jnp.minimum(tile_b, last_b)
    n_valid = jnp.clip(count[tile_b] - (tile - tile_start[tile_b]) * MOE_TM, 0, MOE_TM)
    lo_tab, hi_tab = _bucket_tables()
    i32 = lambda a: a.astype(jnp.int32)
    return (i32(pos), i32(fill_lo), i32(fill_hi)), (
        jnp.asarray(lo_tab)[tile_b], jnp.asarray(hi_tab)[tile_b], i32(n_valid), i32(n_used).reshape(1))


def _permute_kernel(pos_ref, lo_ref, hi_ref, h_hbm, hs_hbm, tok_ref, buf, lsem, ssem, fsem):
    i = pl.program_id(0)
    last = pl.num_programs(0) - 1
    half = i % 2
    blk = PERM_TM * ROW_SLABS

    def load(block, h):
        return pltpu.make_async_copy(h_hbm.at[pl.ds(block * blk, blk), :], buf.at[pl.ds(h * blk, blk), :], lsem.at[h])

    def wait_rows(sem, rows):
        n = rows * ROW_SLABS
        pltpu.make_async_copy(hs_hbm.at[pl.ds(0, n), :], hs_hbm.at[pl.ds(0, n), :], sem).wait()

    @pl.when(i == 0)
    def _():
        load(0, 0).start()

    load(i, half).wait()

    @pl.when(i >= 1)
    def _():
        wait_rows(ssem.at[1 - half], PERM_TM)

    @pl.when(i < last)
    def _():
        load(i + 1, 1 - half).start()

    base = i * PERM_TM
    for r in range(PERM_TM):
        p = pos_ref[base + r]
        pltpu.make_async_copy(_row_slabs(buf, half * PERM_TM + r), _row_slabs(hs_hbm, p),
                              ssem.at[half]).start(priority=r % 2)
        tok_ref[p] = base + r

    @pl.when(i == 0)
    def _():
        def fill(first, rows):
            n = rows * ROW_SLABS
            pltpu.make_async_copy(buf.at[pl.ds(0, n), :],
                                  hs_hbm.at[pl.ds(pl.multiple_of(first * ROW_SLABS, ROW_SLABS), n), :], fsem).start()

        def zero_tok(p, carry):
            tok_ref[p] = 0
            return carry

        for b in range(N_BUCKETS + 1):
            lax.fori_loop(lo_ref[b], hi_ref[b], zero_tok, 0)
        for b in range(N_BUCKETS):
            first = lo_ref[b]
            length = hi_ref[b] - first
            run = MOE_TM // 2
            while run >= 1:
                take = (length & run) != 0
                pl.when(take)(functools.partial(fill, first, run))
                first = first + jnp.where(take, run, 0)
                run //= 2

        def fill_tile(t, carry):
            fill(t * MOE_TM, MOE_TM)
            return carry

        lax.fori_loop(lo_ref[N_BUCKETS] // MOE_TM, MOE_TILES, fill_tile, 0)
        wait_rows(fsem, MOE_TILES * MOE_TM - N_TOKENS)

    @pl.when(i == last)
    def _():
        wait_rows(ssem.at[half], PERM_TM)


def _permute(h, pos, fill_lo, fill_hi):
    grid_spec = pltpu.PrefetchScalarGridSpec(
        num_scalar_prefetch=3,
        grid=(N_TOKENS // PERM_TM,),
        in_specs=[pl.BlockSpec(memory_space=pl.ANY)],
        out_specs=[pl.BlockSpec(memory_space=pl.ANY), pl.BlockSpec(memory_space=pltpu.SMEM)],
        scratch_shapes=[pltpu.VMEM((2 * PERM_TM * ROW_SLABS, LANES), F32),
                        pltpu.SemaphoreType.DMA((2,)),
                        pltpu.SemaphoreType.DMA((2,)),
                        pltpu.SemaphoreType.DMA(())],
    )
    return pl.pallas_call(
        _permute_kernel,
        grid_spec=grid_spec,
        out_shape=[jax.ShapeDtypeStruct((MOE_TILES * MOE_TM * ROW_SLABS, LANES), F32),
                   jax.ShapeDtypeStruct((MOE_TILES * MOE_TM,), jnp.int32)],
        compiler_params=pltpu.CompilerParams(dimension_semantics=("arbitrary",)),
        name="permute_rows",
    )(pos, fill_lo, fill_hi, h)


def _row_slabs(ref, row):
    return ref.at[pl.ds(pl.multiple_of(row * ROW_SLABS, ROW_SLABS), ROW_SLABS), :]


def _moe_kernel(e0_ref, e1_ref, nv_ref, nu_ref, tok_ref, hs_ref, wrt_ref, wg0_ref, wu0_ref, wd0_ref,
                wg1_ref, wu1_ref, wd1_ref, y_hbm, xb_scr, stg, ssem):
    i = pl.program_id(0)
    n_used = nu_ref[0]

    def wait_scatter(rows):
        n = rows * ROW_SLABS
        pltpu.make_async_copy(y_hbm.at[pl.ds(0, n), :], y_hbm.at[pl.ds(0, n), :], ssem).wait()

    @pl.when(i < n_used)
    def _():
        w0 = wrt_ref[pl.ds(e0_ref[i], 1), :]
        w1 = wrt_ref[pl.ds(e1_ref[i], 1), :]
        l0 = jnp.zeros((MOE_TM, LANES), F32)
        l1 = jnp.zeros((MOE_TM, LANES), F32)
        for j in range(ROW_SLABS):
            xj = hs_ref[_slab(j, MOE_TM), :]
            xb_scr[:, j * LANES:(j + 1) * LANES] = xj.astype(BF16)
            l0 = l0 + xj * w0[:, j * LANES:(j + 1) * LANES]
            l1 = l1 + xj * w1[:, j * LANES:(j + 1) * LANES]
        a0 = jax.nn.sigmoid(jnp.sum(l0, axis=1, keepdims=True))
        a1 = jax.nn.sigmoid(jnp.sum(l1, axis=1, keepdims=True))
        den = a0 + a1
        acc = None
        for gate, wg_ref, wu_ref, wd_ref in ((a0 / den, wg0_ref, wu0_ref, wd0_ref),
                                             (a1 / den, wg1_ref, wu1_ref, wd1_ref)):
            a = jnp.dot(xb_scr[...], wg_ref[...], preferred_element_type=F32)
            b = jnp.dot(xb_scr[...], wu_ref[...], preferred_element_type=F32)
            hid = (_silu(a) * b * gate).astype(BF16)
            o = jnp.dot(hid, wd_ref[...], preferred_element_type=F32)
            acc = o if acc is None else acc + o

        @pl.when(i >= 1)
        def _():
            wait_scatter(nv_ref[jnp.maximum(i - 1, 0)])

        _store_slabs(stg, acc, MOE_TM)
        base = i * MOE_TM

        def scatter_row(r, priority):
            pltpu.make_async_copy(_row_slabs(stg, r), _row_slabs(y_hbm, tok_ref[base + r]),
                                  ssem).start(priority=priority)

        def scatter_rows(c, carry):
            for u in range(DMA_UNROLL):
                scatter_row(c * DMA_UNROLL + u, u % 2)
            return carry

        def scatter_one(r, carry):
            scatter_row(r, 0)
            return carry

        n_full = nv_ref[i] // DMA_UNROLL
        lax.fori_loop(0, n_full, scatter_rows, 0)
        lax.fori_loop(n_full * DMA_UNROLL, nv_ref[i], scatter_one, 0)

        @pl.when(i == n_used - 1)
        def _():
            wait_scatter(nv_ref[i])


def _moe(hs, row_tok, tiles, wrt, wg, wu, wd, layer):
    e0, e1, n_valid, n_used = tiles

    def w_spec(which, shape):
        return pl.BlockSpec((None, None) + shape,
                            lambda i, e0, e1, nv, nu, tok: (layer, (e0, e1)[which][i], 0, 0))

    up_shape, down_shape = (D_MODEL, D_FF), (D_FF, D_MODEL)
    grid_spec = pltpu.PrefetchScalarGridSpec(
        num_scalar_prefetch=5,
        grid=(MOE_TILES,),
        in_specs=[pl.BlockSpec((MOE_TM * ROW_SLABS, LANES),
                               lambda i, e0, e1, nv, nu, tok: (jnp.minimum(i, nu[0] - 1), 0)),
                  pl.BlockSpec((N_EXPERTS, D_MODEL), lambda i, *_: (0, 0)),
                  w_spec(0, up_shape), w_spec(0, up_shape), w_spec(0, down_shape),
                  w_spec(1, up_shape), w_spec(1, up_shape), w_spec(1, down_shape)],
        out_specs=pl.BlockSpec(memory_space=pl.ANY),
        scratch_shapes=[pltpu.VMEM((MOE_TM, D_MODEL), BF16),
                        pltpu.VMEM((MOE_TM * ROW_SLABS, LANES), F32),
                        pltpu.SemaphoreType.DMA(())],
    )
    return pl.pallas_call(
        _moe_kernel,
        grid_spec=grid_spec,
        out_shape=jax.ShapeDtypeStruct((N_TOKENS * ROW_SLABS, LANES), F32),
        compiler_params=pltpu.CompilerParams(dimension_semantics=("arbitrary",),
                                             vmem_limit_bytes=VMEM_LIMIT),
        name="moe_routed",
    )(e0, e1, n_valid, n_used, row_tok, hs, wrt, wg, wu, wd, wg, wu, wd)


def _final_kernel(x1_ref, y_ref, mod_ref, fg_ref, o_ref):
    x2 = x1_ref[...] + mod_ref[5:6, :] * _rows_to_2d(y_ref, x1_ref.shape[0])
    o_ref[...] = _rms(x2, fg_ref[...])


def _final(x1, y, mod_l, fg, *, latent):
    n, own_spec, rows_spec, mod_spec = _stream_specs(latent)
    return pl.pallas_call(
        _final_kernel,
        grid=(x1.shape[0] // n,),
        in_specs=[own_spec, rows_spec, mod_spec, pl.BlockSpec((1, D_MODEL), lambda b: (0, 0))],
        out_specs=own_spec,
        out_shape=jax.ShapeDtypeStruct(x1.shape, F32),
        name="final_norm",
    )(x1, y, mod_l, fg)


def kernel(x_prompt, x_sample, cache_k, cache_v, c, c_ctx, w_ada, b_ada, norm1_g, norm2_g, w_in, w_out, rpb, pool_w, pool_scale, conv_w, conv_b, conv_ln_g, conv_ln_b, conv_pw_w, conv_pw_b, w_router, b_router, moe_w_gate, moe_w_up, moe_w_down, final_g):
    xp = x_prompt.reshape(BATCH * SEQ, D_MODEL)
    xs = x_sample.reshape(DEC_BATCH * DEC_SEQ, D_MODEL)
    cond = jnp.concatenate([c_ctx[None, :], c, jnp.zeros((N_MOD_ROWS - 1 - DEC_BATCH, D_MODEL), F32)], axis=0)
    mod = _ada_modulation(cond, w_ada, b_ada).reshape(DEPTH, N_MOD_ROWS, 6, D_MODEL)
    bias = _rpb_tiles(rpb)
    ck = cache_k.reshape(DEC_BATCH, DEPTH, PAST_LEN, D_ATTN)
    cv = cache_v.reshape(DEC_BATCH, DEPTH, PAST_LEN, D_ATTN)
    wrt = w_router.T
    wr = jnp.pad(w_router, ((0, 0), (0, LANES - N_EXPERTS)))
    wr_hi = wr.astype(BF16)
    wr_lo = (wr - wr_hi.astype(F32)).astype(BF16)
    br = b_router.reshape(N_EXPERTS, 1)
    fg = final_g.reshape(1, D_MODEL)
    wg, wu, wd = moe_w_gate.astype(BF16), moe_w_up.astype(BF16), moe_w_down.astype(BF16)
    new_k = new_v = None
    prev = None
    for l in range(DEPTH):
        pool_bd = jnp.zeros((D_POOL, D_POOL), F32)
        gd = D_POOL // 4
        for g in range(4):
            pool_bd = pool_bd.at[g * gd:(g + 1) * gd, g * gd:(g + 1) * gd].set(pool_w[l, g])
        weights = [norm1_g[l].reshape(1, -1),
                   w_in[l].astype(BF16), w_out[l].astype(BF16),
                   pool_bd.astype(BF16), pool_scale[l].reshape(1, -1),
                   conv_w[l], conv_b[l].reshape(1, -1),
                   conv_ln_g[l].reshape(1, -1), conv_ln_b[l].reshape(1, -1),
                   conv_pw_w[l].astype(BF16), conv_pw_b[l].reshape(1, -1)]

        xp, new_k, new_v = _mixer(xp, mod[l], weights, latent=False, prev=prev, kv_old=(new_k, new_v), layer=l)
        xs, = _mixer(xs, mod[l], weights, latent=True, prev=prev, cache=(ck, cv), bias=bias[l], layer=l)

        h, meta, count = _route(xp, xs, mod[l], norm2_g[l].reshape(1, -1), wr_hi, wr_lo, br)
        perm, tiles = _route_plan(meta[0], meta[1], count[:N_BUCKETS, 0])
        hs, row_tok = _permute(h, *perm)
        y = _moe(hs, row_tok, tiles, wrt, wg, wu, wd, l)
        prev = (y, mod[l])

    y_prompt = _final(xp, y, mod[DEPTH - 1], fg, latent=False).reshape(BATCH, SEQ, D_MODEL)
    y_sample = _final(xs, y, mod[DEPTH - 1], fg, latent=True).reshape(DEC_BATCH, DEC_SEQ, D_MODEL)
    new_cache_k = new_k.reshape(BATCH, DEPTH, SEQ, N_HEADS, HEAD_DIM)
    new_cache_v = new_v.reshape(BATCH, DEPTH, SEQ, N_HEADS, HEAD_DIM)
    return (y_prompt, y_sample, new_cache_k, new_cache_v)
```
